```python
import jax, jax.numpy as jnp
from jax import lax
import numpy as np

D_MODEL = 1024
BATCH = 1
SEQ = 16384
DEPTH = 1

N_HEADS = 8
HEAD_DIM = 64
ATTN_WIDTH = N_HEADS * HEAD_DIM
IDX_HEADS = 4
IDX_DIM = 64
TOPK_MAX = 256
Q_BLOCK = 128
POOL_WINDOWS = (2, 4, 8, 16)
POOL_GROUPS = 4
POOL_GROUP_DIM = 128
POOL_WIDTH = POOL_GROUPS * POOL_GROUP_DIM
D_FF = 2816
CONV_WIDTH = 3
EPS = 1e-6
N_MOD = 6
IN_SIZES = (ATTN_WIDTH, ATTN_WIDTH, ATTN_WIDTH,
            IDX_HEADS * IDX_DIM, IDX_DIM, IDX_HEADS,
            POOL_WIDTH,
            D_MODEL, D_MODEL)
IN_TOTAL = sum(IN_SIZES)

kernel_name = "hybrid_dsa_pool_convffn_block"


def split_cols(a, sizes):
    outs = []
    off = 0
    for s in sizes:
        outs.append(a[..., off:off + s])
        off += s
    return outs


def rms_norm(x, g):
    xf = x.astype(jnp.float32)
    y = xf * lax.rsqrt(jnp.mean(xf * xf, axis=-1, keepdims=True) + EPS)
    return (y * g.astype(jnp.float32)).astype(x.dtype)


def modulate(x, g, shift, scale):
    return rms_norm(x, g) * (1.0 + scale[:, None, :]) + shift[:, None, :]


def alibi_slopes(n_heads):
    return 2.0 ** (-8.0 * jnp.arange(1, n_heads + 1, dtype=jnp.float32) / n_heads)


def dsa_sparse_attention(q, k, v, qi, ki, wi):
    B, S, H, Dh = q.shape
    topk = min(TOPK_MAX, S // 4)
    nb = S // Q_BLOCK
    slopes = alibi_slopes(H)
    key_pos = jnp.arange(S, dtype=jnp.int32)
    scale = HEAD_DIM ** -0.5

    def to_blocks(a):
        return a.reshape((B, nb, Q_BLOCK) + a.shape[2:]).swapaxes(0, 1)

    def block_fn(args):
        qb, qib, wib, bidx = args
        qpos = bidx * Q_BLOCK + jnp.arange(Q_BLOCK, dtype=jnp.int32)
        ilog = jnp.einsum('bqhd,bsd->bqhs', qib, ki).astype(jnp.float32)
        iscore = jnp.einsum('bqhs,bqh->bqs', jax.nn.relu(ilog), wib.astype(jnp.float32))
        causal = key_pos[None, :] <= qpos[:, None]
        iscore = jnp.where(causal[None], iscore, -jnp.inf)
        _, sel = lax.top_k(iscore, topk)
        kg = jax.vmap(lambda kk, ii: kk[ii])(k, sel)
        vg = jax.vmap(lambda vv, ii: vv[ii])(v, sel)
        s = jnp.einsum('bqhd,bqkhd->bhqk', qb, kg).astype(jnp.float32) * scale
        dist = (qpos[None, :, None] - sel).astype(jnp.float32)
        s = s - slopes[None, :, None, None] * dist[:, None]
        valid = sel <= qpos[None, :, None]
        s = jnp.where(valid[:, None], s, -jnp.inf)
        p = jax.nn.softmax(s, axis=-1).astype(vg.dtype)
        return jnp.einsum('bhqk,bqkhd->bqhd', p, vg)

    out = lax.map(block_fn, (to_blocks(q), to_blocks(qi), to_blocks(wi),
                             jnp.arange(nb, dtype=jnp.int32)))
    return out.swapaxes(0, 1).reshape(B, S, H, Dh)


def multiscale_pool(u, w_grp, pool_scale):
    B, S, P = u.shape
    uf = u.astype(jnp.float32)
    cs = jnp.pad(jnp.cumsum(uf, axis=1), ((0, 0), (1, 0), (0, 0)))
    t = jnp.arange(S, dtype=jnp.float32)
    groups = []
    for g, w in enumerate(POOL_WINDOWS):
        c_g = cs[:, :, g * POOL_GROUP_DIM:(g + 1) * POOL_GROUP_DIM]
        upper = c_g[:, 1:]
        lower = jnp.pad(c_g, ((0, 0), (w - 1, 0), (0, 0)))[:, :S]
        count = jnp.minimum(t + 1.0, float(w))[None, :, None]
        mean = (upper - lower) / count
        groups.append(mean - uf[:, :, g * POOL_GROUP_DIM:(g + 1) * POOL_GROUP_DIM])
    pooled = jnp.stack(groups, axis=2).astype(u.dtype)
    mixed = jnp.einsum('bsgc,gcd->bsgd', pooled, w_grp).reshape(B, S, P)
    return mixed * pool_scale


def conv_ffn(h, w_up, conv_w, conv_b, w_down):
    S = h.shape[1]
    up = h @ w_up
    padded = jnp.pad(up, ((0, 0), (CONV_WIDTH - 1, 0), (0, 0)))
    y = conv_b + sum(conv_w[j] * padded[:, j:j + S] for j in range(CONV_WIDTH))
    a, b = jnp.split(y, 2, axis=-1)
    return (jax.nn.silu(a) * b) @ w_down


def setup_inputs(seed: int = 0) -> dict:
    key = jax.random.key(seed)
    ks = jax.random.split(key, 20)
    f = jnp.float32
    D = D_MODEL
    nrm = lambda k, shape, fan: jax.random.normal(k, shape, f) * (fan ** -0.5)
    return {
        "x": jax.random.normal(ks[0], (BATCH, SEQ, D), f),
        "c": jax.random.normal(ks[1], (BATCH, D), f),
        "w_ada": nrm(ks[2], (DEPTH, D, N_MOD * D), D) * 0.5,
        "b_ada": jax.random.normal(ks[3], (DEPTH, N_MOD * D), f) * 0.02,
        "norm1_g": 1.0 + 0.05 * jax.random.normal(ks[4], (DEPTH, D), f),
        "w_in": nrm(ks[5], (DEPTH, D, IN_TOTAL), D),
        "q_norm_g": 1.0 + 0.05 * jax.random.normal(ks[6], (DEPTH, HEAD_DIM), f),
        "k_norm_g": 1.0 + 0.05 * jax.random.normal(ks[7], (DEPTH, HEAD_DIM), f),
        "w_attn_br": nrm(ks[8], (DEPTH, ATTN_WIDTH, D), ATTN_WIDTH),
        "w_pool_grp": nrm(ks[9], (DEPTH, POOL_GROUPS, POOL_GROUP_DIM, POOL_GROUP_DIM), POOL_GROUP_DIM),
        "pool_scale": 1.0 + 0.1 * jax.random.normal(ks[10], (DEPTH, POOL_WIDTH), f),
        "w_pool_br": nrm(ks[11], (DEPTH, POOL_WIDTH, D), POOL_WIDTH),
        "w_out": nrm(ks[12], (DEPTH, D, D), D),
        "norm2_g": 1.0 + 0.05 * jax.random.normal(ks[13], (DEPTH, D), f),
        "w_up": nrm(ks[14], (DEPTH, D, 2 * D_FF), D),
        "conv_w": nrm(ks[15], (DEPTH, CONV_WIDTH, 2 * D_FF), CONV_WIDTH),
        "conv_b": jax.random.normal(ks[16], (DEPTH, 2 * D_FF), f) * 0.02,
        "w_down": nrm(ks[17], (DEPTH, D_FF, D), D_FF),
    }


def reference(x, c, w_ada, b_ada, norm1_g, w_in, q_norm_g, k_norm_g, w_attn_br,
              w_pool_grp, pool_scale, w_pool_br, w_out, norm2_g, w_up, conv_w,
              conv_b, w_down):
    B, S, D = x.shape
    for l in range(DEPTH):
        mod = jax.nn.silu(c) @ w_ada[l] + b_ada[l]
        shift1, scale1, gate1, shift2, scale2, gate2 = jnp.split(mod, N_MOD, axis=-1)

        h = modulate(x, norm1_g[l], shift1, scale1)
        proj = h @ w_in[l]
        q, k, v, qi, ki, wi, u_pool, g_attn, g_pool = split_cols(proj, IN_SIZES)
        q = rms_norm(q.reshape(B, S, N_HEADS, HEAD_DIM), q_norm_g[l])
        k = rms_norm(k.reshape(B, S, N_HEADS, HEAD_DIM), k_norm_g[l])
        v = v.reshape(B, S, N_HEADS, HEAD_DIM)
        qi = qi.reshape(B, S, IDX_HEADS, IDX_DIM) * (IDX_DIM ** -0.5)
        wi = wi * (IDX_HEADS ** -0.5)
        attn = dsa_sparse_attention(q, k, v, qi, ki, wi).reshape(B, S, ATTN_WIDTH)
        y_attn = attn @ w_attn_br[l]
        y_pool = multiscale_pool(u_pool, w_pool_grp[l], pool_scale[l]) @ w_pool_br[l]
        merged = jax.nn.sigmoid(g_attn) * y_attn + jax.nn.sigmoid(g_pool) * y_pool
        x = x + gate1[:, None, :] * (merged @ w_out[l])

        h2 = modulate(x, norm2_g[l], shift2, scale2)
        x = x + gate2[:, None, :] * conv_ffn(h2, w_up[l], conv_w[l], conv_b[l], w_down[l])
    return x
```

```python
import functools
import math

import jax
import jax.numpy as jnp
from jax import lax
from jax.experimental import pallas as pl
from jax.experimental.pallas import tpu as pltpu

f32 = jnp.float32
bf16 = jnp.bfloat16
i32 = jnp.int32

D_MODEL = 1024
N_HEADS = 8
HEAD_DIM = 64
ATTN_WIDTH = N_HEADS * HEAD_DIM
IDX_HEADS = 4
IDX_DIM = 64
TOPK_MAX = 256
POOL_WINDOWS = (2, 4, 8, 16)
POOL_GROUP_DIM = 128
POOL_WIDTH = 512
D_FF = 2816
EPS = 1e-6
N_MOD = 6

LANE = 128
SUBLANE = 8
VMEM_LIMIT = 56 * 1024 * 1024

LOG2E = 1.4426950408889634
INT_MIN = -2147483648
NEG_INF = float("-inf")

C_Q, C_K, C_V, C_QI, C_KI, C_WI, C_U, C_GA, C_GP, C_END = (
    0, 512, 1024, 1536, 1792, 1920, 2048, 2560, 3584, 4608)

TQ = 128
TK = 256
N_PAIR = N_HEADS // 2


def _nt_dot(a, b):
    return lax.dot_general(a, b, (((1,), (1,)), ((), ())), preferred_element_type=f32)


def _sigmoid(x):
    return 1.0 / (1.0 + jnp.exp(-x))


def _rms_modulate(x, g, shift, scale):
    y = x * lax.rsqrt(jnp.mean(x * x, axis=-1, keepdims=True) + EPS)
    return (y * g) * (1.0 + scale) + shift


def _mod_kernel(c_ref, w_ref, b_ref, o_ref):
    c = c_ref[...]
    sc = c * _sigmoid(c)
    o_ref[...] = jnp.dot(sc, w_ref[...], precision=lax.Precision.HIGHEST,
                         preferred_element_type=f32) + b_ref[...]


def _mod_call(c8, w_ada, b_ada):
    n = w_ada.shape[1]
    tn = 1024
    return pl.pallas_call(
        _mod_kernel,
        grid=(n // tn,),
        in_specs=[pl.BlockSpec((SUBLANE, D_MODEL), lambda j: (0, 0)),
                  pl.BlockSpec((D_MODEL, tn), lambda j: (0, j)),
                  pl.BlockSpec((1, tn), lambda j: (0, j))],
        out_specs=pl.BlockSpec((SUBLANE, tn), lambda j: (0, j)),
        out_shape=jax.ShapeDtypeStruct((SUBLANE, n), f32),
        name="mod",
    )(c8, w_ada, b_ada)


def _head_norm(z, g, bd):
    z2 = z * z
    hi = z2.astype(bf16)
    lo = (z2 - hi.astype(f32)).astype(bf16)
    ms = jnp.dot(hi, bd, preferred_element_type=f32) + jnp.dot(lo, bd, preferred_element_type=f32)
    return (z * lax.rsqrt(ms + EPS)) * g


def _proj_kernel(x_ref, mod_ref, g1_ref, w_ref, qg_ref, kg_ref,
                 qbd_ref, k_ref, vt_ref, qi_ref, ki_ref, wit_ref, u_ref, ga_ref, gp_ref, *, tm):
    x = x_ref[...]
    shift = mod_ref[0:1, 0:D_MODEL]
    scale = mod_ref[0:1, D_MODEL:2 * D_MODEL]
    h = _rms_modulate(x, g1_ref[...], shift, scale)
    proj = jnp.dot(h.astype(bf16), w_ref[...], preferred_element_type=f32)

    r = lax.broadcasted_iota(i32, (ATTN_WIDTH, ATTN_WIDTH), 0)
    c = lax.broadcasted_iota(i32, (ATTN_WIDTH, ATTN_WIDTH), 1)
    bd = jnp.where((r >> 6) == (c >> 6), 1.0 / HEAD_DIM, 0.0).astype(bf16)

    q = _head_norm(proj[:, C_Q:C_K], qg_ref[...], bd) * (HEAD_DIM ** -0.5 * LOG2E)
    k = _head_norm(proj[:, C_K:C_V], kg_ref[...], bd)
    k_ref[...] = k.astype(bf16)

    lane = lax.broadcasted_iota(i32, (TQ, LANE), 1)
    low = lane < HEAD_DIM
    qi = proj[:, C_QI:C_KI] * (IDX_DIM ** -0.5)
    for g in range(tm // TQ):
        rows = slice(g * TQ, (g + 1) * TQ)
        for p in range(N_PAIR):
            qp = q[rows, p * LANE:(p + 1) * LANE]
            qbd_ref[g, p, 0:TQ, :] = jnp.where(low, qp, 0.0).astype(bf16)
            qbd_ref[g, p, TQ:2 * TQ, :] = jnp.where(low, 0.0, qp).astype(bf16)
        for hh in range(IDX_HEADS):
            seg = qi[rows, (hh // 2) * LANE:(hh // 2 + 1) * LANE]
            keep = low if hh % 2 == 0 else jnp.logical_not(low)
            qi_ref[g, hh * TQ:(hh + 1) * TQ, :] = jnp.where(keep, seg, 0.0).astype(bf16)

    v = proj[:, C_V:C_QI]
    vt = v.T.astype(bf16)
    for cc in range(tm // TK):
        vt_ref[cc] = vt[:, cc * TK:(cc + 1) * TK]

    ki_ref[...] = proj[:, C_KI:C_WI].astype(bf16)
    wt = (proj[:, C_WI:C_U] * (IDX_HEADS ** -0.5)).T
    wit_ref[...] = wt[0:SUBLANE, :]
    u_ref[...] = proj[:, C_U:C_GA]
    ga_ref[...] = _sigmoid(proj[:, C_GA:C_GP]).astype(bf16)
    gp_ref[...] = _sigmoid(proj[:, C_GP:C_END]).astype(bf16)


def _proj_call(x2, mod, g1, w_in_p, qg, kg, tm):
    s = x2.shape[0]
    nq = s // TQ
    const = lambda i: (0, 0)
    return pl.pallas_call(
        functools.partial(_proj_kernel, tm=tm),
        grid=(s // tm,),
        in_specs=[pl.BlockSpec((tm, D_MODEL), lambda i: (i, 0)),
                  pl.BlockSpec((SUBLANE, N_MOD * D_MODEL), const),
                  pl.BlockSpec((1, D_MODEL), const),
                  pl.BlockSpec((D_MODEL, C_END), const),
                  pl.BlockSpec((1, ATTN_WIDTH), const),
                  pl.BlockSpec((1, ATTN_WIDTH), const)],
        out_specs=[pl.BlockSpec((tm // TQ, N_PAIR, 2 * TQ, LANE), lambda i: (i, 0, 0, 0)),
                   pl.BlockSpec((tm, ATTN_WIDTH), lambda i: (i, 0)),
                   pl.BlockSpec((tm // TK, ATTN_WIDTH, TK), lambda i: (i, 0, 0)),
                   pl.BlockSpec((tm // TQ, IDX_HEADS * TQ, LANE), lambda i: (i, 0, 0)),
                   pl.BlockSpec((tm, LANE), lambda i: (i, 0)),
                   pl.BlockSpec((SUBLANE, tm), lambda i: (0, i)),
                   pl.BlockSpec((tm, POOL_WIDTH), lambda i: (i, 0)),
                   pl.BlockSpec((tm, D_MODEL), lambda i: (i, 0)),
                   pl.BlockSpec((tm, D_MODEL), lambda i: (i, 0))],
        out_shape=[jax.ShapeDtypeStruct((nq, N_PAIR, 2 * TQ, LANE), bf16),
                   jax.ShapeDtypeStruct((s, ATTN_WIDTH), bf16),
                   jax.ShapeDtypeStruct((s // TK, ATTN_WIDTH, TK), bf16),
                   jax.ShapeDtypeStruct((nq, IDX_HEADS * TQ, LANE), bf16),
                   jax.ShapeDtypeStruct((s, LANE), bf16),
                   jax.ShapeDtypeStruct((SUBLANE, s), f32),
                   jax.ShapeDtypeStruct((s, POOL_WIDTH), f32),
                   jax.ShapeDtypeStruct((s, D_MODEL), bf16),
                   jax.ShapeDtypeStruct((s, D_MODEL), bf16)],
        compiler_params=pltpu.CompilerParams(dimension_semantics=("parallel",),
                                             vmem_limit_bytes=VMEM_LIMIT),
        name="proj",
    )(x2, mod, g1, w_in_p, qg, kg)


def _slope2(h):
    return (2.0 ** (-8.0 * (h + 1) / N_HEADS)) * LOG2E


def _attn_kernel(qbd_ref, qi_ref, wi_ref, k_ref, vt_ref, ki_ref, o_ref,
                 keys_ref, sb_ref, m_ref, l_ref, acc_ref, *, topk):
    i = pl.program_id(0)
    nch = (i + 2) >> 1

    lane2 = lax.broadcasted_iota(i32, (1, 2 * TQ), 1)

    @pl.when(i == 0)
    def _():
        row = lax.broadcasted_iota(i32, (TK, 2 * TQ), 0).astype(f32)
        lane = lax.broadcasted_iota(i32, (TK, 2 * TQ), 1)
        for p in range(N_PAIR):
            sb_ref[p] = row * jnp.where(lane < TQ, _slope2(2 * p), _slope2(2 * p + 1))

    qi = qi_ref[0]
    w = wi_ref[...]
    d0 = (lax.broadcasted_iota(i32, (TK, TQ), 1) - lax.broadcasted_iota(i32, (TK, TQ), 0))

    def score_chunk(c, carry):
        r0 = pl.multiple_of(c * TK, TK)
        il = _nt_dot(ki_ref[pl.ds(r0, TK), :], qi)
        sc = jnp.maximum(il[:, 0:TQ], 0.0) * w[0:1, :]
        for hh in range(1, IDX_HEADS):
            sc = sc + jnp.maximum(il[:, hh * TQ:(hh + 1) * TQ], 0.0) * w[hh:hh + 1, :]
        b = lax.bitcast_convert_type(sc, i32)
        key = jnp.where(b < 0, -(b & 0x7FFFFFFF), b)
        valid = d0 >= (r0 - i * TQ)
        keys_ref[pl.ds(r0, TK), :] = jnp.where(valid, key, INT_MIN)
        return carry

    lax.fori_loop(0, nch, score_chunk, 0)

    def count_ge(cand):
        def body(c, acc):
            r0 = pl.multiple_of(c * TK, TK)
            ind = jnp.where(keys_ref[pl.ds(r0, TK), :] >= cand, 1, 0)
            part = ind[0:SUBLANE]
            for j in range(1, TK // SUBLANE):
                part = part + ind[j * SUBLANE:(j + 1) * SUBLANE]
            return acc + part
        acc = lax.fori_loop(0, nch, body, jnp.zeros((SUBLANE, TQ), i32))
        return jnp.sum(acc, axis=0, keepdims=True)

    def bit_step(bi, t):
        cand = t + lax.shift_left(jnp.int32(1), 31 - bi)
        return jnp.where(count_ge(cand) >= topk, cand, t)

    t = lax.fori_loop(0, 32, bit_step, jnp.full((1, TQ), INT_MIN, i32))
    t = jnp.maximum(t, INT_MIN + 1)
    r_tie = (topk - count_ge(t + 1)).astype(f32)

    tri = jnp.where(lax.broadcasted_iota(i32, (TK, TK), 0) >= lax.broadcasted_iota(i32, (TK, TK), 1),
                    1.0, 0.0).astype(bf16)

    def mask_chunk(c, carry):
        r0 = pl.multiple_of(c * TK, TK)
        kk = keys_ref[pl.ds(r0, TK), :]
        eq = kk == t
        e = jnp.where(eq, 1.0, 0.0).astype(bf16)
        pre = jnp.dot(tri, e, preferred_element_type=f32) + carry
        nm = jnp.where(kk > t, 0.0, jnp.where(eq, jnp.where(pre <= r_tie, 0.0, NEG_INF), NEG_INF))
        keys_ref[pl.ds(r0, TK), :] = lax.bitcast_convert_type(nm, i32)
        return pre[TK - 1:TK, :]

    lax.fori_loop(0, nch, mask_chunk, jnp.zeros((1, TQ), f32))

    m_ref[...] = jnp.full(m_ref.shape, NEG_INF, f32)
    l_ref[...] = jnp.zeros(l_ref.shape, f32)
    acc_ref[...] = jnp.zeros(acc_ref.shape, f32)

    def attn_chunk(c, carry):
        r0 = pl.multiple_of(c * TK, TK)
        nm = lax.bitcast_convert_type(keys_ref[pl.ds(r0, TK), :], f32)
        nm2 = jnp.concatenate([nm, nm], axis=1)
        r0f = r0.astype(f32)
        for p in range(N_PAIR):
            kb = k_ref[pl.ds(r0, TK), p * LANE:(p + 1) * LANE]
            s = _nt_dot(kb, qbd_ref[0, p])
            sm = (s + sb_ref[p]) + nm2
            coff = jnp.where(lane2 < TQ, _slope2(2 * p), _slope2(2 * p + 1)) * r0f
            m_old = m_ref[p]
            m_new = jnp.maximum(m_old, jnp.max(sm, axis=0, keepdims=True) + coff)
            m_safe = jnp.where(m_new == NEG_INF, 0.0, m_new)
            pm = jnp.exp2(sm - (m_safe - coff))
            alpha = jnp.where(m_old == NEG_INF, 0.0, jnp.exp2(m_old - m_safe))
            l_ref[p] = alpha * l_ref[p] + jnp.sum(pm, axis=0, keepdims=True)
            pv = jnp.dot(vt_ref[c, p * LANE:(p + 1) * LANE, :], pm.astype(bf16),
                         preferred_element_type=f32)
            acc_ref[p] = acc_ref[p] * alpha + pv
            m_ref[p] = m_new
        return carry

    lax.fori_loop(0, nch, attn_chunk, 0)

    outs = []
    for p in range(N_PAIR):
        o = acc_ref[p] / l_ref[p]
        outs.append(o[0:HEAD_DIM, 0:TQ])
        outs.append(o[HEAD_DIM:2 * HEAD_DIM, TQ:2 * TQ])
    o_ref[...] = jnp.concatenate(outs, axis=0).T.astype(bf16)


def _attn_call(qbd, qi, wit, k, vt, ki, topk):
    s = k.shape[0]
    nq = s // TQ
    whole = pl.BlockSpec(memory_space=pltpu.VMEM)
    return pl.pallas_call(
        functools.partial(_attn_kernel, topk=topk),
        grid=(nq,),
        in_specs=[pl.BlockSpec((1, N_PAIR, 2 * TQ, LANE), lambda i: (i, 0, 0, 0)),
                  pl.BlockSpec((1, IDX_HEADS * TQ, LANE), lambda i: (i, 0, 0)),
                  pl.BlockSpec((SUBLANE, TQ), lambda i: (0, i)),
                  whole, whole, whole],
        out_specs=pl.BlockSpec((TQ, ATTN_WIDTH), lambda i: (i, 0)),
        out_shape=jax.ShapeDtypeStruct((s, ATTN_WIDTH), bf16),
        scratch_shapes=[pltpu.VMEM((s, TQ), i32),
                        pltpu.VMEM((N_PAIR, TK, 2 * TQ), f32),
                        pltpu.VMEM((N_PAIR, 1, 2 * TQ), f32),
                        pltpu.VMEM((N_PAIR, 1, 2 * TQ), f32),
                        pltpu.VMEM((N_PAIR, LANE, 2 * TQ), f32)],
        compiler_params=pltpu.CompilerParams(dimension_semantics=("arbitrary",),
                                             vmem_limit_bytes=VMEM_LIMIT),
        name="attn",
    )(qbd, qi, wit, k, vt, ki)


HALO_POOL = 16


def _mix_kernel(x_ref, attn_ref, u_ref, uh_ref, ga_ref, gp_ref, mod_ref,
                wab_ref, wg_ref, ps_ref, wpb_ref, wo_ref, o_ref, *, tm):
    i = pl.program_id(0)
    y_attn = jnp.dot(attn_ref[...], wab_ref[...], preferred_element_type=f32)

    u = u_ref[...]
    halo = jnp.where(i > 0, uh_ref[...], 0.0)
    a = jnp.concatenate([halo, u], axis=0)
    tpos = (i * tm + lax.broadcasted_iota(i32, (tm, POOL_GROUP_DIM), 0) + 1).astype(f32)
    mixed = []
    for g, wdw in enumerate(POOL_WINDOWS):
        ag = a[:, g * POOL_GROUP_DIM:(g + 1) * POOL_GROUP_DIM]
        ug = ag[HALO_POOL:HALO_POOL + tm]
        ssum = ug
        for j in range(1, wdw):
            ssum = ssum + ag[HALO_POOL - j:HALO_POOL - j + tm]
        pooled = ssum / jnp.minimum(tpos, float(wdw)) - ug
        mixed.append(jnp.dot(pooled.astype(bf16), wg_ref[g], preferred_element_type=f32))
    mixed = jnp.concatenate(mixed, axis=1) * ps_ref[...]
    y_pool = jnp.dot(mixed.astype(bf16), wpb_ref[...], preferred_element_type=f32)

    merged = ga_ref[...].astype(f32) * y_attn + gp_ref[...].astype(f32) * y_pool
    o = jnp.dot(merged.astype(bf16), wo_ref[...], preferred_element_type=f32)
    gate = mod_ref[0:1, 2 * D_MODEL:3 * D_MODEL]
    o_ref[...] = x_ref[...] + gate * o


def _mix_call(x2, attn, u, ga, gp, mod, wab, wg, ps, wpb, wo, tm):
    s = x2.shape[0]
    const2 = lambda i: (0, 0)
    hb = tm // HALO_POOL
    return pl.pallas_call(
        functools.partial(_mix_kernel, tm=tm),
        grid=(s // tm,),
        in_specs=[pl.BlockSpec((tm, D_MODEL), lambda i: (i, 0)),
                  pl.BlockSpec((tm, ATTN_WIDTH), lambda i: (i, 0)),
                  pl.BlockSpec((tm, POOL_WIDTH), lambda i: (i, 0)),
                  pl.BlockSpec((HALO_POOL, POOL_WIDTH), lambda i: (jnp.maximum(i * hb - 1, 0), 0)),
                  pl.BlockSpec((tm, D_MODEL), lambda i: (i, 0)),
                  pl.BlockSpec((tm, D_MODEL), lambda i: (i, 0)),
                  pl.BlockSpec((SUBLANE, N_MOD * D_MODEL), const2),
                  pl.BlockSpec((ATTN_WIDTH, D_MODEL), const2),
                  pl.BlockSpec((len(POOL_WINDOWS), POOL_GROUP_DIM, POOL_GROUP_DIM), lambda i: (0, 0, 0)),
                  pl.BlockSpec((1, POOL_WIDTH), const2),
                  pl.BlockSpec((POOL_WIDTH, D_MODEL), const2),
                  pl.BlockSpec((D_MODEL, D_MODEL), const2)],
        out_specs=pl.BlockSpec((tm, D_MODEL), lambda i: (i, 0)),
        out_shape=jax.ShapeDtypeStruct((s, D_MODEL), f32),
        compiler_params=pltpu.CompilerParams(dimension_semantics=("parallel",),
                                             vmem_limit_bytes=VMEM_LIMIT),
        name="mix",
    )(x2, attn, u, u, ga, gp, mod, wab, wg, ps, wpb, wo)


HALO_CONV = 8


def _ffn_kernel(x_ref, xh_ref, mod_ref, g2_ref, wup_ref, cw_ref, cb_ref, wdn_ref, o_ref, *, tm):
    i = pl.program_id(0)
    shift = mod_ref[0:1, 3 * D_MODEL:4 * D_MODEL]
    scale = mod_ref[0:1, 4 * D_MODEL:5 * D_MODEL]
    gate = mod_ref[0:1, 5 * D_MODEL:6 * D_MODEL]
    g2 = g2_ref[...]
    x = x_ref[...]
    h = _rms_modulate(x, g2, shift, scale)
    hh = jnp.where(i > 0, _rms_modulate(xh_ref[...], g2, shift, scale), 0.0)
    ha = jnp.concatenate([hh, h], axis=0).astype(bf16)
    up = jnp.dot(ha, wup_ref[...], preferred_element_type=f32)
    cw = cw_ref[...]
    y = cb_ref[...] + cw[0:1, :] * up[HALO_CONV - 2:HALO_CONV - 2 + tm]
    y = y + cw[1:2, :] * up[HALO_CONV - 1:HALO_CONV - 1 + tm]
    y = y + cw[2:3, :] * up[HALO_CONV:HALO_CONV + tm]
    a = y[:, 0:D_FF]
    b = y[:, D_FF:2 * D_FF]
    gated = (a * _sigmoid(a)) * b
    o = jnp.dot(gated.astype(bf16), wdn_ref[...], preferred_element_type=f32)
    o_ref[...] = x + gate * o


def _ffn_call(x1, mod, g2, wup, cw, cb, wdn, tm):
    s = x1.shape[0]
    const2 = lambda i: (0, 0)
    hb = tm // HALO_CONV
    return pl.pallas_call(
        functools.partial(_ffn_kernel, tm=tm),
        grid=(s // tm,),
        in_specs=[pl.BlockSpec((tm, D_MODEL), lambda i: (i, 0)),
                  pl.BlockSpec((HALO_CONV, D_MODEL), lambda i: (jnp.maximum(i * hb - 1, 0), 0)),
                  pl.BlockSpec((SUBLANE, N_MOD * D_MODEL), const2),
                  pl.BlockSpec((1, D_MODEL), const2),
                  pl.BlockSpec((D_MODEL, 2 * D_FF), const2),
                  pl.BlockSpec((3, 2 * D_FF), const2),
                  pl.BlockSpec((1, 2 * D_FF), const2),
                  pl.BlockSpec((D_FF, D_MODEL), const2)],
        out_specs=pl.BlockSpec((tm, D_MODEL), lambda i: (i, 0)),
        out_shape=jax.ShapeDtypeStruct((s, D_MODEL), f32),
        compiler_params=pltpu.CompilerParams(dimension_semantics=("parallel",),
                                             vmem_limit_bytes=VMEM_LIMIT),
        name="ffn",
    )(x1, x1, mod, g2, wup, cw, cb, wdn)


def _pack_w_in(w):
    q, k, v, qi, ki, wi, u, ga, gp = (w[:, 0:512], w[:, 512:1024], w[:, 1024:1536], w[:, 1536:1792],
                                      w[:, 1792:1856], w[:, 1856:1860], w[:, 1860:2372],
                                      w[:, 2372:3396], w[:, 3396:4420])
    wi_pad = jnp.pad(wi, ((0, 0), (0, LANE - IDX_HEADS)))
    return jnp.concatenate([q, k, v, qi, ki, ki, wi_pad, u, ga, gp], axis=1).astype(bf16)


def kernel(x, c, w_ada, b_ada, norm1_g, w_in, q_norm_g, k_norm_g, w_attn_br, w_pool_grp,
           pool_scale, w_pool_br, w_out, norm2_g, w_up, conv_w, conv_b, w_down):
    bsz, s, d = x.shape
    assert bsz == 1 and d == D_MODEL and s % 512 == 0
    depth = w_ada.shape[0]
    topk = min(TOPK_MAX, s // 4)
    x2 = x.reshape(s, d)
    c8 = jnp.pad(c, ((0, SUBLANE - bsz), (0, 0)))
    for l in range(depth):
        mod = _mod_call(c8, w_ada[l], b_ada[l].reshape(1, -1))
        qbd, k, vt, qi, ki, wit, u, ga, gp = _proj_call(
            x2, mod, norm1_g[l].reshape(1, -1), _pack_w_in(w_in[l]),
            jnp.tile(q_norm_g[l], N_HEADS).reshape(1, -1),
            jnp.tile(k_norm_g[l], N_HEADS).reshape(1, -1), tm=512)
        attn = _attn_call(qbd, qi, wit, k, vt, ki, topk)
        x2 = _mix_call(x2, attn, u, ga, gp, mod, w_attn_br[l].astype(bf16),
                       w_pool_grp[l].astype(bf16), pool_scale[l].reshape(1, -1),
                       w_pool_br[l].astype(bf16), w_out[l].astype(bf16), tm=512)
        x2 = _ffn_call(x2, mod, norm2_g[l].reshape(1, -1), w_up[l].astype(bf16), conv_w[l],
                       conv_b[l].reshape(1, -1), w_down[l].astype(bf16), tm=256)
    return x2.reshape(bsz, s, d)
```

```python
import functools
import math

import jax
import jax.numpy as jnp
from jax import lax
from jax.experimental import pallas as pl
from jax.experimental.pallas import tpu as pltpu

f32 = jnp.float32
bf16 = jnp.bfloat16
i32 = jnp.int32

D_MODEL = 1024
N_HEADS = 8
HEAD_DIM = 64
ATTN_WIDTH = N_HEADS * HEAD_DIM
IDX_HEADS = 4
IDX_DIM = 64
TOPK_MAX = 256
POOL_WINDOWS = (2, 4, 8, 16)
POOL_GROUP_DIM = 128
POOL_WIDTH = 512
D_FF = 2816
EPS = 1e-6
N_MOD = 6

LANE = 128
SUBLANE = 8
VMEM_LIMIT = 56 * 1024 * 1024

LOG2E = 1.4426950408889634
INT_MIN = -2147483648
NEG_INF = float("-inf")

C_Q, C_K, C_V, C_QI, C_KI, C_WI, C_U, C_GA, C_GP, C_END = (
    0, 512, 1024, 1536, 1792, 1920, 2048, 2560, 3584, 4608)

TQ = 128
TK = 256
N_PAIR = N_HEADS // 2
VT_ROWS = 2 * HEAD_DIM + 16


def _nt_dot(a, b):
    return lax.dot_general(a, b, (((1,), (1,)), ((), ())), preferred_element_type=f32)


def _sigmoid(x):
    return 1.0 / (1.0 + jnp.exp(-x))


def _rms_modulate(x, g, shift, scale):
    y = x * lax.rsqrt(jnp.mean(x * x, axis=-1, keepdims=True) + EPS)
    return (y * g) * (1.0 + scale) + shift


def _mod_kernel(c_ref, w_ref, b_ref, o_ref):
    c = c_ref[...]
    sc = c * _sigmoid(c)
    o_ref[...] = jnp.dot(sc, w_ref[...], precision=lax.Precision.HIGHEST,
                         preferred_element_type=f32) + b_ref[...]


def _mod_call(c8, w_ada, b_ada):
    n = w_ada.shape[1]
    tn = 1024
    return pl.pallas_call(
        _mod_kernel,
        grid=(n // tn,),
        in_specs=[pl.BlockSpec((SUBLANE, D_MODEL), lambda j: (0, 0)),
                  pl.BlockSpec((D_MODEL, tn), lambda j: (0, j)),
                  pl.BlockSpec((1, tn), lambda j: (0, j))],
        out_specs=pl.BlockSpec((SUBLANE, tn), lambda j: (0, j)),
        out_shape=jax.ShapeDtypeStruct((SUBLANE, n), f32),
        name="mod",
    )(c8, w_ada, b_ada)


def _head_norm(z, g, bd):
    z2 = z * z
    hi = z2.astype(bf16)
    lo = (z2 - hi.astype(f32)).astype(bf16)
    ms = jnp.dot(hi, bd, preferred_element_type=f32) + jnp.dot(lo, bd, preferred_element_type=f32)
    return (z * lax.rsqrt(ms + EPS)) * g


def _proj_kernel(x_ref, mod_ref, g1_ref, w_ref, qg_ref, kg_ref,
                 qbd_ref, k_ref, vt_ref, qi_ref, ki_ref, wit_ref, u_ref, ga_ref, gp_ref, *, tm):
    x = x_ref[...]
    shift = mod_ref[0:1, 0:D_MODEL]
    scale = mod_ref[0:1, D_MODEL:2 * D_MODEL]
    h = _rms_modulate(x, g1_ref[...], shift, scale)
    proj = jnp.dot(h.astype(bf16), w_ref[...], preferred_element_type=f32)

    r = lax.broadcasted_iota(i32, (ATTN_WIDTH, ATTN_WIDTH), 0)
    c = lax.broadcasted_iota(i32, (ATTN_WIDTH, ATTN_WIDTH), 1)
    bd = jnp.where((r >> 6) == (c >> 6), 1.0 / HEAD_DIM, 0.0).astype(bf16)

    q = _head_norm(proj[:, C_Q:C_K], qg_ref[...], bd) * (HEAD_DIM ** -0.5 * LOG2E)
    k = _head_norm(proj[:, C_K:C_V], kg_ref[...], bd)
    k_ref[...] = k.astype(bf16)

    lane = lax.broadcasted_iota(i32, (TQ, LANE), 1)
    low = lane < HEAD_DIM
    qi = proj[:, C_QI:C_KI] * (IDX_DIM ** -0.5)
    for g in range(tm // TQ):
        rows = slice(g * TQ, (g + 1) * TQ)
        for p in range(N_PAIR):
            qp = q[rows, p * LANE:(p + 1) * LANE]
            qbd_ref[g, p, 0:TQ, :] = jnp.where(low, qp, 0.0).astype(bf16)
            qbd_ref[g, p, TQ:2 * TQ, :] = jnp.where(low, 0.0, qp).astype(bf16)
        for hh in range(IDX_HEADS):
            seg = qi[rows, (hh // 2) * LANE:(hh // 2 + 1) * LANE]
            keep = low if hh % 2 == 0 else jnp.logical_not(low)
            qi_ref[g, hh * TQ:(hh + 1) * TQ, :] = jnp.where(keep, seg, 0.0).astype(bf16)

    v = proj[:, C_V:C_QI]
    vt = v.T.astype(bf16)
    ones = jnp.ones((VT_ROWS - LANE, TK), bf16)
    for cc in range(tm // TK):
        for p in range(N_PAIR):
            vt_ref[cc, p, 0:LANE, :] = vt[p * LANE:(p + 1) * LANE, cc * TK:(cc + 1) * TK]
            vt_ref[cc, p, LANE:VT_ROWS, :] = ones

    ki_ref[...] = proj[:, C_KI:C_WI].astype(bf16)
    wt = (proj[:, C_WI:C_U] * (IDX_HEADS ** -0.5)).T
    wit_ref[...] = wt[0:SUBLANE, :]
    u_ref[...] = proj[:, C_U:C_GA]
    ga_ref[...] = _sigmoid(proj[:, C_GA:C_GP]).astype(bf16)
    gp_ref[...] = _sigmoid(proj[:, C_GP:C_END]).astype(bf16)


def _proj_call(x2, mod, g1, w_in_p, qg, kg, tm):
    s = x2.shape[0]
    nq = s // TQ
    const = lambda i: (0, 0)
    return pl.pallas_call(
        functools.partial(_proj_kernel, tm=tm),
        grid=(s // tm,),
        in_specs=[pl.BlockSpec((tm, D_MODEL), lambda i: (i, 0)),
                  pl.BlockSpec((SUBLANE, N_MOD * D_MODEL), const),
                  pl.BlockSpec((1, D_MODEL), const),
                  pl.BlockSpec((D_MODEL, C_END), const),
                  pl.BlockSpec((1, ATTN_WIDTH), const),
                  pl.BlockSpec((1, ATTN_WIDTH), const)],
        out_specs=[pl.BlockSpec((tm // TQ, N_PAIR, 2 * TQ, LANE), lambda i: (i, 0, 0, 0)),
                   pl.BlockSpec((tm, ATTN_WIDTH), lambda i: (i, 0)),
                   pl.BlockSpec((tm // TK, N_PAIR, VT_ROWS, TK), lambda i: (i, 0, 0, 0)),
                   pl.BlockSpec((tm // TQ, IDX_HEADS * TQ, LANE), lambda i: (i, 0, 0)),
                   pl.BlockSpec((tm, LANE), lambda i: (i, 0)),
                   pl.BlockSpec((SUBLANE, tm), lambda i: (0, i)),
                   pl.BlockSpec((tm, POOL_WIDTH), lambda i: (i, 0)),
                   pl.BlockSpec((tm, D_MODEL), lambda i: (i, 0)),
                   pl.BlockSpec((tm, D_MODEL), lambda i: (i, 0))],
        out_shape=[jax.ShapeDtypeStruct((nq, N_PAIR, 2 * TQ, LANE), bf16),
                   jax.ShapeDtypeStruct((s, ATTN_WIDTH), bf16),
                   jax.ShapeDtypeStruct((s // TK, N_PAIR, VT_ROWS, TK), bf16),
                   jax.ShapeDtypeStruct((nq, IDX_HEADS * TQ, LANE), bf16),
                   jax.ShapeDtypeStruct((s, LANE), bf16),
                   jax.ShapeDtypeStruct((SUBLANE, s), f32),
                   jax.ShapeDtypeStruct((s, POOL_WIDTH), f32),
                   jax.ShapeDtypeStruct((s, D_MODEL), bf16),
                   jax.ShapeDtypeStruct((s, D_MODEL), bf16)],
        compiler_params=pltpu.CompilerParams(dimension_semantics=("parallel",),
                                             vmem_limit_bytes=VMEM_LIMIT),
        name="proj",
    )(x2, mod, g1, w_in_p, qg, kg)


def _slope2(h):
    return (2.0 ** (-8.0 * (h + 1) / N_HEADS)) * LOG2E


def _tree(op, xs):
    xs = list(xs)
    while len(xs) > 1:
        nxt = [op(xs[a], xs[a + 1]) for a in range(0, len(xs) - 1, 2)]
        if len(xs) % 2:
            nxt.append(xs[-1])
        xs = nxt
    return xs[0]


def _row_groups(x):
    return [x[j * SUBLANE:(j + 1) * SUBLANE] for j in range(x.shape[0] // SUBLANE)]


N_CNT_ACC = 4


def _attn_kernel(qbd_ref, qi_ref, wi_ref, k_ref, vt_ref, ki_ref, o_ref,
                 keys_ref, sb_ref, il0_ref, il1_ref, pre0_ref, pre1_ref, sm0_ref, sm1_ref,
                 cmax0_ref, cmax1_ref, p_ref, m_ref, acc_ref, *, topk):
    il_ref = (il0_ref, il1_ref)
    pre_ref = (pre0_ref, pre1_ref)
    sm_ref = (sm0_ref, sm1_ref)
    cmax_ref = (cmax0_ref, cmax1_ref)
    i = pl.program_id(0)
    nsc = (i + 4) >> 2
    nch = 2 * nsc

    def chunk_start(c):
        return pl.multiple_of(c * TK, TK)

    def pipeline(produce, consume, carry):
        produce(0, 0)
        produce(1, 1)

        def body(sc, carry):
            c0 = 2 * sc
            carry = consume(c0, 0, carry)
            produce(c0 + 2, 0)
            carry = consume(c0 + 1, 1, carry)
            produce(c0 + 3, 1)
            return carry

        carry = lax.fori_loop(0, nsc - 1, body, carry)
        carry = consume(nch - 2, 0, carry)
        return consume(nch - 1, 1, carry)

    lane2 = lax.broadcasted_iota(i32, (1, 2 * TQ), 1)

    @pl.when(i == 0)
    def _():
        row = lax.broadcasted_iota(i32, (TK, 2 * TQ), 0).astype(f32)
        lane = lax.broadcasted_iota(i32, (TK, 2 * TQ), 1)
        for p in range(N_PAIR):
            sb_ref[p] = row * jnp.where(lane < TQ, _slope2(2 * p), _slope2(2 * p + 1))

    qi = qi_ref[0]
    w = wi_ref[...]
    d0 = (lax.broadcasted_iota(i32, (TK, TQ), 1) - lax.broadcasted_iota(i32, (TK, TQ), 0))

    def idx_matmul(c, slot):
        il_ref[slot][...] = _nt_dot(ki_ref[pl.ds(chunk_start(c), TK), :], qi)

    def idx_keys(c, slot, carry):
        r0 = chunk_start(c)
        sc = jnp.maximum(il_ref[slot][:, 0:TQ], 0.0) * w[0:1, :]
        for hh in range(1, IDX_HEADS):
            sc = sc + jnp.maximum(il_ref[slot][:, hh * TQ:(hh + 1) * TQ], 0.0) * w[hh:hh + 1, :]
        b = lax.bitcast_convert_type(sc, i32)
        key = jnp.where(b < 0, -(b & 0x7FFFFFFF), b)
        valid = d0 >= (r0 - i * TQ)
        keys_ref[pl.ds(r0, TK), :] = jnp.where(valid, key, INT_MIN)
        return carry

    pipeline(idx_matmul, idx_keys, 0)

    def count_ge(cand):
        def body(sc, accs):
            r0 = pl.multiple_of(sc * 2 * TK, 2 * TK)
            ind = jnp.where(keys_ref[pl.ds(r0, 2 * TK), :] >= cand, 1, 0)
            g = _row_groups(ind)
            n = len(g) // N_CNT_ACC
            return tuple(accs[a] + _tree(jnp.add, g[a * n:(a + 1) * n]) for a in range(N_CNT_ACC))
        accs = lax.fori_loop(0, nsc, body,
                             tuple(jnp.zeros((SUBLANE, TQ), i32) for _ in range(N_CNT_ACC)))
        return jnp.sum(_tree(jnp.add, accs), axis=0, keepdims=True)

    def bit_step(bi, t):
        cand = t + lax.shift_left(jnp.int32(1), 31 - bi)
        return jnp.where(count_ge(cand) >= topk, cand, t)

    t = lax.fori_loop(0, 32, bit_step, jnp.full((1, TQ), INT_MIN, i32))
    t = jnp.maximum(t, INT_MIN + 1)
    r_tie = (topk - count_ge(t + 1)).astype(f32)

    tri = jnp.where(lax.broadcasted_iota(i32, (TK, TK), 0) >= lax.broadcasted_iota(i32, (TK, TK), 1),
                    1.0, 0.0).astype(bf16)

    def tie_rank(c, slot):
        e = jnp.where(keys_ref[pl.ds(chunk_start(c), TK), :] == t, 1.0, 0.0).astype(bf16)
        pre_ref[slot][...] = jnp.dot(tri, e, preferred_element_type=f32)

    def mask_out(c, slot, rank):
        r0 = chunk_start(c)
        kk = keys_ref[pl.ds(r0, TK), :]
        pre = pre_ref[slot][...] + rank
        nm = jnp.where(kk > t, 0.0, jnp.where(kk == t, jnp.where(pre <= r_tie, 0.0, NEG_INF), NEG_INF))
        keys_ref[pl.ds(r0, TK), :] = lax.bitcast_convert_type(nm, i32)
        return pre[TK - 1:TK, :]

    pipeline(tie_rank, mask_out, jnp.zeros((1, TQ), f32))

    m_ref[...] = jnp.full(m_ref.shape, NEG_INF, f32)
    acc_ref[...] = jnp.zeros(acc_ref.shape, f32)

    def qk_stage(c, slot):
        r0 = chunk_start(c)
        for p in range(N_PAIR):
            nm = lax.bitcast_convert_type(keys_ref[pl.ds(r0, TK), :], f32)
            s = _nt_dot(k_ref[pl.ds(r0, TK), p * LANE:(p + 1) * LANE], qbd_ref[0, p])
            sm = (s + sb_ref[p]) + jnp.concatenate([nm, nm], axis=1)
            sm_ref[slot][p] = sm
            cmax_ref[slot][p] = jnp.max(_tree(jnp.maximum, _row_groups(sm)), axis=0, keepdims=True)

    def pv_stage(c, slot, carry):
        r0f = (c * TK).astype(f32)
        for p in range(N_PAIR):
            coff = jnp.where(lane2 < TQ, _slope2(2 * p), _slope2(2 * p + 1)) * r0f
            m_old = m_ref[p]
            m_new = jnp.maximum(m_old, cmax_ref[slot][p] + coff)
            m_safe = jnp.where(m_new == NEG_INF, 0.0, m_new)
            alpha = jnp.where(m_old == NEG_INF, 0.0, jnp.exp2(m_old - m_safe))
            p_ref[p] = jnp.exp2(sm_ref[slot][p] - (m_safe - coff)).astype(bf16)
            pv = jnp.dot(vt_ref[c, p], p_ref[p], preferred_element_type=f32)
            acc_ref[p] = acc_ref[p] * alpha + pv
            m_ref[p] = m_new
        return carry

    pipeline(qk_stage, pv_stage, 0)

    outs = []
    for p in range(N_PAIR):
        a = acc_ref[p]
        o = a[0:LANE] / a[LANE:LANE + 1]
        outs.append(o[0:HEAD_DIM, 0:TQ])
        outs.append(o[HEAD_DIM:2 * HEAD_DIM, TQ:2 * TQ])
    o_ref[...] = jnp.concatenate(outs, axis=0).T.astype(bf16)


def _attn_call(qbd, qi, wit, k, vt, ki, topk):
    s = k.shape[0]
    nq = s // TQ
    whole = pl.BlockSpec(memory_space=pltpu.VMEM)
    return pl.pallas_call(
        functools.partial(_attn_kernel, topk=topk),
        grid=(nq,),
        in_specs=[pl.BlockSpec((1, N_PAIR, 2 * TQ, LANE), lambda i: (i, 0, 0, 0)),
                  pl.BlockSpec((1, IDX_HEADS * TQ, LANE), lambda i: (i, 0, 0)),
                  pl.BlockSpec((SUBLANE, TQ), lambda i: (0, i)),
                  whole, whole, whole],
        out_specs=pl.BlockSpec((TQ, ATTN_WIDTH), lambda i: (i, 0)),
        out_shape=jax.ShapeDtypeStruct((s, ATTN_WIDTH), bf16),
        scratch_shapes=[pltpu.VMEM((s, TQ), i32),
                        pltpu.VMEM((N_PAIR, TK, 2 * TQ), f32),
                        pltpu.VMEM((TK, IDX_HEADS * TQ), f32),
                        pltpu.VMEM((TK, IDX_HEADS * TQ), f32),
                        pltpu.VMEM((TK, TQ), f32),
                        pltpu.VMEM((TK, TQ), f32),
                        pltpu.VMEM((N_PAIR, TK, 2 * TQ), f32),
                        pltpu.VMEM((N_PAIR, TK, 2 * TQ), f32),
                        pltpu.VMEM((N_PAIR, 1, 2 * TQ), f32),
                        pltpu.VMEM((N_PAIR, 1, 2 * TQ), f32),
                        pltpu.VMEM((N_PAIR, TK, 2 * TQ), bf16),
                        pltpu.VMEM((N_PAIR, 1, 2 * TQ), f32),
                        pltpu.VMEM((N_PAIR, VT_ROWS, 2 * TQ), f32)],
        compiler_params=pltpu.CompilerParams(dimension_semantics=("arbitrary",),
                                             vmem_limit_bytes=VMEM_LIMIT),
        name="attn",
    )(qbd, qi, wit, k, vt, ki)


HALO_POOL = 16


def _mix_kernel(x_ref, attn_ref, u_ref, uh_ref, ga_ref, gp_ref, mod_ref,
                wab_ref, wg_ref, ps_ref, wpb_ref, wo_ref, o_ref, *, tm):
    i = pl.program_id(0)
    y_attn = jnp.dot(attn_ref[...], wab_ref[...], preferred_element_type=f32)

    u = u_ref[...]
    halo = jnp.where(i > 0, uh_ref[...], 0.0)
    a = jnp.concatenate([halo, u], axis=0)
    tpos = (i * tm + lax.broadcasted_iota(i32, (tm, POOL_GROUP_DIM), 0) + 1).astype(f32)
    mixed = []
    for g, wdw in enumerate(POOL_WINDOWS):
        ag = a[:, g * POOL_GROUP_DIM:(g + 1) * POOL_GROUP_DIM]
        ug = ag[HALO_POOL:HALO_POOL + tm]
        ssum = ug
        for j in range(1, wdw):
            ssum = ssum + ag[HALO_POOL - j:HALO_POOL - j + tm]
        pooled = ssum / jnp.minimum(tpos, float(wdw)) - ug
        mixed.append(jnp.dot(pooled.astype(bf16), wg_ref[g], preferred_element_type=f32))
    mixed = jnp.concatenate(mixed, axis=1) * ps_ref[...]
    y_pool = jnp.dot(mixed.astype(bf16), wpb_ref[...], preferred_element_type=f32)

    merged = ga_ref[...].astype(f32) * y_attn + gp_ref[...].astype(f32) * y_pool
    o = jnp.dot(merged.astype(bf16), wo_ref[...], preferred_element_type=f32)
    gate = mod_ref[0:1, 2 * D_MODEL:3 * D_MODEL]
    o_ref[...] = x_ref[...] + gate * o


def _mix_call(x2, attn, u, ga, gp, mod, wab, wg, ps, wpb, wo, tm):
    s = x2.shape[0]
    const2 = lambda i: (0, 0)
    hb = tm // HALO_POOL
    return pl.pallas_call(
        functools.partial(_mix_kernel, tm=tm),
        grid=(s // tm,),
        in_specs=[pl.BlockSpec((tm, D_MODEL), lambda i: (i, 0)),
                  pl.BlockSpec((tm, ATTN_WIDTH), lambda i: (i, 0)),
                  pl.BlockSpec((tm, POOL_WIDTH), lambda i: (i, 0)),
                  pl.BlockSpec((HALO_POOL, POOL_WIDTH), lambda i: (jnp.maximum(i * hb - 1, 0), 0)),
                  pl.BlockSpec((tm, D_MODEL), lambda i: (i, 0)),
                  pl.BlockSpec((tm, D_MODEL), lambda i: (i, 0)),
                  pl.BlockSpec((SUBLANE, N_MOD * D_MODEL), const2),
                  pl.BlockSpec((ATTN_WIDTH, D_MODEL), const2),
                  pl.BlockSpec((len(POOL_WINDOWS), POOL_GROUP_DIM, POOL_GROUP_DIM), lambda i: (0, 0, 0)),
                  pl.BlockSpec((1, POOL_WIDTH), const2),
                  pl.BlockSpec((POOL_WIDTH, D_MODEL), const2),
                  pl.BlockSpec((D_MODEL, D_MODEL), const2)],
        out_specs=pl.BlockSpec((tm, D_MODEL), lambda i: (i, 0)),
        out_shape=jax.ShapeDtypeStruct((s, D_MODEL), f32),
        compiler_params=pltpu.CompilerParams(dimension_semantics=("parallel",),
                                             vmem_limit_bytes=VMEM_LIMIT),
        name="mix",
    )(x2, attn, u, u, ga, gp, mod, wab, wg, ps, wpb, wo)


HALO_CONV = 8


def _ffn_kernel(x_ref, xh_ref, mod_ref, g2_ref, wup_ref, cw_ref, cb_ref, wdn_ref, o_ref, *, tm):
    i = pl.program_id(0)
    shift = mod_ref[0:1, 3 * D_MODEL:4 * D_MODEL]
    scale = mod_ref[0:1, 4 * D_MODEL:5 * D_MODEL]
    gate = mod_ref[0:1, 5 * D_MODEL:6 * D_MODEL]
    g2 = g2_ref[...]
    x = x_ref[...]
    h = _rms_modulate(x, g2, shift, scale)
    hh = jnp.where(i > 0, _rms_modulate(xh_ref[...], g2, shift, scale), 0.0)
    ha = jnp.concatenate([hh, h], axis=0).astype(bf16)
    up = jnp.dot(ha, wup_ref[...], preferred_element_type=f32)
    cw = cw_ref[...]
    y = cb_ref[...] + cw[0:1, :] * up[HALO_CONV - 2:HALO_CONV - 2 + tm]
    y = y + cw[1:2, :] * up[HALO_CONV - 1:HALO_CONV - 1 + tm]
    y = y + cw[2:3, :] * up[HALO_CONV:HALO_CONV + tm]
    a = y[:, 0:D_FF]
    b = y[:, D_FF:2 * D_FF]
    gated = (a * _sigmoid(a)) * b
    o = jnp.dot(gated.astype(bf16), wdn_ref[...], preferred_element_type=f32)
    o_ref[...] = x + gate * o


def _ffn_call(x1, mod, g2, wup, cw, cb, wdn, tm):
    s = x1.shape[0]
    const2 = lambda i: (0, 0)
    hb = tm // HALO_CONV
    return pl.pallas_call(
        functools.partial(_ffn_kernel, tm=tm),
        grid=(s // tm,),
        in_specs=[pl.BlockSpec((tm, D_MODEL), lambda i: (i, 0)),
                  pl.BlockSpec((HALO_CONV, D_MODEL), lambda i: (jnp.maximum(i * hb - 1, 0), 0)),
                  pl.BlockSpec((SUBLANE, N_MOD * D_MODEL), const2),
                  pl.BlockSpec((1, D_MODEL), const2),
                  pl.BlockSpec((D_MODEL, 2 * D_FF), const2),
                  pl.BlockSpec((3, 2 * D_FF), const2),
                  pl.BlockSpec((1, 2 * D_FF), const2),
                  pl.BlockSpec((D_FF, D_MODEL), const2)],
        out_specs=pl.BlockSpec((tm, D_MODEL), lambda i: (i, 0)),
        out_shape=jax.ShapeDtypeStruct((s, D_MODEL), f32),
        compiler_params=pltpu.CompilerParams(dimension_semantics=("parallel",),
                                             vmem_limit_bytes=VMEM_LIMIT),
        name="ffn",
    )(x1, x1, mod, g2, wup, cw, cb, wdn)


def _pack_w_in(w):
    q, k, v, qi, ki, wi, u, ga, gp = (w[:, 0:512], w[:, 512:1024], w[:, 1024:1536], w[:, 1536:1792],
                                      w[:, 1792:1856], w[:, 1856:1860], w[:, 1860:2372],
                                      w[:, 2372:3396], w[:, 3396:4420])
    wi_pad = jnp.pad(wi, ((0, 0), (0, LANE - IDX_HEADS)))
    return jnp.concatenate([q, k, v, qi, ki, ki, wi_pad, u, ga, gp], axis=1).astype(bf16)


def kernel(x, c, w_ada, b_ada, norm1_g, w_in, q_norm_g, k_norm_g, w_attn_br, w_pool_grp,
           pool_scale, w_pool_br, w_out, norm2_g, w_up, conv_w, conv_b, w_down):
    bsz, s, d = x.shape
    assert bsz == 1 and d == D_MODEL and s % 512 == 0
    depth = w_ada.shape[0]
    topk = min(TOPK_MAX, s // 4)
    x2 = x.reshape(s, d)
    c8 = jnp.pad(c, ((0, SUBLANE - bsz), (0, 0)))
    for l in range(depth):
        mod = _mod_call(c8, w_ada[l], b_ada[l].reshape(1, -1))
        qbd, k, vt, qi, ki, wit, u, ga, gp = _proj_call(
            x2, mod, norm1_g[l].reshape(1, -1), _pack_w_in(w_in[l]),
            jnp.tile(q_norm_g[l], N_HEADS).reshape(1, -1),
            jnp.tile(k_norm_g[l], N_HEADS).reshape(1, -1), tm=512)
        attn = _attn_call(qbd, qi, wit, k, vt, ki, topk)
        x2 = _mix_call(x2, attn, u, ga, gp, mod, w_attn_br[l].astype(bf16),
                       w_pool_grp[l].astype(bf16), pool_scale[l].reshape(1, -1),
                       w_pool_br[l].astype(bf16), w_out[l].astype(bf16), tm=512)
        x2 = _ffn_call(x2, mod, norm2_g[l].reshape(1, -1), w_up[l].astype(bf16), conv_w[l],
                       conv_b[l].reshape(1, -1), w_down[l].astype(bf16), tm=256)
    return x2.reshape(bsz, s, d)
```

```python
import functools
import math

import jax
import jax.numpy as jnp
from jax import lax
from jax.experimental import pallas as pl
from jax.experimental.pallas import tpu as pltpu

f32 = jnp.float32
bf16 = jnp.bfloat16
i32 = jnp.int32

D_MODEL = 1024
N_HEADS = 8
HEAD_DIM = 64
ATTN_WIDTH = N_HEADS * HEAD_DIM
IDX_HEADS = 4
IDX_DIM = 64
TOPK_MAX = 256
POOL_WINDOWS = (2, 4, 8, 16)
POOL_GROUP_DIM = 128
POOL_WIDTH = 512
D_FF = 2816
EPS = 1e-6
N_MOD = 6

LANE = 128
SUBLANE = 8
VMEM_LIMIT = 56 * 1024 * 1024

LOG2E = 1.4426950408889634
INT_MIN = -2147483648
NEG_INF = float("-inf")

C_Q, C_K, C_V, C_QI, C_KI, C_WI, C_U, C_GA, C_GP, C_END = (
    0, 512, 1024, 1536, 1792, 1920, 2048, 2560, 3584, 4608)

TQ = 128
TK = 256
N_PAIR = N_HEADS // 2
VT_ROWS = 2 * HEAD_DIM + 16
IDX_K = 4 * LANE


def _nt_dot(a, b):
    return lax.dot_general(a, b, (((1,), (1,)), ((), ())), preferred_element_type=f32)


def _sigmoid(x):
    return 1.0 / (1.0 + jnp.exp(-x))


def _rms_modulate(x, g, shift, scale):
    y = x * lax.rsqrt(jnp.mean(x * x, axis=-1, keepdims=True) + EPS)
    return (y * g) * (1.0 + scale) + shift


def _mod_kernel(c_ref, w_ref, b_ref, o_ref):
    c = c_ref[...]
    sc = c * _sigmoid(c)
    o_ref[...] = jnp.dot(sc, w_ref[...], precision=lax.Precision.HIGHEST,
                         preferred_element_type=f32) + b_ref[...]


def _mod_call(c8, w_ada, b_ada):
    n = w_ada.shape[1]
    tn = 1024
    return pl.pallas_call(
        _mod_kernel,
        grid=(n // tn,),
        in_specs=[pl.BlockSpec((SUBLANE, D_MODEL), lambda j: (0, 0)),
                  pl.BlockSpec((D_MODEL, tn), lambda j: (0, j)),
                  pl.BlockSpec((1, tn), lambda j: (0, j))],
        out_specs=pl.BlockSpec((SUBLANE, tn), lambda j: (0, j)),
        out_shape=jax.ShapeDtypeStruct((SUBLANE, n), f32),
        name="mod",
    )(c8, w_ada, b_ada)


def _split3(x):
    hi = x.astype(bf16).astype(f32)
    r = x - hi
    mid = r.astype(bf16).astype(f32)
    lo = (r - mid).astype(bf16).astype(f32)
    return hi, mid, lo


def _head_norm(z, g, bd):
    z2 = z * z
    hi = z2.astype(bf16)
    lo = (z2 - hi.astype(f32)).astype(bf16)
    ms = jnp.dot(hi, bd, preferred_element_type=f32) + jnp.dot(lo, bd, preferred_element_type=f32)
    return (z * lax.rsqrt(ms + EPS)) * g


def _proj_kernel(x_ref, mod_ref, g1_ref, w_ref, qg_ref, kg_ref, widx_ref,
                 qbd_ref, k_ref, vt_ref, qi_ref, ki_ref, wit_ref, u_ref, ga_ref, gp_ref, *, tm):
    x = x_ref[...]
    shift = mod_ref[0:1, 0:D_MODEL]
    scale = mod_ref[0:1, D_MODEL:2 * D_MODEL]
    h = _rms_modulate(x, g1_ref[...], shift, scale)
    proj = jnp.dot(h.astype(bf16), w_ref[...], preferred_element_type=f32)

    r = lax.broadcasted_iota(i32, (ATTN_WIDTH, ATTN_WIDTH), 0)
    c = lax.broadcasted_iota(i32, (ATTN_WIDTH, ATTN_WIDTH), 1)
    bd = jnp.where((r >> 6) == (c >> 6), 1.0 / HEAD_DIM, 0.0).astype(bf16)

    q = _head_norm(proj[:, C_Q:C_K], qg_ref[...], bd) * (HEAD_DIM ** -0.5 * LOG2E)
    k = _head_norm(proj[:, C_K:C_V], kg_ref[...], bd)
    k_ref[...] = k.astype(bf16)

    low = lax.broadcasted_iota(i32, (TQ, LANE), 1) < HEAD_DIM
    low_tm = lax.broadcasted_iota(i32, (tm, LANE), 1) < HEAD_DIM
    pidx = jnp.dot(h, widx_ref[...], precision=lax.Precision.HIGHEST, preferred_element_type=f32)
    qi = pidx[:, 0:256] * (IDX_DIM ** -0.5)
    up64 = lambda z: pltpu.roll(z, HEAD_DIM, axis=1)
    for g in range(tm // TQ):
        rows = slice(g * TQ, (g + 1) * TQ)
        for p in range(N_PAIR):
            qp = q[rows, p * LANE:(p + 1) * LANE]
            qbd_ref[g, p, 0:TQ, :] = jnp.where(low, qp, 0.0).astype(bf16)
            qbd_ref[g, p, TQ:2 * TQ, :] = jnp.where(low, 0.0, qp).astype(bf16)
        for hh in range(IDX_HEADS):
            seg = qi[rows, (hh // 2) * LANE:(hh // 2 + 1) * LANE]
            own = low if hh % 2 == 0 else jnp.logical_not(low)
            qh, qm, ql = _split3(jnp.where(own, seg, 0.0))
            both = lambda z: z + up64(z)
            in_low = lambda z: jnp.where(low, both(z), 0.0)
            hrows = slice(hh * TQ, (hh + 1) * TQ)
            qi_ref[g, hrows, 0:LANE] = both(qh).astype(bf16)
            qi_ref[g, hrows, LANE:2 * LANE] = both(qm).astype(bf16)
            qi_ref[g, hrows, 2 * LANE:3 * LANE] = in_low(ql).astype(bf16)
            qi_ref[g, hrows, 3 * LANE:4 * LANE] = in_low(qh).astype(bf16)

    v = proj[:, C_V:C_QI]
    vt = v.T.astype(bf16)
    for cc in range(tm // TK):
        for p in range(N_PAIR):
            vt_ref[cc, p] = vt[p * LANE:(p + 1) * LANE, cc * TK:(cc + 1) * TK]

    kw = pidx[:, 256:384]
    kh, km, kl = _split3(jnp.where(low_tm, kw, 0.0))
    ki_ref[:, 0:LANE] = (kh + up64(km)).astype(bf16)
    ki_ref[:, LANE:2 * LANE] = kl.astype(bf16)
    wt = (kw * (IDX_HEADS ** -0.5)).T
    wit_ref[...] = wt[HEAD_DIM:HEAD_DIM + SUBLANE, :]
    u_ref[...] = proj[:, C_U:C_GA]
    ga_ref[...] = _sigmoid(proj[:, C_GA:C_GP]).astype(bf16)
    gp_ref[...] = _sigmoid(proj[:, C_GP:C_END]).astype(bf16)


def _proj_call(x2, mod, g1, w_in_p, qg, kg, widx, tm):
    s = x2.shape[0]
    nq = s // TQ
    const = lambda i: (0, 0)
    return pl.pallas_call(
        functools.partial(_proj_kernel, tm=tm),
        grid=(s // tm,),
        in_specs=[pl.BlockSpec((tm, D_MODEL), lambda i: (i, 0)),
                  pl.BlockSpec((SUBLANE, N_MOD * D_MODEL), const),
                  pl.BlockSpec((1, D_MODEL), const),
                  pl.BlockSpec((D_MODEL, C_END), const),
                  pl.BlockSpec((1, ATTN_WIDTH), const),
                  pl.BlockSpec((1, ATTN_WIDTH), const),
                  pl.BlockSpec((D_MODEL, 3 * LANE), const)],
        out_specs=[pl.BlockSpec((tm // TQ, N_PAIR, 2 * TQ, LANE), lambda i: (i, 0, 0, 0)),
                   pl.BlockSpec((tm, ATTN_WIDTH), lambda i: (i, 0)),
                   pl.BlockSpec((tm // TK, N_PAIR, LANE, TK), lambda i: (i, 0, 0, 0)),
                   pl.BlockSpec((tm // TQ, IDX_HEADS * TQ, IDX_K), lambda i: (i, 0, 0)),
                   pl.BlockSpec((tm, 2 * LANE), lambda i: (i, 0)),
                   pl.BlockSpec((SUBLANE, tm), lambda i: (0, i)),
                   pl.BlockSpec((tm, POOL_WIDTH), lambda i: (i, 0)),
                   pl.BlockSpec((tm, D_MODEL), lambda i: (i, 0)),
                   pl.BlockSpec((tm, D_MODEL), lambda i: (i, 0))],
        out_shape=[jax.ShapeDtypeStruct((nq, N_PAIR, 2 * TQ, LANE), bf16),
                   jax.ShapeDtypeStruct((s, ATTN_WIDTH), bf16),
                   jax.ShapeDtypeStruct((s // TK, N_PAIR, LANE, TK), bf16),
                   jax.ShapeDtypeStruct((nq, IDX_HEADS * TQ, IDX_K), bf16),
                   jax.ShapeDtypeStruct((s, 2 * LANE), bf16),
                   jax.ShapeDtypeStruct((SUBLANE, s), f32),
                   jax.ShapeDtypeStruct((s, POOL_WIDTH), f32),
                   jax.ShapeDtypeStruct((s, D_MODEL), bf16),
                   jax.ShapeDtypeStruct((s, D_MODEL), bf16)],
        compiler_params=pltpu.CompilerParams(dimension_semantics=("parallel",),
                                             vmem_limit_bytes=VMEM_LIMIT),
        name="proj",
    )(x2, mod, g1, w_in_p, qg, kg, widx)


def _slope2(h):
    return (2.0 ** (-8.0 * (h + 1) / N_HEADS)) * LOG2E


def _tree(op, xs):
    xs = list(xs)
    while len(xs) > 1:
        nxt = [op(xs[a], xs[a + 1]) for a in range(0, len(xs) - 1, 2)]
        if len(xs) % 2:
            nxt.append(xs[-1])
        xs = nxt
    return xs[0]


def _row_groups(x):
    return [x[j * SUBLANE:(j + 1) * SUBLANE] for j in range(x.shape[0] // SUBLANE)]


N_CNT_ACC = 4


def _attn_kernel(qbd_ref, qi_ref, wi_ref, k_ref, vt_ref, ki_ref, o_ref,
                 keys_ref, sb_ref, il0_ref, il1_ref, pre0_ref, pre1_ref, sm0_ref, sm1_ref,
                 cmax0_ref, cmax1_ref, p_ref, m_ref, acc_ref, *, topk):
    il_ref = (il0_ref, il1_ref)
    pre_ref = (pre0_ref, pre1_ref)
    sm_ref = (sm0_ref, sm1_ref)
    cmax_ref = (cmax0_ref, cmax1_ref)
    i = pl.program_id(0)
    nsc = (i + 4) >> 2
    nch = 2 * nsc

    def chunk_start(c):
        return pl.multiple_of(c * TK, TK)

    def pipeline(produce, consume, carry):
        produce(0, 0)
        produce(1, 1)

        def body(sc, carry):
            c0 = 2 * sc
            carry = consume(c0, 0, carry)
            produce(c0 + 2, 0)
            carry = consume(c0 + 1, 1, carry)
            produce(c0 + 3, 1)
            return carry

        carry = lax.fori_loop(0, nsc - 1, body, carry)
        carry = consume(nch - 2, 0, carry)
        return consume(nch - 1, 1, carry)

    lane2 = lax.broadcasted_iota(i32, (1, 2 * TQ), 1)

    @pl.when(i == 0)
    def _():
        row = lax.broadcasted_iota(i32, (TK, 2 * TQ), 0).astype(f32)
        lane = lax.broadcasted_iota(i32, (TK, 2 * TQ), 1)
        for p in range(N_PAIR):
            sb_ref[p] = row * jnp.where(lane < TQ, _slope2(2 * p), _slope2(2 * p + 1))

    qi = qi_ref[0]
    w = wi_ref[...]
    d0 = (lax.broadcasted_iota(i32, (TK, TQ), 1) - lax.broadcasted_iota(i32, (TK, TQ), 0))

    def idx_matmul(c, slot):
        rows = pl.ds(chunk_start(c), TK)
        kx = ki_ref[rows, 0:LANE]
        ky = ki_ref[rows, LANE:2 * LANE]
        il_ref[slot][...] = _nt_dot(jnp.concatenate([kx, kx, kx, ky], axis=1), qi)

    def idx_keys(c, slot, carry):
        r0 = chunk_start(c)
        sc = jnp.maximum(il_ref[slot][:, 0:TQ], 0.0) * w[0:1, :]
        for hh in range(1, IDX_HEADS):
            sc = sc + jnp.maximum(il_ref[slot][:, hh * TQ:(hh + 1) * TQ], 0.0) * w[hh:hh + 1, :]
        b = lax.bitcast_convert_type(sc, i32)
        key = jnp.where(b < 0, -(b & 0x7FFFFFFF), b)
        valid = d0 >= (r0 - i * TQ)
        keys_ref[pl.ds(r0, TK), :] = jnp.where(valid, key, INT_MIN)
        return carry

    pipeline(idx_matmul, idx_keys, 0)

    def count_ge(cand):
        def body(sc, accs):
            r0 = pl.multiple_of(sc * 2 * TK, 2 * TK)
            ind = jnp.where(keys_ref[pl.ds(r0, 2 * TK), :] >= cand, 1, 0)
            g = _row_groups(ind)
            n = len(g) // N_CNT_ACC
            return tuple(accs[a] + _tree(jnp.add, g[a * n:(a + 1) * n]) for a in range(N_CNT_ACC))
        accs = lax.fori_loop(0, nsc, body,
                             tuple(jnp.zeros((SUBLANE, TQ), i32) for _ in range(N_CNT_ACC)))
        return jnp.sum(_tree(jnp.add, accs), axis=0, keepdims=True)

    def bit_step(bi, t):
        cand = t + lax.shift_left(jnp.int32(1), 31 - bi)
        return jnp.where(count_ge(cand) >= topk, cand, t)

    t = lax.fori_loop(0, 32, bit_step, jnp.full((1, TQ), INT_MIN, i32))
    t = jnp.maximum(t, INT_MIN + 1)
    r_tie = (topk - count_ge(t + 1)).astype(f32)

    tri = jnp.where(lax.broadcasted_iota(i32, (TK, TK), 0) >= lax.broadcasted_iota(i32, (TK, TK), 1),
                    1.0, 0.0).astype(bf16)

    def tie_rank(c, slot):
        e = jnp.where(keys_ref[pl.ds(chunk_start(c), TK), :] == t, 1.0, 0.0).astype(bf16)
        pre_ref[slot][...] = jnp.dot(tri, e, preferred_element_type=f32)

    def mask_out(c, slot, rank):
        r0 = chunk_start(c)
        kk = keys_ref[pl.ds(r0, TK), :]
        pre = pre_ref[slot][...] + rank
        nm = jnp.where(kk > t, 0.0, jnp.where(kk == t, jnp.where(pre <= r_tie, 0.0, NEG_INF), NEG_INF))
        keys_ref[pl.ds(r0, TK), :] = lax.bitcast_convert_type(nm, i32)
        return pre[TK - 1:TK, :]

    pipeline(tie_rank, mask_out, jnp.zeros((1, TQ), f32))

    m_ref[...] = jnp.full(m_ref.shape, NEG_INF, f32)
    acc_ref[...] = jnp.zeros(acc_ref.shape, f32)

    def qk_stage(c, slot):
        r0 = chunk_start(c)
        for p in range(N_PAIR):
            nm = lax.bitcast_convert_type(keys_ref[pl.ds(r0, TK), :], f32)
            s = _nt_dot(k_ref[pl.ds(r0, TK), p * LANE:(p + 1) * LANE], qbd_ref[0, p])
            sm = (s + sb_ref[p]) + jnp.concatenate([nm, nm], axis=1)
            sm_ref[slot][p] = sm
            cmax_ref[slot][p] = jnp.max(_tree(jnp.maximum, _row_groups(sm)), axis=0, keepdims=True)

    ones_rows = jnp.ones((VT_ROWS - LANE, TK), bf16)

    def pv_stage(c, slot, carry):
        r0f = (c * TK).astype(f32)
        for p in range(N_PAIR):
            coff = jnp.where(lane2 < TQ, _slope2(2 * p), _slope2(2 * p + 1)) * r0f
            m_old = m_ref[p]
            m_new = jnp.maximum(m_old, cmax_ref[slot][p] + coff)
            m_safe = jnp.where(m_new == NEG_INF, 0.0, m_new)
            alpha = jnp.where(m_old == NEG_INF, 0.0, jnp.exp2(m_old - m_safe))
            p_ref[p] = jnp.exp2(sm_ref[slot][p] - (m_safe - coff)).astype(bf16)
            pv = jnp.dot(jnp.concatenate([vt_ref[c, p], ones_rows], axis=0), p_ref[p],
                         preferred_element_type=f32)
            acc_ref[p] = acc_ref[p] * alpha + pv
            m_ref[p] = m_new
        return carry

    pipeline(qk_stage, pv_stage, 0)

    outs = []
    for p in range(N_PAIR):
        a = acc_ref[p]
        o = a[0:LANE] / a[LANE:LANE + 1]
        outs.append(o[0:HEAD_DIM, 0:TQ])
        outs.append(o[HEAD_DIM:2 * HEAD_DIM, TQ:2 * TQ])
    o_ref[...] = jnp.concatenate(outs, axis=0).T.astype(bf16)


def _attn_call(qbd, qi, wit, k, vt, ki, topk):
    s = k.shape[0]
    nq = s // TQ
    whole = pl.BlockSpec(memory_space=pltpu.VMEM)
    return pl.pallas_call(
        functools.partial(_attn_kernel, topk=topk),
        grid=(nq,),
        in_specs=[pl.BlockSpec((1, N_PAIR, 2 * TQ, LANE), lambda i: (i, 0, 0, 0)),
                  pl.BlockSpec((1, IDX_HEADS * TQ, IDX_K), lambda i: (i, 0, 0)),
                  pl.BlockSpec((SUBLANE, TQ), lambda i: (0, i)),
                  whole, whole, whole],
        out_specs=pl.BlockSpec((TQ, ATTN_WIDTH), lambda i: (i, 0)),
        out_shape=jax.ShapeDtypeStruct((s, ATTN_WIDTH), bf16),
        scratch_shapes=[pltpu.VMEM((s, TQ), i32),
                        pltpu.VMEM((N_PAIR, TK, 2 * TQ), f32),
                        pltpu.VMEM((TK, IDX_HEADS * TQ), f32),
                        pltpu.VMEM((TK, IDX_HEADS * TQ), f32),
                        pltpu.VMEM((TK, TQ), f32),
                        pltpu.VMEM((TK, TQ), f32),
                        pltpu.VMEM((N_PAIR, TK, 2 * TQ), f32),
                        pltpu.VMEM((N_PAIR, TK, 2 * TQ), f32),
                        pltpu.VMEM((N_PAIR, 1, 2 * TQ), f32),
                        pltpu.VMEM((N_PAIR, 1, 2 * TQ), f32),
                        pltpu.VMEM((N_PAIR, TK, 2 * TQ), bf16),
                        pltpu.VMEM((N_PAIR, 1, 2 * TQ), f32),
                        pltpu.VMEM((N_PAIR, VT_ROWS, 2 * TQ), f32)],
        compiler_params=pltpu.CompilerParams(dimension_semantics=("arbitrary",),
                                             vmem_limit_bytes=VMEM_LIMIT),
        name="attn",
    )(qbd, qi, wit, k, vt, ki)


HALO_POOL = 16


def _mix_kernel(x_ref, attn_ref, u_ref, uh_ref, ga_ref, gp_ref, mod_ref,
                wab_ref, wg_ref, ps_ref, wpb_ref, wo_ref, o_ref, *, tm):
    i = pl.program_id(0)
    y_attn = jnp.dot(attn_ref[...], wab_ref[...], preferred_element_type=f32)

    u = u_ref[...]
    halo = jnp.where(i > 0, uh_ref[...], 0.0)
    a = jnp.concatenate([halo, u], axis=0)
    tpos = (i * tm + lax.broadcasted_iota(i32, (tm, POOL_GROUP_DIM), 0) + 1).astype(f32)
    mixed = []
    for g, wdw in enumerate(POOL_WINDOWS):
        ag = a[:, g * POOL_GROUP_DIM:(g + 1) * POOL_GROUP_DIM]
        ug = ag[HALO_POOL:HALO_POOL + tm]
        ssum = ug
        for j in range(1, wdw):
            ssum = ssum + ag[HALO_POOL - j:HALO_POOL - j + tm]
        pooled = ssum / jnp.minimum(tpos, float(wdw)) - ug
        mixed.append(jnp.dot(pooled.astype(bf16), wg_ref[g], preferred_element_type=f32))
    mixed = jnp.concatenate(mixed, axis=1) * ps_ref[...]
    y_pool = jnp.dot(mixed.astype(bf16), wpb_ref[...], preferred_element_type=f32)

    merged = ga_ref[...].astype(f32) * y_attn + gp_ref[...].astype(f32) * y_pool
    o = jnp.dot(merged.astype(bf16), wo_ref[...], preferred_element_type=f32)
    gate = mod_ref[0:1, 2 * D_MODEL:3 * D_MODEL]
    o_ref[...] = x_ref[...] + gate * o


def _mix_call(x2, attn, u, ga, gp, mod, wab, wg, ps, wpb, wo, tm):
    s = x2.shape[0]
    const2 = lambda i: (0, 0)
    hb = tm // HALO_POOL
    return pl.pallas_call(
        functools.partial(_mix_kernel, tm=tm),
        grid=(s // tm,),
        in_specs=[pl.BlockSpec((tm, D_MODEL), lambda i: (i, 0)),
                  pl.BlockSpec((tm, ATTN_WIDTH), lambda i: (i, 0)),
                  pl.BlockSpec((tm, POOL_WIDTH), lambda i: (i, 0)),
                  pl.BlockSpec((HALO_POOL, POOL_WIDTH), lambda i: (jnp.maximum(i * hb - 1, 0), 0)),
                  pl.BlockSpec((tm, D_MODEL), lambda i: (i, 0)),
                  pl.BlockSpec((tm, D_MODEL), lambda i: (i, 0)),
                  pl.BlockSpec((SUBLANE, N_MOD * D_MODEL), const2),
                  pl.BlockSpec((ATTN_WIDTH, D_MODEL), const2),
                  pl.BlockSpec((len(POOL_WINDOWS), POOL_GROUP_DIM, POOL_GROUP_DIM), lambda i: (0, 0, 0)),
                  pl.BlockSpec((1, POOL_WIDTH), const2),
                  pl.BlockSpec((POOL_WIDTH, D_MODEL), const2),
                  pl.BlockSpec((D_MODEL, D_MODEL), const2)],
        out_specs=pl.BlockSpec((tm, D_MODEL), lambda i: (i, 0)),
        out_shape=jax.ShapeDtypeStruct((s, D_MODEL), f32),
        compiler_params=pltpu.CompilerParams(dimension_semantics=("parallel",),
                                             vmem_limit_bytes=VMEM_LIMIT),
        name="mix",
    )(x2, attn, u, u, ga, gp, mod, wab, wg, ps, wpb, wo)


HALO_CONV = 8


def _ffn_kernel(x_ref, xh_ref, mod_ref, g2_ref, wup_ref, cw_ref, cb_ref, wdn_ref, o_ref, *, tm):
    i = pl.program_id(0)
    shift = mod_ref[0:1, 3 * D_MODEL:4 * D_MODEL]
    scale = mod_ref[0:1, 4 * D_MODEL:5 * D_MODEL]
    gate = mod_ref[0:1, 5 * D_MODEL:6 * D_MODEL]
    g2 = g2_ref[...]
    x = x_ref[...]
    h = _rms_modulate(x, g2, shift, scale)
    hh = jnp.where(i > 0, _rms_modulate(xh_ref[...], g2, shift, scale), 0.0)
    ha = jnp.concatenate([hh, h], axis=0).astype(bf16)
    up = jnp.dot(ha, wup_ref[...], preferred_element_type=f32)
    cw = cw_ref[...]
    y = cb_ref[...] + cw[0:1, :] * up[HALO_CONV - 2:HALO_CONV - 2 + tm]
    y = y + cw[1:2, :] * up[HALO_CONV - 1:HALO_CONV - 1 + tm]
    y = y + cw[2:3, :] * up[HALO_CONV:HALO_CONV + tm]
    a = y[:, 0:D_FF]
    b = y[:, D_FF:2 * D_FF]
    gated = (a * _sigmoid(a)) * b
    o = jnp.dot(gated.astype(bf16), wdn_ref[...], preferred_element_type=f32)
    o_ref[...] = x + gate * o


def _ffn_call(x1, mod, g2, wup, cw, cb, wdn, tm):
    s = x1.shape[0]
    const2 = lambda i: (0, 0)
    hb = tm // HALO_CONV
    return pl.pallas_call(
        functools.partial(_ffn_kernel, tm=tm),
        grid=(s // tm,),
        in_specs=[pl.BlockSpec((tm, D_MODEL), lambda i: (i, 0)),
                  pl.BlockSpec((HALO_CONV, D_MODEL), lambda i: (jnp.maximum(i * hb - 1, 0), 0)),
                  pl.BlockSpec((SUBLANE, N_MOD * D_MODEL), const2),
                  pl.BlockSpec((1, D_MODEL), const2),
                  pl.BlockSpec((D_MODEL, 2 * D_FF), const2),
                  pl.BlockSpec((3, 2 * D_FF), const2),
                  pl.BlockSpec((1, 2 * D_FF), const2),
                  pl.BlockSpec((D_FF, D_MODEL), const2)],
        out_specs=pl.BlockSpec((tm, D_MODEL), lambda i: (i, 0)),
        out_shape=jax.ShapeDtypeStruct((s, D_MODEL), f32),
        compiler_params=pltpu.CompilerParams(dimension_semantics=("parallel",),
                                             vmem_limit_bytes=VMEM_LIMIT),
        name="ffn",
    )(x1, x1, mod, g2, wup, cw, cb, wdn)


def _pack_w_in(w):
    q, k, v, qi, ki, wi, u, ga, gp = (w[:, 0:512], w[:, 512:1024], w[:, 1024:1536], w[:, 1536:1792],
                                      w[:, 1792:1856], w[:, 1856:1860], w[:, 1860:2372],
                                      w[:, 2372:3396], w[:, 3396:4420])
    wi_pad = jnp.pad(wi, ((0, 0), (0, LANE - IDX_HEADS)))
    return jnp.concatenate([q, k, v, qi, ki, ki, wi_pad, u, ga, gp], axis=1).astype(bf16)


def _pack_w_idx(w):
    return jnp.pad(w[:, 1536:1860], ((0, 0), (0, 3 * LANE - (1860 - 1536))))


def kernel(x, c, w_ada, b_ada, norm1_g, w_in, q_norm_g, k_norm_g, w_attn_br, w_pool_grp,
           pool_scale, w_pool_br, w_out, norm2_g, w_up, conv_w, conv_b, w_down):
    bsz, s, d = x.shape
    assert bsz == 1 and d == D_MODEL and s % 512 == 0
    depth = w_ada.shape[0]
    topk = min(TOPK_MAX, s // 4)
    x2 = x.reshape(s, d)
    c8 = jnp.pad(c, ((0, SUBLANE - bsz), (0, 0)))
    for l in range(depth):
        mod = _mod_call(c8, w_ada[l], b_ada[l].reshape(1, -1))
        qbd, k, vt, qi, ki, wit, u, ga, gp = _proj_call(
            x2, mod, norm1_g[l].reshape(1, -1), _pack_w_in(w_in[l]),
            jnp.tile(q_norm_g[l], N_HEADS).reshape(1, -1),
            jnp.tile(k_norm_g[l], N_HEADS).reshape(1, -1), _pack_w_idx(w_in[l]), tm=512)
        attn = _attn_call(qbd, qi, wit, k, vt, ki, topk)
        x2 = _mix_call(x2, attn, u, ga, gp, mod, w_attn_br[l].astype(bf16),
                       w_pool_grp[l].astype(bf16), pool_scale[l].reshape(1, -1),
                       w_pool_br[l].astype(bf16), w_out[l].astype(bf16), tm=512)
        x2 = _ffn_call(x2, mod, norm2_g[l].reshape(1, -1), w_up[l].astype(bf16), conv_w[l],
                       conv_b[l].reshape(1, -1), w_down[l].astype(bf16), tm=256)
    return x2.reshape(bsz, s, d)
```

```python
import functools
import math

import jax
import jax.numpy as jnp
from jax import lax
from jax.experimental import pallas as pl
from jax.experimental.pallas import tpu as pltpu

f32 = jnp.float32
bf16 = jnp.bfloat16
i32 = jnp.int32

D_MODEL = 1024
N_HEADS = 8
HEAD_DIM = 64
ATTN_WIDTH = N_HEADS * HEAD_DIM
IDX_HEADS = 4
IDX_DIM = 64
TOPK_MAX = 256
POOL_WINDOWS = (2, 4, 8, 16)
POOL_GROUP_DIM = 128
POOL_WIDTH = 512
D_FF = 2816
EPS = 1e-6
N_MOD = 6

LANE = 128
SUBLANE = 8
VMEM_LIMIT = 58 * 1024 * 1024

LOG2E = 1.4426950408889634
INT_MIN = -2147483648
NEG_INF = float("-inf")

C_Q, C_K, C_V, C_QI, C_KI, C_WI, C_U, C_GA, C_GP, C_END = (
    0, 512, 1024, 1536, 1792, 1920, 2048, 2560, 3584, 4608)

TQ = 128
TK = 256
N_PAIR = N_HEADS // 2
IDX_K = 4 * LANE


def _nt_dot(a, b):
    return lax.dot_general(a, b, (((1,), (1,)), ((), ())), preferred_element_type=f32)


def _sigmoid(x):
    return 1.0 / (1.0 + jnp.exp(-x))


def _rms_modulate(x, g, shift, scale):
    y = x * lax.rsqrt(jnp.mean(x * x, axis=-1, keepdims=True) + EPS)
    return (y * g) * (1.0 + scale) + shift


def _mod_kernel(c_ref, w_ref, b_ref, o_ref):
    c = c_ref[...]
    sc = c * _sigmoid(c)
    o_ref[...] = jnp.dot(sc, w_ref[...], precision=lax.Precision.HIGHEST,
                         preferred_element_type=f32) + b_ref[...]


def _mod_call(c8, w_ada, b_ada):
    n = w_ada.shape[1]
    tn = 1024
    return pl.pallas_call(
        _mod_kernel,
        grid=(n // tn,),
        in_specs=[pl.BlockSpec((SUBLANE, D_MODEL), lambda j: (0, 0)),
                  pl.BlockSpec((D_MODEL, tn), lambda j: (0, j)),
                  pl.BlockSpec((1, tn), lambda j: (0, j))],
        out_specs=pl.BlockSpec((SUBLANE, tn), lambda j: (0, j)),
        out_shape=jax.ShapeDtypeStruct((SUBLANE, n), f32),
        name="mod",
    )(c8, w_ada, b_ada)


def _split3(x):
    hi = x.astype(bf16).astype(f32)
    r = x - hi
    mid = r.astype(bf16).astype(f32)
    lo = (r - mid).astype(bf16).astype(f32)
    return hi, mid, lo


def _head_norm(z, g, bd):
    z2 = z * z
    hi = z2.astype(bf16)
    lo = (z2 - hi.astype(f32)).astype(bf16)
    ms = jnp.dot(hi, bd, preferred_element_type=f32) + jnp.dot(lo, bd, preferred_element_type=f32)
    return (z * lax.rsqrt(ms + EPS)) * g


def _proj_kernel(x_ref, mod_ref, g1_ref, w_ref, qg_ref, kg_ref, widx_ref,
                 qbd_ref, k_ref, vt_ref, qi_ref, ki_ref, wit_ref, u_ref, ga_ref, gp_ref, *, tm):
    x = x_ref[...]
    shift = mod_ref[0:1, 0:D_MODEL]
    scale = mod_ref[0:1, D_MODEL:2 * D_MODEL]
    h = _rms_modulate(x, g1_ref[...], shift, scale)
    proj = jnp.dot(h.astype(bf16), w_ref[...], preferred_element_type=f32)

    r = lax.broadcasted_iota(i32, (ATTN_WIDTH, ATTN_WIDTH), 0)
    c = lax.broadcasted_iota(i32, (ATTN_WIDTH, ATTN_WIDTH), 1)
    bd = jnp.where((r >> 6) == (c >> 6), 1.0 / HEAD_DIM, 0.0).astype(bf16)

    q = _head_norm(proj[:, C_Q:C_K], qg_ref[...], bd) * (HEAD_DIM ** -0.5 * LOG2E)
    k = _head_norm(proj[:, C_K:C_V], kg_ref[...], bd)
    k_ref[...] = k.astype(bf16)

    low = lax.broadcasted_iota(i32, (TQ, LANE), 1) < HEAD_DIM
    low_tm = lax.broadcasted_iota(i32, (tm, LANE), 1) < HEAD_DIM
    pidx = jnp.dot(h, widx_ref[...], precision=lax.Precision.HIGHEST, preferred_element_type=f32)
    qi = pidx[:, 0:256] * (IDX_DIM ** -0.5)
    up64 = lambda z: pltpu.roll(z, HEAD_DIM, axis=1)
    for g in range(tm // TQ):
        rows = slice(g * TQ, (g + 1) * TQ)
        for p in range(N_PAIR):
            qp = q[rows, p * LANE:(p + 1) * LANE]
            qbd_ref[g, p, 0:TQ, :] = jnp.where(low, qp, 0.0).astype(bf16)
            qbd_ref[g, p, TQ:2 * TQ, :] = jnp.where(low, 0.0, qp).astype(bf16)
        for hh in range(IDX_HEADS):
            seg = qi[rows, (hh // 2) * LANE:(hh // 2 + 1) * LANE]
            own = low if hh % 2 == 0 else jnp.logical_not(low)
            qh, qm, ql = _split3(jnp.where(own, seg, 0.0))
            both = lambda z: z + up64(z)
            in_low = lambda z: jnp.where(low, both(z), 0.0)
            hrows = slice(hh * TQ, (hh + 1) * TQ)
            qi_ref[g, hrows, 0:LANE] = both(qh).astype(bf16)
            qi_ref[g, hrows, LANE:2 * LANE] = both(qm).astype(bf16)
            qi_ref[g, hrows, 2 * LANE:3 * LANE] = in_low(ql).astype(bf16)
            qi_ref[g, hrows, 3 * LANE:4 * LANE] = in_low(qh).astype(bf16)

    v = proj[:, C_V:C_QI]
    vt = v.T.astype(bf16)
    for cc in range(tm // TK):
        for hh in range(N_HEADS):
            vt_ref[cc, hh] = vt[hh * HEAD_DIM:(hh + 1) * HEAD_DIM, cc * TK:(cc + 1) * TK]

    kw = pidx[:, 256:384]
    kh, km, kl = _split3(jnp.where(low_tm, kw, 0.0))
    ki_ref[:, 0:LANE] = (kh + up64(km)).astype(bf16)
    ki_ref[:, LANE:2 * LANE] = kl.astype(bf16)
    wt = (kw * (IDX_HEADS ** -0.5)).T
    wit_ref[...] = wt[HEAD_DIM:HEAD_DIM + SUBLANE, :]
    u_ref[...] = proj[:, C_U:C_GA]
    ga_ref[...] = _sigmoid(proj[:, C_GA:C_GP]).astype(bf16)
    gp_ref[...] = _sigmoid(proj[:, C_GP:C_END]).astype(bf16)


def _proj_call(x2, mod, g1, w_in_p, qg, kg, widx, tm):
    s = x2.shape[0]
    nq = s // TQ
    const = lambda i: (0, 0)
    return pl.pallas_call(
        functools.partial(_proj_kernel, tm=tm),
        grid=(s // tm,),
        in_specs=[pl.BlockSpec((tm, D_MODEL), lambda i: (i, 0)),
                  pl.BlockSpec((SUBLANE, N_MOD * D_MODEL), const),
                  pl.BlockSpec((1, D_MODEL), const),
                  pl.BlockSpec((D_MODEL, C_END), const),
                  pl.BlockSpec((1, ATTN_WIDTH), const),
                  pl.BlockSpec((1, ATTN_WIDTH), const),
                  pl.BlockSpec((D_MODEL, 3 * LANE), const)],
        out_specs=[pl.BlockSpec((tm // TQ, N_PAIR, 2 * TQ, LANE), lambda i: (i, 0, 0, 0)),
                   pl.BlockSpec((tm, ATTN_WIDTH), lambda i: (i, 0)),
                   pl.BlockSpec((tm // TK, N_HEADS, HEAD_DIM, TK), lambda i: (i, 0, 0, 0)),
                   pl.BlockSpec((tm // TQ, IDX_HEADS * TQ, IDX_K), lambda i: (i, 0, 0)),
                   pl.BlockSpec((tm, 2 * LANE), lambda i: (i, 0)),
                   pl.BlockSpec((SUBLANE, tm), lambda i: (0, i)),
                   pl.BlockSpec((tm, POOL_WIDTH), lambda i: (i, 0)),
                   pl.BlockSpec((tm, D_MODEL), lambda i: (i, 0)),
                   pl.BlockSpec((tm, D_MODEL), lambda i: (i, 0))],
        out_shape=[jax.ShapeDtypeStruct((nq, N_PAIR, 2 * TQ, LANE), bf16),
                   jax.ShapeDtypeStruct((s, ATTN_WIDTH), bf16),
                   jax.ShapeDtypeStruct((s // TK, N_HEADS, HEAD_DIM, TK), bf16),
                   jax.ShapeDtypeStruct((nq, IDX_HEADS * TQ, IDX_K), bf16),
                   jax.ShapeDtypeStruct((s, 2 * LANE), bf16),
                   jax.ShapeDtypeStruct((SUBLANE, s), f32),
                   jax.ShapeDtypeStruct((s, POOL_WIDTH), f32),
                   jax.ShapeDtypeStruct((s, D_MODEL), bf16),
                   jax.ShapeDtypeStruct((s, D_MODEL), bf16)],
        compiler_params=pltpu.CompilerParams(dimension_semantics=("parallel",),
                                             vmem_limit_bytes=VMEM_LIMIT),
        name="proj",
    )(x2, mod, g1, w_in_p, qg, kg, widx)


def _slope2(h):
    return (2.0 ** (-8.0 * (h + 1) / N_HEADS)) * LOG2E


def _tree(op, xs):
    xs = list(xs)
    while len(xs) > 1:
        nxt = [op(xs[a], xs[a + 1]) for a in range(0, len(xs) - 1, 2)]
        if len(xs) % 2:
            nxt.append(xs[-1])
        xs = nxt
    return xs[0]


def _row_groups(x):
    return [x[j * SUBLANE:(j + 1) * SUBLANE] for j in range(x.shape[0] // SUBLANE)]


N_CNT_ACC = 4
N_STAGE = 4
PV_ROWS = HEAD_DIM + 16


def _attn_kernel(qbd_ref, qi_ref, wi_ref, k_ref, vt_ref, ki_ref, o_ref,
                 keys_ref, pos_ref, slf_ref, il0_ref, il1_ref, il2_ref, il3_ref,
                 pre0_ref, pre1_ref, pre2_ref, pre3_ref, sm0_ref, sm1_ref, cmax0_ref, cmax1_ref,
                 p0_ref, p1_ref, alpha0_ref, alpha1_ref, m_ref, acc_ref, *, topk):
    il_ref = (il0_ref, il1_ref, il2_ref, il3_ref)
    pre_ref = (pre0_ref, pre1_ref, pre2_ref, pre3_ref)
    sm_ref = (sm0_ref, sm1_ref)
    cmax_ref = (cmax0_ref, cmax1_ref)
    p_ref = (p0_ref, p1_ref)
    alpha_ref = (alpha0_ref, alpha1_ref)
    i = pl.program_id(0)
    nsc = (i + 4) >> 2
    nch = 2 * nsc
    nbody = (i + 8) >> 3

    def chunk_start(c):
        return pl.multiple_of(c * TK, TK)

    def pipeline(produce, consume, carry):
        for j in range(N_STAGE):
            produce(j, j)

        def body(b, carry):
            c0 = N_STAGE * b
            for j in range(N_STAGE):
                carry = consume(c0 + j, j, carry)
                produce(c0 + N_STAGE + j, j)
            return carry

        carry = lax.fori_loop(0, nbody - 1, body, carry)
        c0 = N_STAGE * (nbody - 1)
        for j in range(N_STAGE):
            carry = consume(c0 + j, j, carry)
        return carry

    @pl.when(i == 0)
    def _():
        col = lax.broadcasted_iota(i32, (TK, LANE), 1)
        row = lax.broadcasted_iota(i32, (TK, LANE), 0).astype(f32)
        pos_ref[...] = jnp.where(col < 3, row, 0.0).astype(bf16)
        colq = lax.broadcasted_iota(i32, (2 * TQ, LANE), 1)
        rowq = lax.broadcasted_iota(i32, (2 * TQ, LANE), 0)
        for p in range(N_PAIR):
            sl = jnp.where(rowq < TQ, _slope2(2 * p), _slope2(2 * p + 1)) + jnp.zeros((2 * TQ, LANE), f32)
            hi, mid, lo = _split3(sl)
            slf_ref[p] = jnp.where(colq == 0, hi, jnp.where(colq == 1, mid,
                                   jnp.where(colq == 2, lo, 0.0))).astype(bf16)

    qi = qi_ref[0]
    w = wi_ref[...]
    d0 = (lax.broadcasted_iota(i32, (TK, TQ), 1) - lax.broadcasted_iota(i32, (TK, TQ), 0))

    def idx_matmul(c, slot):
        rows = pl.ds(chunk_start(c), TK)
        kx = ki_ref[rows, 0:LANE]
        ky = ki_ref[rows, LANE:2 * LANE]
        il_ref[slot][...] = _nt_dot(jnp.concatenate([kx, kx, kx, ky], axis=1), qi)

    def idx_keys(c, slot, carry):
        r0 = chunk_start(c)
        sc = jnp.maximum(il_ref[slot][:, 0:TQ], 0.0) * w[0:1, :]
        for hh in range(1, IDX_HEADS):
            sc = sc + jnp.maximum(il_ref[slot][:, hh * TQ:(hh + 1) * TQ], 0.0) * w[hh:hh + 1, :]
        b = lax.bitcast_convert_type(sc, i32)
        key = jnp.where(b < 0, -(b & 0x7FFFFFFF), b)
        valid = d0 >= (r0 - i * TQ)
        keys_ref[pl.ds(r0, TK), :] = jnp.where(valid, key, INT_MIN)
        return carry

    pipeline(idx_matmul, idx_keys, 0)

    def count_ge(cand):
        def body(sc, accs):
            r0 = pl.multiple_of(sc * 2 * TK, 2 * TK)
            ind = jnp.where(keys_ref[pl.ds(r0, 2 * TK), :] >= cand, 1, 0)
            g = _row_groups(ind)
            n = len(g) // N_CNT_ACC
            return tuple(accs[a] + _tree(jnp.add, g[a * n:(a + 1) * n]) for a in range(N_CNT_ACC))
        accs = lax.fori_loop(0, nsc, body,
                             tuple(jnp.zeros((SUBLANE, TQ), i32) for _ in range(N_CNT_ACC)))
        return jnp.sum(_tree(jnp.add, accs), axis=0, keepdims=True)

    def bit_step(bi, t):
        cand = t + lax.shift_left(jnp.int32(1), 31 - bi)
        return jnp.where(count_ge(cand) >= topk, cand, t)

    t = lax.fori_loop(0, 32, bit_step, jnp.full((1, TQ), INT_MIN, i32))
    t = jnp.maximum(t, INT_MIN + 1)
    r_tie = (topk - count_ge(t + 1)).astype(f32)

    tri = jnp.where(lax.broadcasted_iota(i32, (TK, TK), 0) >= lax.broadcasted_iota(i32, (TK, TK), 1),
                    1.0, 0.0).astype(bf16)

    def tie_rank(c, slot):
        e = jnp.where(keys_ref[pl.ds(chunk_start(c), TK), :] == t, 1.0, 0.0).astype(bf16)
        pre_ref[slot][...] = jnp.dot(tri, e, preferred_element_type=f32)

    def mask_out(c, slot, rank):
        r0 = chunk_start(c)
        kk = keys_ref[pl.ds(r0, TK), :]
        pre = pre_ref[slot][...] + rank
        nm = jnp.where(kk > t, 0.0, jnp.where(kk == t, jnp.where(pre <= r_tie, 0.0, NEG_INF), NEG_INF))
        keys_ref[pl.ds(r0, TK), :] = lax.bitcast_convert_type(nm, i32)
        return pre[TK - 1:TK, :]

    pipeline(tie_rank, mask_out, jnp.zeros((1, TQ), f32))

    m_ref[...] = jnp.full(m_ref.shape, NEG_INF, f32)
    acc_ref[...] = jnp.zeros(acc_ref.shape, f32)
    lane2 = lax.broadcasted_iota(i32, (1, 2 * TQ), 1)
    ones_rows = jnp.ones((PV_ROWS - HEAD_DIM, TK), bf16)

    def logits(c, slot):
        rows = pl.ds(chunk_start(c), TK)
        nm = lax.bitcast_convert_type(keys_ref[rows, :], f32)
        nm2 = jnp.concatenate([nm, nm], axis=1)
        for p in range(N_PAIR):
            lhs = jnp.concatenate([k_ref[rows, p * LANE:(p + 1) * LANE], pos_ref[...]], axis=1)
            rhs = jnp.concatenate([qbd_ref[0, p], slf_ref[p]], axis=1)
            sm = _nt_dot(lhs, rhs) + nm2
            sm_ref[slot][p] = sm
            cmax_ref[slot][p] = jnp.max(_tree(jnp.maximum, _row_groups(sm)), axis=0, keepdims=True)

    def probs(c, slot):
        r0f = jnp.asarray(c * TK, dtype=f32)
        for p in range(N_PAIR):
            coff = jnp.where(lane2 < TQ, _slope2(2 * p), _slope2(2 * p + 1)) * r0f
            m_old = m_ref[p]
            m_new = jnp.maximum(m_old, cmax_ref[slot][p] + coff)
            m_safe = jnp.where(m_new == NEG_INF, 0.0, m_new)
            alpha_ref[slot][p] = jnp.where(m_old == NEG_INF, 0.0, jnp.exp2(m_old - m_safe))
            p_ref[slot][p] = jnp.exp2((sm_ref[slot][p] - (m_safe - coff)).astype(bf16))
            m_ref[p] = m_new

    def weighted_sum(c, slot):
        for h in range(N_HEADS):
            lanes = slice((h % 2) * TQ, (h % 2 + 1) * TQ)
            lhs = jnp.concatenate([vt_ref[c, h], ones_rows], axis=0)
            pv = jnp.dot(lhs, p_ref[slot][h // 2, :, lanes], preferred_element_type=f32)
            acc_ref[h] = acc_ref[h] * alpha_ref[slot][h // 2, :, lanes] + pv

    logits(0, 0)
    probs(0, 0)
    logits(1, 1)

    def attn_body(sc, carry):
        c0 = 2 * sc
        weighted_sum(c0, 0)
        probs(c0 + 1, 1)
        logits(c0 + 2, 0)
        weighted_sum(c0 + 1, 1)
        probs(c0 + 2, 0)
        logits(c0 + 3, 1)
        return carry

    lax.fori_loop(0, nsc - 1, attn_body, 0)
    weighted_sum(nch - 2, 0)
    probs(nch - 1, 1)
    weighted_sum(nch - 1, 1)

    outs = []
    for h in range(N_HEADS):
        a = acc_ref[h]
        outs.append(a[0:HEAD_DIM] / a[HEAD_DIM:HEAD_DIM + 1])
    o_ref[...] = jnp.concatenate(outs, axis=0).T.astype(bf16)


def _attn_call(qbd, qi, wit, k, vt, ki, topk):
    s = k.shape[0]
    nq = s // TQ
    whole = pl.BlockSpec(memory_space=pltpu.VMEM)
    il = [pltpu.VMEM((TK, IDX_HEADS * TQ), f32)] * N_STAGE
    pre = [pltpu.VMEM((TK, TQ), f32)] * N_STAGE
    return pl.pallas_call(
        functools.partial(_attn_kernel, topk=topk),
        grid=(nq,),
        in_specs=[pl.BlockSpec((1, N_PAIR, 2 * TQ, LANE), lambda i: (i, 0, 0, 0)),
                  pl.BlockSpec((1, IDX_HEADS * TQ, IDX_K), lambda i: (i, 0, 0)),
                  pl.BlockSpec((SUBLANE, TQ), lambda i: (0, i)),
                  whole, whole, whole],
        out_specs=pl.BlockSpec((TQ, ATTN_WIDTH), lambda i: (i, 0)),
        out_shape=jax.ShapeDtypeStruct((s, ATTN_WIDTH), bf16),
        scratch_shapes=[pltpu.VMEM((s, TQ), i32),
                        pltpu.VMEM((TK, LANE), bf16),
                        pltpu.VMEM((N_PAIR, 2 * TQ, LANE), bf16),
                        *il, *pre,
                        pltpu.VMEM((N_PAIR, TK, 2 * TQ), f32),
                        pltpu.VMEM((N_PAIR, TK, 2 * TQ), f32),
                        pltpu.VMEM((N_PAIR, 1, 2 * TQ), f32),
                        pltpu.VMEM((N_PAIR, 1, 2 * TQ), f32),
                        pltpu.VMEM((N_PAIR, TK, 2 * TQ), bf16),
                        pltpu.VMEM((N_PAIR, TK, 2 * TQ), bf16),
                        pltpu.VMEM((N_PAIR, 1, 2 * TQ), f32),
                        pltpu.VMEM((N_PAIR, 1, 2 * TQ), f32),
                        pltpu.VMEM((N_PAIR, 1, 2 * TQ), f32),
                        pltpu.VMEM((N_HEADS, PV_ROWS, TQ), f32)],
        compiler_params=pltpu.CompilerParams(dimension_semantics=("arbitrary",),
                                             vmem_limit_bytes=VMEM_LIMIT),
        name="attn",
    )(qbd, qi, wit, k, vt, ki)


HALO_POOL = 16


def _mix_kernel(x_ref, attn_ref, u_ref, uh_ref, ga_ref, gp_ref, mod_ref,
                wab_ref, wg_ref, ps_ref, wpb_ref, wo_ref, o_ref, *, tm):
    i = pl.program_id(0)
    y_attn = jnp.dot(attn_ref[...], wab_ref[...], preferred_element_type=f32)

    u = u_ref[...]
    halo = jnp.where(i > 0, uh_ref[...], 0.0)
    a = jnp.concatenate([halo, u], axis=0)
    tpos = (i * tm + lax.broadcasted_iota(i32, (tm, POOL_GROUP_DIM), 0) + 1).astype(f32)
    mixed = []
    for g, wdw in enumerate(POOL_WINDOWS):
        ag = a[:, g * POOL_GROUP_DIM:(g + 1) * POOL_GROUP_DIM]
        ug = ag[HALO_POOL:HALO_POOL + tm]
        ssum = ug
        for j in range(1, wdw):
            ssum = ssum + ag[HALO_POOL - j:HALO_POOL - j + tm]
        pooled = ssum / jnp.minimum(tpos, float(wdw)) - ug
        mixed.append(jnp.dot(pooled.astype(bf16), wg_ref[g], preferred_element_type=f32))
    mixed = jnp.concatenate(mixed, axis=1) * ps_ref[...]
    y_pool = jnp.dot(mixed.astype(bf16), wpb_ref[...], preferred_element_type=f32)

    merged = ga_ref[...].astype(f32) * y_attn + gp_ref[...].astype(f32) * y_pool
    o = jnp.dot(merged.astype(bf16), wo_ref[...], preferred_element_type=f32)
    gate = mod_ref[0:1, 2 * D_MODEL:3 * D_MODEL]
    o_ref[...] = x_ref[...] + gate * o


def _mix_call(x2, attn, u, ga, gp, mod, wab, wg, ps, wpb, wo, tm):
    s = x2.shape[0]
    const2 = lambda i: (0, 0)
    hb = tm // HALO_POOL
    return pl.pallas_call(
        functools.partial(_mix_kernel, tm=tm),
        grid=(s // tm,),
        in_specs=[pl.BlockSpec((tm, D_MODEL), lambda i: (i, 0)),
                  pl.BlockSpec((tm, ATTN_WIDTH), lambda i: (i, 0)),
                  pl.BlockSpec((tm, POOL_WIDTH), lambda i: (i, 0)),
                  pl.BlockSpec((HALO_POOL, POOL_WIDTH), lambda i: (jnp.maximum(i * hb - 1, 0), 0)),
                  pl.BlockSpec((tm, D_MODEL), lambda i: (i, 0)),
                  pl.BlockSpec((tm, D_MODEL), lambda i: (i, 0)),
                  pl.BlockSpec((SUBLANE, N_MOD * D_MODEL), const2),
                  pl.BlockSpec((ATTN_WIDTH, D_MODEL), const2),
                  pl.BlockSpec((len(POOL_WINDOWS), POOL_GROUP_DIM, POOL_GROUP_DIM), lambda i: (0, 0, 0)),
                  pl.BlockSpec((1, POOL_WIDTH), const2),
                  pl.BlockSpec((POOL_WIDTH, D_MODEL), const2),
                  pl.BlockSpec((D_MODEL, D_MODEL), const2)],
        out_specs=pl.BlockSpec((tm, D_MODEL), lambda i: (i, 0)),
        out_shape=jax.ShapeDtypeStruct((s, D_MODEL), f32),
        compiler_params=pltpu.CompilerParams(dimension_semantics=("parallel",),
                                             vmem_limit_bytes=VMEM_LIMIT),
        name="mix",
    )(x2, attn, u, u, ga, gp, mod, wab, wg, ps, wpb, wo)


HALO_CONV = 8


def _ffn_kernel(x_ref, xh_ref, mod_ref, g2_ref, wup_ref, cw_ref, cb_ref, wdn_ref, o_ref, *, tm):
    i = pl.program_id(0)
    shift = mod_ref[0:1, 3 * D_MODEL:4 * D_MODEL]
    scale = mod_ref[0:1, 4 * D_MODEL:5 * D_MODEL]
    gate = mod_ref[0:1, 5 * D_MODEL:6 * D_MODEL]
    g2 = g2_ref[...]
    x = x_ref[...]
    h = _rms_modulate(x, g2, shift, scale)
    hh = jnp.where(i > 0, _rms_modulate(xh_ref[...], g2, shift, scale), 0.0)
    ha = jnp.concatenate([hh, h], axis=0).astype(bf16)
    up = jnp.dot(ha, wup_ref[...], preferred_element_type=f32)
    cw = cw_ref[...]
    y = cb_ref[...] + cw[0:1, :] * up[HALO_CONV - 2:HALO_CONV - 2 + tm]
    y = y + cw[1:2, :] * up[HALO_CONV - 1:HALO_CONV - 1 + tm]
    y = y + cw[2:3, :] * up[HALO_CONV:HALO_CONV + tm]
    a = y[:, 0:D_FF]
    b = y[:, D_FF:2 * D_FF]
    gated = (a * _sigmoid(a)) * b
    o = jnp.dot(gated.astype(bf16), wdn_ref[...], preferred_element_type=f32)
    o_ref[...] = x + gate * o


def _ffn_call(x1, mod, g2, wup, cw, cb, wdn, tm):
    s = x1.shape[0]
    const2 = lambda i: (0, 0)
    hb = tm // HALO_CONV
    return pl.pallas_call(
        functools.partial(_ffn_kernel, tm=tm),
        grid=(s // tm,),
        in_specs=[pl.BlockSpec((tm, D_MODEL), lambda i: (i, 0)),
                  pl.BlockSpec((HALO_CONV, D_MODEL), lambda i: (jnp.maximum(i * hb - 1, 0), 0)),
                  pl.BlockSpec((SUBLANE, N_MOD * D_MODEL), const2),
                  pl.BlockSpec((1, D_MODEL), const2),
                  pl.BlockSpec((D_MODEL, 2 * D_FF), const2),
                  pl.BlockSpec((3, 2 * D_FF), const2),
                  pl.BlockSpec((1, 2 * D_FF), const2),
                  pl.BlockSpec((D_FF, D_MODEL), const2)],
        out_specs=pl.BlockSpec((tm, D_MODEL), lambda i: (i, 0)),
        out_shape=jax.ShapeDtypeStruct((s, D_MODEL), f32),
        compiler_params=pltpu.CompilerParams(dimension_semantics=("parallel",),
                                             vmem_limit_bytes=VMEM_LIMIT),
        name="ffn",
    )(x1, x1, mod, g2, wup, cw, cb, wdn)


def _pack_w_in(w):
    q, k, v, qi, ki, wi, u, ga, gp = (w[:, 0:512], w[:, 512:1024], w[:, 1024:1536], w[:, 1536:1792],
                                      w[:, 1792:1856], w[:, 1856:1860], w[:, 1860:2372],
                                      w[:, 2372:3396], w[:, 3396:4420])
    wi_pad = jnp.pad(wi, ((0, 0), (0, LANE - IDX_HEADS)))
    return jnp.concatenate([q, k, v, qi, ki, ki, wi_pad, u, ga, gp], axis=1).astype(bf16)


def _pack_w_idx(w):
    return jnp.pad(w[:, 1536:1860], ((0, 0), (0, 3 * LANE - (1860 - 1536))))


def kernel(x, c, w_ada, b_ada, norm1_g, w_in, q_norm_g, k_norm_g, w_attn_br, w_pool_grp,
           pool_scale, w_pool_br, w_out, norm2_g, w_up, conv_w, conv_b, w_down):
    bsz, s, d = x.shape
    assert bsz == 1 and d == D_MODEL and s % (N_STAGE * TK) == 0
    depth = w_ada.shape[0]
    topk = min(TOPK_MAX, s // 4)
    x2 = x.reshape(s, d)
    c8 = jnp.pad(c, ((0, SUBLANE - bsz), (0, 0)))
    for l in range(depth):
        mod = _mod_call(c8, w_ada[l], b_ada[l].reshape(1, -1))
        qbd, k, vt, qi, ki, wit, u, ga, gp = _proj_call(
            x2, mod, norm1_g[l].reshape(1, -1), _pack_w_in(w_in[l]),
            jnp.tile(q_norm_g[l], N_HEADS).reshape(1, -1),
            jnp.tile(k_norm_g[l], N_HEADS).reshape(1, -1), _pack_w_idx(w_in[l]), tm=512)
        attn = _attn_call(qbd, qi, wit, k, vt, ki, topk)
        x2 = _mix_call(x2, attn, u, ga, gp, mod, w_attn_br[l].astype(bf16),
                       w_pool_grp[l].astype(bf16), pool_scale[l].reshape(1, -1),
                       w_pool_br[l].astype(bf16), w_out[l].astype(bf16), tm=512)
        x2 = _ffn_call(x2, mod, norm2_g[l].reshape(1, -1), w_up[l].astype(bf16), conv_w[l],
                       conv_b[l].reshape(1, -1), w_down[l].astype(bf16), tm=256)
    return x2.reshape(bsz, s, d)
```

```python
import functools
import math

import jax
import jax.numpy as jnp
from jax import lax
from jax.experimental import pallas as pl
from jax.experimental.pallas import tpu as pltpu

f32 = jnp.float32
bf16 = jnp.bfloat16
i32 = jnp.int32

D_MODEL = 1024
N_HEADS = 8
HEAD_DIM = 64
ATTN_WIDTH = N_HEADS * HEAD_DIM
IDX_HEADS = 4
IDX_DIM = 64
TOPK_MAX = 256
POOL_WINDOWS = (2, 4, 8, 16)
POOL_GROUP_DIM = 128
POOL_WIDTH = 512
D_FF = 2816
EPS = 1e-6
N_MOD = 6

LANE = 128
SUBLANE = 8
VMEM_LIMIT = 58 * 1024 * 1024

LOG2E = 1.4426950408889634
INT_MIN = -2147483648
NEG_INF = float("-inf")

C_Q, C_K, C_V, C_QI, C_KI, C_WI, C_U, C_GA, C_GP, C_END = (
    0, 512, 1024, 1536, 1792, 1920, 2048, 2560, 3584, 4608)

TQ = 128
TK = 256
N_PAIR = N_HEADS // 2
IDX_K = 4 * LANE


def _nt_dot(a, b):
    return lax.dot_general(a, b, (((1,), (1,)), ((), ())), preferred_element_type=f32)


def _sigmoid(x):
    return 1.0 / (1.0 + jnp.exp(-x))


def _rms_modulate(x, g, shift, scale):
    y = x * lax.rsqrt(jnp.mean(x * x, axis=-1, keepdims=True) + EPS)
    return (y * g) * (1.0 + scale) + shift


def _mod_kernel(c_ref, w_ref, b_ref, o_ref):
    c = c_ref[...]
    sc = c * _sigmoid(c)
    o_ref[...] = jnp.dot(sc, w_ref[...], precision=lax.Precision.HIGHEST,
                         preferred_element_type=f32) + b_ref[...]


def _mod_call(c8, w_ada, b_ada):
    n = w_ada.shape[1]
    tn = 1024
    return pl.pallas_call(
        _mod_kernel,
        grid=(n // tn,),
        in_specs=[pl.BlockSpec((SUBLANE, D_MODEL), lambda j: (0, 0)),
                  pl.BlockSpec((D_MODEL, tn), lambda j: (0, j)),
                  pl.BlockSpec((1, tn), lambda j: (0, j))],
        out_specs=pl.BlockSpec((SUBLANE, tn), lambda j: (0, j)),
        out_shape=jax.ShapeDtypeStruct((SUBLANE, n), f32),
        name="mod",
    )(c8, w_ada, b_ada)


def _split3(x):
    hi = x.astype(bf16).astype(f32)
    r = x - hi
    mid = r.astype(bf16).astype(f32)
    lo = (r - mid).astype(bf16).astype(f32)
    return hi, mid, lo


def _head_norm(z, g, bd):
    z2 = z * z
    hi = z2.astype(bf16)
    lo = (z2 - hi.astype(f32)).astype(bf16)
    ms = jnp.dot(hi, bd, preferred_element_type=f32) + jnp.dot(lo, bd, preferred_element_type=f32)
    return (z * lax.rsqrt(ms + EPS)) * g


def _proj_kernel(x_ref, mod_ref, g1_ref, w_ref, qg_ref, kg_ref, widx_ref,
                 qbd_ref, k_ref, vt_ref, qi_ref, ki_ref, wit_ref, u_ref, ga_ref, gp_ref, *, tm):
    x = x_ref[...]
    shift = mod_ref[0:1, 0:D_MODEL]
    scale = mod_ref[0:1, D_MODEL:2 * D_MODEL]
    h = _rms_modulate(x, g1_ref[...], shift, scale)
    proj = jnp.dot(h.astype(bf16), w_ref[...], preferred_element_type=f32)

    r = lax.broadcasted_iota(i32, (ATTN_WIDTH, ATTN_WIDTH), 0)
    c = lax.broadcasted_iota(i32, (ATTN_WIDTH, ATTN_WIDTH), 1)
    bd = jnp.where((r >> 6) == (c >> 6), 1.0 / HEAD_DIM, 0.0).astype(bf16)

    q = _head_norm(proj[:, C_Q:C_K], qg_ref[...], bd) * (HEAD_DIM ** -0.5 * LOG2E)
    k = _head_norm(proj[:, C_K:C_V], kg_ref[...], bd)
    k_ref[...] = k.astype(bf16)

    low = lax.broadcasted_iota(i32, (TQ, LANE), 1) < HEAD_DIM
    low_tm = lax.broadcasted_iota(i32, (tm, LANE), 1) < HEAD_DIM
    pidx = jnp.dot(h, widx_ref[...], precision=lax.Precision.HIGHEST, preferred_element_type=f32)
    qi = pidx[:, 0:256] * (IDX_DIM ** -0.5)
    up64 = lambda z: pltpu.roll(z, HEAD_DIM, axis=1)
    for g in range(tm // TQ):
        rows = slice(g * TQ, (g + 1) * TQ)
        for p in range(N_PAIR):
            qp = q[rows, p * LANE:(p + 1) * LANE]
            qbd_ref[g, p, 0:TQ, :] = jnp.where(low, qp, 0.0).astype(bf16)
            qbd_ref[g, p, TQ:2 * TQ, :] = jnp.where(low, 0.0, qp).astype(bf16)
        for hh in range(IDX_HEADS):
            seg = qi[rows, (hh // 2) * LANE:(hh // 2 + 1) * LANE]
            own = low if hh % 2 == 0 else jnp.logical_not(low)
            qh, qm, ql = _split3(jnp.where(own, seg, 0.0))
            both = lambda z: z + up64(z)
            in_low = lambda z: jnp.where(low, both(z), 0.0)
            hrows = slice(hh * TQ, (hh + 1) * TQ)
            qi_ref[g, hrows, 0:LANE] = both(qh).astype(bf16)
            qi_ref[g, hrows, LANE:2 * LANE] = both(qm).astype(bf16)
            qi_ref[g, hrows, 2 * LANE:3 * LANE] = in_low(ql).astype(bf16)
            qi_ref[g, hrows, 3 * LANE:4 * LANE] = in_low(qh).astype(bf16)

    v = proj[:, C_V:C_QI]
    vt = v.T.astype(bf16)
    for cc in range(tm // TK):
        for hh in range(N_HEADS):
            vt_ref[cc, hh] = vt[hh * HEAD_DIM:(hh + 1) * HEAD_DIM, cc * TK:(cc + 1) * TK]

    kw = pidx[:, 256:384]
    kh, km, kl = _split3(jnp.where(low_tm, kw, 0.0))
    ki_ref[:, 0:LANE] = (kh + up64(km)).astype(bf16)
    ki_ref[:, LANE:2 * LANE] = kl.astype(bf16)
    wt = (kw * (IDX_HEADS ** -0.5)).T
    wit_ref[...] = wt[HEAD_DIM:HEAD_DIM + SUBLANE, :]
    u_ref[...] = proj[:, C_U:C_GA]
    ga_ref[...] = _sigmoid(proj[:, C_GA:C_GP]).astype(bf16)
    gp_ref[...] = _sigmoid(proj[:, C_GP:C_END]).astype(bf16)


def _proj_call(x2, mod, g1, w_in_p, qg, kg, widx, tm):
    s = x2.shape[0]
    nq = s // TQ
    const = lambda i: (0, 0)
    return pl.pallas_call(
        functools.partial(_proj_kernel, tm=tm),
        grid=(s // tm,),
        in_specs=[pl.BlockSpec((tm, D_MODEL), lambda i: (i, 0)),
                  pl.BlockSpec((SUBLANE, N_MOD * D_MODEL), const),
                  pl.BlockSpec((1, D_MODEL), const),
                  pl.BlockSpec((D_MODEL, C_END), const),
                  pl.BlockSpec((1, ATTN_WIDTH), const),
                  pl.BlockSpec((1, ATTN_WIDTH), const),
                  pl.BlockSpec((D_MODEL, 3 * LANE), const)],
        out_specs=[pl.BlockSpec((tm // TQ, N_PAIR, 2 * TQ, LANE), lambda i: (i, 0, 0, 0)),
                   pl.BlockSpec((tm, ATTN_WIDTH), lambda i: (i, 0)),
                   pl.BlockSpec((tm // TK, N_HEADS, HEAD_DIM, TK), lambda i: (i, 0, 0, 0)),
                   pl.BlockSpec((tm // TQ, IDX_HEADS * TQ, IDX_K), lambda i: (i, 0, 0)),
                   pl.BlockSpec((tm, 2 * LANE), lambda i: (i, 0)),
                   pl.BlockSpec((SUBLANE, tm), lambda i: (0, i)),
                   pl.BlockSpec((tm, POOL_WIDTH), lambda i: (i, 0)),
                   pl.BlockSpec((tm, D_MODEL), lambda i: (i, 0)),
                   pl.BlockSpec((tm, D_MODEL), lambda i: (i, 0))],
        out_shape=[jax.ShapeDtypeStruct((nq, N_PAIR, 2 * TQ, LANE), bf16),
                   jax.ShapeDtypeStruct((s, ATTN_WIDTH), bf16),
                   jax.ShapeDtypeStruct((s // TK, N_HEADS, HEAD_DIM, TK), bf16),
                   jax.ShapeDtypeStruct((nq, IDX_HEADS * TQ, IDX_K), bf16),
                   jax.ShapeDtypeStruct((s, 2 * LANE), bf16),
                   jax.ShapeDtypeStruct((SUBLANE, s), f32),
                   jax.ShapeDtypeStruct((s, POOL_WIDTH), f32),
                   jax.ShapeDtypeStruct((s, D_MODEL), bf16),
                   jax.ShapeDtypeStruct((s, D_MODEL), bf16)],
        compiler_params=pltpu.CompilerParams(dimension_semantics=("parallel",),
                                             vmem_limit_bytes=VMEM_LIMIT),
        name="proj",
    )(x2, mod, g1, w_in_p, qg, kg, widx)


def _slope2(h):
    return (2.0 ** (-8.0 * (h + 1) / N_HEADS)) * LOG2E


def _tree(op, xs):
    xs = list(xs)
    while len(xs) > 1:
        nxt = [op(xs[a], xs[a + 1]) for a in range(0, len(xs) - 1, 2)]
        if len(xs) % 2:
            nxt.append(xs[-1])
        xs = nxt
    return xs[0]


def _row_groups(x):
    return [x[j * SUBLANE:(j + 1) * SUBLANE] for j in range(x.shape[0] // SUBLANE)]


N_CNT_ACC = 4
N_STAGE = 4
PV_ROWS = HEAD_DIM + 16


def _attn_kernel(qbd_ref, qi_ref, wi_ref, k_ref, vt_ref, ki_ref, o_ref,
                 keys_ref, pos_ref, slf_ref, il0_ref, il1_ref, il2_ref, il3_ref,
                 pre0_ref, pre1_ref, pre2_ref, pre3_ref, sm0_ref, sm1_ref, cmax0_ref, cmax1_ref,
                 p0_ref, p1_ref, alpha0_ref, alpha1_ref, m_ref, acc_ref, *, topk):
    il_ref = (il0_ref, il1_ref, il2_ref, il3_ref)
    pre_ref = (pre0_ref, pre1_ref, pre2_ref, pre3_ref)
    sm_ref = (sm0_ref, sm1_ref)
    cmax_ref = (cmax0_ref, cmax1_ref)
    p_ref = (p0_ref, p1_ref)
    alpha_ref = (alpha0_ref, alpha1_ref)
    i = pl.program_id(0)
    nsc = (i + 4) >> 2
    nch = 2 * nsc
    nbody = (i + 8) >> 3

    def chunk_start(c):
        return pl.multiple_of(c * TK, TK)

    def pipeline(produce, consume, carry):
        for j in range(N_STAGE):
            produce(j, j)

        def body(b, carry):
            c0 = N_STAGE * b
            for j in range(N_STAGE):
                carry = consume(c0 + j, j, carry)
                produce(c0 + N_STAGE + j, j)
            return carry

        carry = lax.fori_loop(0, nbody - 1, body, carry)
        c0 = N_STAGE * (nbody - 1)
        for j in range(N_STAGE):
            carry = consume(c0 + j, j, carry)
        return carry

    @pl.when(i == 0)
    def _():
        col = lax.broadcasted_iota(i32, (TK, LANE), 1)
        row = lax.broadcasted_iota(i32, (TK, LANE), 0).astype(f32)
        pos_ref[...] = jnp.where(col < 3, row, 0.0).astype(bf16)
        colq = lax.broadcasted_iota(i32, (2 * TQ, LANE), 1)
        rowq = lax.broadcasted_iota(i32, (2 * TQ, LANE), 0)
        for p in range(N_PAIR):
            sl = jnp.where(rowq < TQ, _slope2(2 * p), _slope2(2 * p + 1)) + jnp.zeros((2 * TQ, LANE), f32)
            hi, mid, lo = _split3(sl)
            slf_ref[p] = jnp.where(colq == 0, hi, jnp.where(colq == 1, mid,
                                   jnp.where(colq == 2, lo, 0.0))).astype(bf16)

    qi = qi_ref[0]
    w = wi_ref[...]
    d0 = (lax.broadcasted_iota(i32, (TK, TQ), 1) - lax.broadcasted_iota(i32, (TK, TQ), 0))

    def idx_matmul(c, slot):
        rows = pl.ds(chunk_start(c), TK)
        kx = ki_ref[rows, 0:LANE]
        ky = ki_ref[rows, LANE:2 * LANE]
        il_ref[slot][...] = _nt_dot(jnp.concatenate([kx, kx, kx, ky], axis=1), qi)

    def idx_keys(c, slot, carry):
        r0 = chunk_start(c)
        sc = jnp.maximum(il_ref[slot][:, 0:TQ], 0.0) * w[0:1, :]
        for hh in range(1, IDX_HEADS):
            sc = sc + jnp.maximum(il_ref[slot][:, hh * TQ:(hh + 1) * TQ], 0.0) * w[hh:hh + 1, :]
        b = lax.bitcast_convert_type(sc, i32)
        key = jnp.where(b < 0, -(b & 0x7FFFFFFF), b)
        valid = d0 >= (r0 - i * TQ)
        keys_ref[pl.ds(r0, TK), :] = jnp.where(valid, key, INT_MIN)
        return carry

    pipeline(idx_matmul, idx_keys, 0)

    def count_ge(cand):
        def body(sc, accs):
            r0 = pl.multiple_of(sc * 2 * TK, 2 * TK)
            ind = jnp.where(keys_ref[pl.ds(r0, 2 * TK), :] >= cand, 1, 0)
            g = _row_groups(ind)
            n = len(g) // N_CNT_ACC
            return tuple(accs[a] + _tree(jnp.add, g[a * n:(a + 1) * n]) for a in range(N_CNT_ACC))
        accs = lax.fori_loop(0, nsc, body,
                             tuple(jnp.zeros((SUBLANE, TQ), i32) for _ in range(N_CNT_ACC)))
        return jnp.sum(_tree(jnp.add, accs), axis=0, keepdims=True)

    def bit_step(bi, t):
        cand = t + lax.shift_left(jnp.int32(1), 31 - bi)
        return jnp.where(count_ge(cand) >= topk, cand, t)

    t = lax.fori_loop(0, 32, bit_step, jnp.full((1, TQ), INT_MIN, i32))
    t = jnp.maximum(t, INT_MIN + 1)
    r_tie = (topk - count_ge(t + 1)).astype(f32)

    tri = jnp.where(lax.broadcasted_iota(i32, (TK, TK), 0) >= lax.broadcasted_iota(i32, (TK, TK), 1),
                    1.0, 0.0).astype(bf16)

    def tie_rank(c, slot):
        e = jnp.where(keys_ref[pl.ds(chunk_start(c), TK), :] == t, 1.0, 0.0).astype(bf16)
        pre_ref[slot][...] = jnp.dot(tri, e, preferred_element_type=f32)

    def mask_out(c, slot, rank):
        r0 = chunk_start(c)
        kk = keys_ref[pl.ds(r0, TK), :]
        pre = pre_ref[slot][...] + rank
        nm = jnp.where(kk > t, 0.0, jnp.where(kk == t, jnp.where(pre <= r_tie, 0.0, NEG_INF), NEG_INF))
        keys_ref[pl.ds(r0, TK), :] = lax.bitcast_convert_type(nm, i32)
        return pre[TK - 1:TK, :]

    pipeline(tie_rank, mask_out, jnp.zeros((1, TQ), f32))

    m_ref[...] = jnp.full(m_ref.shape, NEG_INF, f32)
    acc_ref[...] = jnp.zeros(acc_ref.shape, f32)
    lane2 = lax.broadcasted_iota(i32, (1, 2 * TQ), 1)
    ones_rows = jnp.ones((PV_ROWS - HEAD_DIM, TK), bf16)

    def logits(c, slot):
        rows = pl.ds(chunk_start(c), TK)
        nm = lax.bitcast_convert_type(keys_ref[rows, :], f32)
        nm2 = jnp.concatenate([nm, nm], axis=1)
        for p in range(N_PAIR):
            lhs = jnp.concatenate([k_ref[rows, p * LANE:(p + 1) * LANE], pos_ref[...]], axis=1)
            rhs = jnp.concatenate([qbd_ref[0, p], slf_ref[p]], axis=1)
            sm = _nt_dot(lhs, rhs) + nm2
            sm_ref[slot][p] = sm
            cmax_ref[slot][p] = jnp.max(_tree(jnp.maximum, _row_groups(sm)), axis=0, keepdims=True)

    def probs(c, slot):
        r0f = jnp.asarray(c * TK, dtype=f32)
        for p in range(N_PAIR):
            coff = jnp.where(lane2 < TQ, _slope2(2 * p), _slope2(2 * p + 1)) * r0f
            m_old = m_ref[p]
            m_new = jnp.maximum(m_old, cmax_ref[slot][p] + coff)
            m_safe = jnp.where(m_new == NEG_INF, 0.0, m_new)
            alpha_ref[slot][p] = jnp.where(m_old == NEG_INF, 0.0, jnp.exp2(m_old - m_safe))
            p_ref[slot][p] = jnp.exp2((sm_ref[slot][p] - (m_safe - coff)).astype(bf16))
            m_ref[p] = m_new

    def weighted_sum(c, slot):
        for h in range(N_HEADS):
            lanes = slice((h % 2) * TQ, (h % 2 + 1) * TQ)
            lhs = jnp.concatenate([vt_ref[c, h], ones_rows], axis=0)
            pv = jnp.dot(lhs, p_ref[slot][h // 2, :, lanes], preferred_element_type=f32)
            acc_ref[h] = acc_ref[h] * alpha_ref[slot][h // 2, :, lanes] + pv

    logits(0, 0)
    probs(0, 0)
    logits(1, 1)

    def attn_step(c0):
        probs(c0 + 1, 1)
        logits(c0 + 2, 0)
        logits(c0 + 3, 1)
        weighted_sum(c0, 0)
        probs(c0 + 2, 0)
        weighted_sum(c0 + 1, 1)

    def attn_body2(b, carry):
        attn_step(4 * b)
        attn_step(4 * b + 2)
        return carry

    def attn_body1(b, carry):
        attn_step(2 * (nsc - 2))
        return carry

    lax.fori_loop(0, (nsc - 1) >> 1, attn_body2, 0)
    lax.fori_loop(0, (nsc - 1) & 1, attn_body1, 0)
    weighted_sum(nch - 2, 0)
    probs(nch - 1, 1)
    weighted_sum(nch - 1, 1)

    outs = []
    for h in range(N_HEADS):
        a = acc_ref[h]
        outs.append(a[0:HEAD_DIM] / a[HEAD_DIM:HEAD_DIM + 1])
    o_ref[...] = jnp.concatenate(outs, axis=0).T.astype(bf16)


def _attn_call(qbd, qi, wit, k, vt, ki, topk):
    s = k.shape[0]
    nq = s // TQ
    whole = pl.BlockSpec(memory_space=pltpu.VMEM)
    il = [pltpu.VMEM((TK, IDX_HEADS * TQ), f32)] * N_STAGE
    pre = [pltpu.VMEM((TK, TQ), f32)] * N_STAGE
    return pl.pallas_call(
        functools.partial(_attn_kernel, topk=topk),
        grid=(nq,),
        in_specs=[pl.BlockSpec((1, N_PAIR, 2 * TQ, LANE), lambda i: (i, 0, 0, 0)),
                  pl.BlockSpec((1, IDX_HEADS * TQ, IDX_K), lambda i: (i, 0, 0)),
                  pl.BlockSpec((SUBLANE, TQ), lambda i: (0, i)),
                  whole, whole, whole],
        out_specs=pl.BlockSpec((TQ, ATTN_WIDTH), lambda i: (i, 0)),
        out_shape=jax.ShapeDtypeStruct((s, ATTN_WIDTH), bf16),
        scratch_shapes=[pltpu.VMEM((s, TQ), i32),
                        pltpu.VMEM((TK, LANE), bf16),
                        pltpu.VMEM((N_PAIR, 2 * TQ, LANE), bf16),
                        *il, *pre,
                        pltpu.VMEM((N_PAIR, TK, 2 * TQ), f32),
                        pltpu.VMEM((N_PAIR, TK, 2 * TQ), f32),
                        pltpu.VMEM((N_PAIR, 1, 2 * TQ), f32),
                        pltpu.VMEM((N_PAIR, 1, 2 * TQ), f32),
                        pltpu.VMEM((N_PAIR, TK, 2 * TQ), bf16),
                        pltpu.VMEM((N_PAIR, TK, 2 * TQ), bf16),
                        pltpu.VMEM((N_PAIR, 1, 2 * TQ), f32),
                        pltpu.VMEM((N_PAIR, 1, 2 * TQ), f32),
                        pltpu.VMEM((N_PAIR, 1, 2 * TQ), f32),
                        pltpu.VMEM((N_HEADS, PV_ROWS, TQ), f32)],
        compiler_params=pltpu.CompilerParams(dimension_semantics=("arbitrary",),
                                             vmem_limit_bytes=VMEM_LIMIT),
        name="attn",
    )(qbd, qi, wit, k, vt, ki)


HALO_POOL = 16


def _mix_kernel(x_ref, attn_ref, u_ref, uh_ref, ga_ref, gp_ref, mod_ref,
                wab_ref, wg_ref, ps_ref, wpb_ref, wo_ref, o_ref, *, tm):
    i = pl.program_id(0)
    y_attn = jnp.dot(attn_ref[...], wab_ref[...], preferred_element_type=f32)

    u = u_ref[...]
    halo = jnp.where(i > 0, uh_ref[...], 0.0)
    a = jnp.concatenate([halo, u], axis=0)
    tpos = (i * tm + lax.broadcasted_iota(i32, (tm, POOL_GROUP_DIM), 0) + 1).astype(f32)
    mixed = []
    for g, wdw in enumerate(POOL_WINDOWS):
        ag = a[:, g * POOL_GROUP_DIM:(g + 1) * POOL_GROUP_DIM]
        ug = ag[HALO_POOL:HALO_POOL + tm]
        ssum = ug
        for j in range(1, wdw):
            ssum = ssum + ag[HALO_POOL - j:HALO_POOL - j + tm]
        pooled = ssum / jnp.minimum(tpos, float(wdw)) - ug
        mixed.append(jnp.dot(pooled.astype(bf16), wg_ref[g], preferred_element_type=f32))
    mixed = jnp.concatenate(mixed, axis=1) * ps_ref[...]
    y_pool = jnp.dot(mixed.astype(bf16), wpb_ref[...], preferred_element_type=f32)

    merged = ga_ref[...].astype(f32) * y_attn + gp_ref[...].astype(f32) * y_pool
    o = jnp.dot(merged.astype(bf16), wo_ref[...], preferred_element_type=f32)
    gate = mod_ref[0:1, 2 * D_MODEL:3 * D_MODEL]
    o_ref[...] = x_ref[...] + gate * o


def _mix_call(x2, attn, u, ga, gp, mod, wab, wg, ps, wpb, wo, tm):
    s = x2.shape[0]
    const2 = lambda i: (0, 0)
    hb = tm // HALO_POOL
    return pl.pallas_call(
        functools.partial(_mix_kernel, tm=tm),
        grid=(s // tm,),
        in_specs=[pl.BlockSpec((tm, D_MODEL), lambda i: (i, 0)),
                  pl.BlockSpec((tm, ATTN_WIDTH), lambda i: (i, 0)),
                  pl.BlockSpec((tm, POOL_WIDTH), lambda i: (i, 0)),
                  pl.BlockSpec((HALO_POOL, POOL_WIDTH), lambda i: (jnp.maximum(i * hb - 1, 0), 0)),
                  pl.BlockSpec((tm, D_MODEL), lambda i: (i, 0)),
                  pl.BlockSpec((tm, D_MODEL), lambda i: (i, 0)),
                  pl.BlockSpec((SUBLANE, N_MOD * D_MODEL), const2),
                  pl.BlockSpec((ATTN_WIDTH, D_MODEL), const2),
                  pl.BlockSpec((len(POOL_WINDOWS), POOL_GROUP_DIM, POOL_GROUP_DIM), lambda i: (0, 0, 0)),
                  pl.BlockSpec((1, POOL_WIDTH), const2),
                  pl.BlockSpec((POOL_WIDTH, D_MODEL), const2),
                  pl.BlockSpec((D_MODEL, D_MODEL), const2)],
        out_specs=pl.BlockSpec((tm, D_MODEL), lambda i: (i, 0)),
        out_shape=jax.ShapeDtypeStruct((s, D_MODEL), f32),
        compiler_params=pltpu.CompilerParams(dimension_semantics=("parallel",),
                                             vmem_limit_bytes=VMEM_LIMIT),
        name="mix",
    )(x2, attn, u, u, ga, gp, mod, wab, wg, ps, wpb, wo)


HALO_CONV = 8


def _ffn_kernel(x_ref, xh_ref, mod_ref, g2_ref, wup_ref, cw_ref, cb_ref, wdn_ref, o_ref, *, tm):
    i = pl.program_id(0)
    shift = mod_ref[0:1, 3 * D_MODEL:4 * D_MODEL]
    scale = mod_ref[0:1, 4 * D_MODEL:5 * D_MODEL]
    gate = mod_ref[0:1, 5 * D_MODEL:6 * D_MODEL]
    g2 = g2_ref[...]
    x = x_ref[...]
    h = _rms_modulate(x, g2, shift, scale)
    hh = jnp.where(i > 0, _rms_modulate(xh_ref[...], g2, shift, scale), 0.0)
    ha = jnp.concatenate([hh, h], axis=0).astype(bf16)
    up = jnp.dot(ha, wup_ref[...], preferred_element_type=f32)
    cw = cw_ref[...]
    y = cb_ref[...] + cw[0:1, :] * up[HALO_CONV - 2:HALO_CONV - 2 + tm]
    y = y + cw[1:2, :] * up[HALO_CONV - 1:HALO_CONV - 1 + tm]
    y = y + cw[2:3, :] * up[HALO_CONV:HALO_CONV + tm]
    a = y[:, 0:D_FF]
    b = y[:, D_FF:2 * D_FF]
    gated = (a * _sigmoid(a)) * b
    o = jnp.dot(gated.astype(bf16), wdn_ref[...], preferred_element_type=f32)
    o_ref[...] = x + gate * o


def _ffn_call(x1, mod, g2, wup, cw, cb, wdn, tm):
    s = x1.shape[0]
    const2 = lambda i: (0, 0)
    hb = tm // HALO_CONV
    return pl.pallas_call(
        functools.partial(_ffn_kernel, tm=tm),
        grid=(s // tm,),
        in_specs=[pl.BlockSpec((tm, D_MODEL), lambda i: (i, 0)),
                  pl.BlockSpec((HALO_CONV, D_MODEL), lambda i: (jnp.maximum(i * hb - 1, 0), 0)),
                  pl.BlockSpec((SUBLANE, N_MOD * D_MODEL), const2),
                  pl.BlockSpec((1, D_MODEL), const2),
                  pl.BlockSpec((D_MODEL, 2 * D_FF), const2),
                  pl.BlockSpec((3, 2 * D_FF), const2),
                  pl.BlockSpec((1, 2 * D_FF), const2),
                  pl.BlockSpec((D_FF, D_MODEL), const2)],
        out_specs=pl.BlockSpec((tm, D_MODEL), lambda i: (i, 0)),
        out_shape=jax.ShapeDtypeStruct((s, D_MODEL), f32),
        compiler_params=pltpu.CompilerParams(dimension_semantics=("parallel",),
                                             vmem_limit_bytes=VMEM_LIMIT),
        name="ffn",
    )(x1, x1, mod, g2, wup, cw, cb, wdn)


def _pack_w_in(w):
    q, k, v, qi, ki, wi, u, ga, gp = (w[:, 0:512], w[:, 512:1024], w[:, 1024:1536], w[:, 1536:1792],
                                      w[:, 1792:1856], w[:, 1856:1860], w[:, 1860:2372],
                                      w[:, 2372:3396], w[:, 3396:4420])
    wi_pad = jnp.pad(wi, ((0, 0), (0, LANE - IDX_HEADS)))
    return jnp.concatenate([q, k, v, qi, ki, ki, wi_pad, u, ga, gp], axis=1).astype(bf16)


def _pack_w_idx(w):
    return jnp.pad(w[:, 1536:1860], ((0, 0), (0, 3 * LANE - (1860 - 1536))))


def kernel(x, c, w_ada, b_ada, norm1_g, w_in, q_norm_g, k_norm_g, w_attn_br, w_pool_grp,
           pool_scale, w_pool_br, w_out, norm2_g, w_up, conv_w, conv_b, w_down):
    bsz, s, d = x.shape
    assert bsz == 1 and d == D_MODEL and s % (N_STAGE * TK) == 0
    depth = w_ada.shape[0]
    topk = min(TOPK_MAX, s // 4)
    x2 = x.reshape(s, d)
    c8 = jnp.pad(c, ((0, SUBLANE - bsz), (0, 0)))
    for l in range(depth):
        mod = _mod_call(c8, w_ada[l], b_ada[l].reshape(1, -1))
        qbd, k, vt, qi, ki, wit, u, ga, gp = _proj_call(
            x2, mod, norm1_g[l].reshape(1, -1), _pack_w_in(w_in[l]),
            jnp.tile(q_norm_g[l], N_HEADS).reshape(1, -1),
            jnp.tile(k_norm_g[l], N_HEADS).reshape(1, -1), _pack_w_idx(w_in[l]), tm=512)
        attn = _attn_call(qbd, qi, wit, k, vt, ki, topk)
        x2 = _mix_call(x2, attn, u, ga, gp, mod, w_attn_br[l].astype(bf16),
                       w_pool_grp[l].astype(bf16), pool_scale[l].reshape(1, -1),
                       w_pool_br[l].astype(bf16), w_out[l].astype(bf16), tm=512)
        x2 = _ffn_call(x2, mod, norm2_g[l].reshape(1, -1), w_up[l].astype(bf16), conv_w[l],
                       conv_b[l].reshape(1, -1), w_down[l].astype(bf16), tm=256)
    return x2.reshape(bsz, s, d)
```

```python
import functools
import math

import jax
import jax.numpy as jnp
from jax import lax
from jax.experimental import pallas as pl
from jax.experimental.pallas import tpu as pltpu

f32 = jnp.float32
bf16 = jnp.bfloat16
i32 = jnp.int32

D_MODEL = 1024
N_HEADS = 8
HEAD_DIM = 64
ATTN_WIDTH = N_HEADS * HEAD_DIM
IDX_HEADS = 4
IDX_DIM = 64
TOPK_MAX = 256
POOL_WINDOWS = (2, 4, 8, 16)
POOL_GROUP_DIM = 128
POOL_WIDTH = 512
D_FF = 2816
EPS = 1e-6
N_MOD = 6

LANE = 128
SUBLANE = 8
VMEM_LIMIT = 58 * 1024 * 1024

LOG2E = 1.4426950408889634
INT_MIN = -2147483648
NEG_INF = float("-inf")

C_Q, C_K, C_V, C_QI, C_KI, C_WI, C_U, C_GA, C_GP, C_END = (
    0, 512, 1024, 1536, 1792, 1920, 2048, 2560, 3584, 4608)

TQ = 128
TK = 256
N_PAIR = N_HEADS // 2
IDX_K = 4 * LANE


def _sigmoid(x):
    return 1.0 / (1.0 + jnp.exp(-x))


def _rms_modulate(x, g, shift, scale):
    y = x * lax.rsqrt(jnp.mean(x * x, axis=-1, keepdims=True) + EPS)
    return (y * g) * (1.0 + scale) + shift


def _mod_kernel(c_ref, w_ref, b_ref, o_ref):
    c = c_ref[...]
    sc = c * _sigmoid(c)
    o_ref[...] = jnp.dot(sc, w_ref[...], precision=lax.Precision.HIGHEST,
                         preferred_element_type=f32) + b_ref[...]


def _mod_call(c8, w_ada, b_ada):
    n = w_ada.shape[1]
    tn = 1024
    return pl.pallas_call(
        _mod_kernel,
        grid=(n // tn,),
        in_specs=[pl.BlockSpec((SUBLANE, D_MODEL), lambda j: (0, 0)),
                  pl.BlockSpec((D_MODEL, tn), lambda j: (0, j)),
                  pl.BlockSpec((1, tn), lambda j: (0, j))],
        out_specs=pl.BlockSpec((SUBLANE, tn), lambda j: (0, j)),
        out_shape=jax.ShapeDtypeStruct((SUBLANE, n), f32),
        name="mod",
    )(c8, w_ada, b_ada)


def _split3(x):
    hi = x.astype(bf16).astype(f32)
    r = x - hi
    mid = r.astype(bf16).astype(f32)
    lo = (r - mid).astype(bf16).astype(f32)
    return hi, mid, lo


def _head_norm(z, g, bd):
    z2 = z * z
    hi = z2.astype(bf16)
    lo = (z2 - hi.astype(f32)).astype(bf16)
    ms = jnp.dot(hi, bd, preferred_element_type=f32) + jnp.dot(lo, bd, preferred_element_type=f32)
    return (z * lax.rsqrt(ms + EPS)) * g


def _proj_kernel(x_ref, mod_ref, g1_ref, w_ref, qg_ref, kg_ref, widx_ref,
                 qbd_ref, k_ref, vt_ref, qi_ref, ki_ref, wit_ref, u_ref, ga_ref, gp_ref, *, tm):
    x = x_ref[...]
    shift = mod_ref[0:1, 0:D_MODEL]
    scale = mod_ref[0:1, D_MODEL:2 * D_MODEL]
    h = _rms_modulate(x, g1_ref[...], shift, scale)
    proj = jnp.dot(h.astype(bf16), w_ref[...], preferred_element_type=f32)

    r = lax.broadcasted_iota(i32, (ATTN_WIDTH, ATTN_WIDTH), 0)
    c = lax.broadcasted_iota(i32, (ATTN_WIDTH, ATTN_WIDTH), 1)
    bd = jnp.where((r >> 6) == (c >> 6), 1.0 / HEAD_DIM, 0.0).astype(bf16)

    q = _head_norm(proj[:, C_Q:C_K], qg_ref[...], bd) * (HEAD_DIM ** -0.5 * LOG2E)
    k = _head_norm(proj[:, C_K:C_V], kg_ref[...], bd)
    k_ref[...] = k.astype(bf16)

    low = lax.broadcasted_iota(i32, (TQ, LANE), 1) < HEAD_DIM
    low_tm = lax.broadcasted_iota(i32, (tm, LANE), 1) < HEAD_DIM
    pidx = jnp.dot(h, widx_ref[...], precision=lax.Precision.HIGHEST, preferred_element_type=f32)
    qi = pidx[:, 0:256] * (IDX_DIM ** -0.5)
    up64 = lambda z: pltpu.roll(z, HEAD_DIM, axis=1)
    for g in range(tm // TQ):
        rows = slice(g * TQ, (g + 1) * TQ)
        for p in range(N_PAIR):
            qp = q[rows, p * LANE:(p + 1) * LANE]
            bd_q = jnp.concatenate([jnp.where(low, qp, 0.0), jnp.where(low, 0.0, qp)], axis=0)
            qbd_ref[g, p] = bd_q.T.astype(bf16)
        for hh in range(IDX_HEADS):
            seg = qi[rows, (hh // 2) * LANE:(hh // 2 + 1) * LANE]
            own = low if hh % 2 == 0 else jnp.logical_not(low)
            qh, qm, ql = _split3(jnp.where(own, seg, 0.0))
            both = lambda z: z + up64(z)
            in_low = lambda z: jnp.where(low, both(z), 0.0)
            hcols = slice(hh * TQ, (hh + 1) * TQ)
            for kt, term in enumerate((both(qh), both(qm), in_low(ql), in_low(qh))):
                qi_ref[g, kt * LANE:(kt + 1) * LANE, hcols] = term.T.astype(bf16)

    v = proj[:, C_V:C_QI]
    vt = v.T.astype(bf16)
    for cc in range(tm // TK):
        for hh in range(N_HEADS):
            vt_ref[cc, hh] = vt[hh * HEAD_DIM:(hh + 1) * HEAD_DIM, cc * TK:(cc + 1) * TK]

    kw = pidx[:, 256:384]
    kh, km, kl = _split3(jnp.where(low_tm, kw, 0.0))
    ki_ref[:, 0:LANE] = (kh + up64(km)).astype(bf16)
    ki_ref[:, LANE:2 * LANE] = kl.astype(bf16)
    wt = (kw * (IDX_HEADS ** -0.5)).T
    wit_ref[...] = wt[HEAD_DIM:HEAD_DIM + SUBLANE, :]
    u_ref[...] = proj[:, C_U:C_GA]
    ga_ref[...] = _sigmoid(proj[:, C_GA:C_GP]).astype(bf16)
    gp_ref[...] = _sigmoid(proj[:, C_GP:C_END]).astype(bf16)


def _proj_call(x2, mod, g1, w_in_p, qg, kg, widx, tm):
    s = x2.shape[0]
    nq = s // TQ
    const = lambda i: (0, 0)
    return pl.pallas_call(
        functools.partial(_proj_kernel, tm=tm),
        grid=(s // tm,),
        in_specs=[pl.BlockSpec((tm, D_MODEL), lambda i: (i, 0)),
                  pl.BlockSpec((SUBLANE, N_MOD * D_MODEL), const),
                  pl.BlockSpec((1, D_MODEL), const),
                  pl.BlockSpec((D_MODEL, C_END), const),
                  pl.BlockSpec((1, ATTN_WIDTH), const),
                  pl.BlockSpec((1, ATTN_WIDTH), const),
                  pl.BlockSpec((D_MODEL, 3 * LANE), const)],
        out_specs=[pl.BlockSpec((tm // TQ, N_PAIR, LANE, 2 * TQ), lambda i: (i, 0, 0, 0)),
                   pl.BlockSpec((tm, ATTN_WIDTH), lambda i: (i, 0)),
                   pl.BlockSpec((tm // TK, N_HEADS, HEAD_DIM, TK), lambda i: (i, 0, 0, 0)),
                   pl.BlockSpec((tm // TQ, IDX_K, IDX_HEADS * TQ), lambda i: (i, 0, 0)),
                   pl.BlockSpec((tm, 2 * LANE), lambda i: (i, 0)),
                   pl.BlockSpec((SUBLANE, tm), lambda i: (0, i)),
                   pl.BlockSpec((tm, POOL_WIDTH), lambda i: (i, 0)),
                   pl.BlockSpec((tm, D_MODEL), lambda i: (i, 0)),
                   pl.BlockSpec((tm, D_MODEL), lambda i: (i, 0))],
        out_shape=[jax.ShapeDtypeStruct((nq, N_PAIR, LANE, 2 * TQ), bf16),
                   jax.ShapeDtypeStruct((s, ATTN_WIDTH), bf16),
                   jax.ShapeDtypeStruct((s // TK, N_HEADS, HEAD_DIM, TK), bf16),
                   jax.ShapeDtypeStruct((nq, IDX_K, IDX_HEADS * TQ), bf16),
                   jax.ShapeDtypeStruct((s, 2 * LANE), bf16),
                   jax.ShapeDtypeStruct((SUBLANE, s), f32),
                   jax.ShapeDtypeStruct((s, POOL_WIDTH), f32),
                   jax.ShapeDtypeStruct((s, D_MODEL), bf16),
                   jax.ShapeDtypeStruct((s, D_MODEL), bf16)],
        compiler_params=pltpu.CompilerParams(dimension_semantics=("parallel",),
                                             vmem_limit_bytes=VMEM_LIMIT),
        name="proj",
    )(x2, mod, g1, w_in_p, qg, kg, widx)


def _slope2(h):
    return (2.0 ** (-8.0 * (h + 1) / N_HEADS)) * LOG2E


def _tree(op, xs):
    xs = list(xs)
    while len(xs) > 1:
        nxt = [op(xs[a], xs[a + 1]) for a in range(0, len(xs) - 1, 2)]
        if len(xs) % 2:
            nxt.append(xs[-1])
        xs = nxt
    return xs[0]


def _row_groups(x):
    return [x[j * SUBLANE:(j + 1) * SUBLANE] for j in range(x.shape[0] // SUBLANE)]


N_CNT_ACC = 8
N_STAGE = 4
PV_ROWS = HEAD_DIM + 16


def _attn_kernel(qbd_ref, qi_ref, wi_ref, k_ref, vt_ref, ki_ref, o_ref,
                 keys_ref, pos_ref, slf_ref, il0_ref, il1_ref, il2_ref, il3_ref,
                 pre0_ref, pre1_ref, pre2_ref, pre3_ref, sm0_ref, sm1_ref, cmax0_ref, cmax1_ref,
                 p0_ref, p1_ref, alpha0_ref, alpha1_ref, m_ref, acc_ref, *, topk):
    il_ref = (il0_ref, il1_ref, il2_ref, il3_ref)
    pre_ref = (pre0_ref, pre1_ref, pre2_ref, pre3_ref)
    sm_ref = (sm0_ref, sm1_ref)
    cmax_ref = (cmax0_ref, cmax1_ref)
    p_ref = (p0_ref, p1_ref)
    alpha_ref = (alpha0_ref, alpha1_ref)
    i = pl.program_id(0)
    nsc = (i + 4) >> 2
    nch = 2 * nsc
    nbody = (i + 8) >> 3

    def chunk_start(c):
        return pl.multiple_of(c * TK, TK)

    def pipeline(produce, consume, carry):
        for j in range(N_STAGE):
            produce(j, j)

        def body(b, carry):
            c0 = N_STAGE * b
            for j in range(N_STAGE):
                carry = consume(c0 + j, j, carry)
                produce(c0 + N_STAGE + j, j)
            return carry

        carry = lax.fori_loop(0, nbody - 1, body, carry)
        c0 = N_STAGE * (nbody - 1)
        for j in range(N_STAGE):
            carry = consume(c0 + j, j, carry)
        return carry

    @pl.when(i == 0)
    def _():
        col = lax.broadcasted_iota(i32, (TK, LANE), 1)
        row = lax.broadcasted_iota(i32, (TK, LANE), 0).astype(f32)
        pos_ref[...] = jnp.where(col < 3, row, 0.0).astype(bf16)
        term = lax.broadcasted_iota(i32, (LANE, 2 * TQ), 0)
        lane = lax.broadcasted_iota(i32, (LANE, 2 * TQ), 1)
        for p in range(N_PAIR):
            sl = jnp.where(lane < TQ, _slope2(2 * p), _slope2(2 * p + 1)) + jnp.zeros((LANE, 2 * TQ), f32)
            hi, mid, lo = _split3(sl)
            slf_ref[p] = jnp.where(term == 0, hi, jnp.where(term == 1, mid,
                                   jnp.where(term == 2, lo, 0.0))).astype(bf16)

    qi = qi_ref[0]
    w = wi_ref[...]
    d0 = (lax.broadcasted_iota(i32, (TK, TQ), 1) - lax.broadcasted_iota(i32, (TK, TQ), 0))

    def idx_matmul(c, slot):
        rows = pl.ds(chunk_start(c), TK)
        kx = ki_ref[rows, 0:LANE]
        ky = ki_ref[rows, LANE:2 * LANE]
        il_ref[slot][...] = jnp.dot(jnp.concatenate([kx, kx, kx, ky], axis=1), qi,
                                    preferred_element_type=f32)

    def idx_keys(c, slot, carry):
        r0 = chunk_start(c)
        sc = jnp.maximum(il_ref[slot][:, 0:TQ], 0.0) * w[0:1, :]
        for hh in range(1, IDX_HEADS):
            sc = sc + jnp.maximum(il_ref[slot][:, hh * TQ:(hh + 1) * TQ], 0.0) * w[hh:hh + 1, :]
        b = lax.bitcast_convert_type(sc, i32)
        key = jnp.where(b < 0, -(b & 0x7FFFFFFF), b)
        valid = d0 >= (r0 - i * TQ)
        keys_ref[pl.ds(r0, TK), :] = jnp.where(valid, key, INT_MIN)
        return carry

    pipeline(idx_matmul, idx_keys, 0)

    def count_ge(cand):
        def body(b, accs):
            accs = list(accs)
            for part in range(N_STAGE):
                r0 = pl.multiple_of((b * N_STAGE + part) * TK, TK)
                ind = jnp.where(keys_ref[pl.ds(r0, TK), :] >= cand, 1, 0)
                for j, g in enumerate(_row_groups(ind)):
                    accs[j % N_CNT_ACC] = accs[j % N_CNT_ACC] + g
            return tuple(accs)
        accs = lax.fori_loop(0, nbody, body,
                             tuple(jnp.zeros((SUBLANE, TQ), i32) for _ in range(N_CNT_ACC)))
        return jnp.sum(_tree(jnp.add, accs), axis=0, keepdims=True)

    def bit_step(bi, t):
        cand = t + lax.shift_left(jnp.int32(1), 31 - bi)
        return jnp.where(count_ge(cand) >= topk, cand, t)

    t = lax.fori_loop(0, 32, bit_step, jnp.full((1, TQ), INT_MIN, i32))
    t = jnp.maximum(t, INT_MIN + 1)
    r_tie = (topk - count_ge(t + 1)).astype(f32)

    tri = jnp.where(lax.broadcasted_iota(i32, (TK, TK), 0) >= lax.broadcasted_iota(i32, (TK, TK), 1),
                    1.0, 0.0).astype(bf16)

    def tie_rank(c, slot):
        e = jnp.where(keys_ref[pl.ds(chunk_start(c), TK), :] == t, 1.0, 0.0).astype(bf16)
        pre_ref[slot][...] = jnp.dot(tri, e, preferred_element_type=f32)

    def mask_out(c, slot, rank):
        r0 = chunk_start(c)
        kk = keys_ref[pl.ds(r0, TK), :]
        pre = pre_ref[slot][...] + rank
        nm = jnp.where(kk > t, 0.0, jnp.where(kk == t, jnp.where(pre <= r_tie, 0.0, NEG_INF), NEG_INF))
        keys_ref[pl.ds(r0, TK), :] = lax.bitcast_convert_type(nm, i32)
        return pre[TK - 1:TK, :]

    pipeline(tie_rank, mask_out, jnp.zeros((1, TQ), f32))

    m_ref[...] = jnp.full(m_ref.shape, NEG_INF, f32)
    acc_ref[...] = jnp.zeros(acc_ref.shape, f32)
    lane2 = lax.broadcasted_iota(i32, (1, 2 * TQ), 1)
    ones_rows = jnp.ones((PV_ROWS - HEAD_DIM, TK), bf16)

    def logits(c, slot):
        rows = pl.ds(chunk_start(c), TK)
        nm = lax.bitcast_convert_type(keys_ref[rows, :], f32)
        nm2 = jnp.concatenate([nm, nm], axis=1)
        for p in range(N_PAIR):
            lhs = jnp.concatenate([k_ref[rows, p * LANE:(p + 1) * LANE], pos_ref[...]], axis=1)
            rhs = jnp.concatenate([qbd_ref[0, p], slf_ref[p]], axis=0)
            sm = jnp.dot(lhs, rhs, preferred_element_type=f32) + nm2
            sm_ref[slot][p] = sm
            cmax_ref[slot][p] = jnp.max(_tree(jnp.maximum, _row_groups(sm)), axis=0, keepdims=True)

    def probs(c, slot):
        r0f = jnp.asarray(c * TK, dtype=f32)
        for p in range(N_PAIR):
            coff = jnp.where(lane2 < TQ, _slope2(2 * p), _slope2(2 * p + 1)) * r0f
            m_old = m_ref[p]
            m_new = jnp.maximum(m_old, cmax_ref[slot][p] + coff)
            m_safe = jnp.where(m_new == NEG_INF, 0.0, m_new)
            alpha_ref[slot][p] = jnp.where(m_old == NEG_INF, 0.0, jnp.exp2(m_old - m_safe))
            p_ref[slot][p] = jnp.exp2((sm_ref[slot][p] - (m_safe - coff)).astype(bf16))
            m_ref[p] = m_new

    def weighted_sum(c, slot):
        for h in range(N_HEADS):
            lanes = slice((h % 2) * TQ, (h % 2 + 1) * TQ)
            lhs = jnp.concatenate([vt_ref[c, h], ones_rows], axis=0)
            pv = jnp.dot(lhs, p_ref[slot][h // 2, :, lanes], preferred_element_type=f32)
            acc_ref[h] = acc_ref[h] * alpha_ref[slot][h // 2, :, lanes] + pv

    logits(0, 0)
    probs(0, 0)
    logits(1, 1)

    def attn_step(c0):
        probs(c0 + 1, 1)
        logits(c0 + 2, 0)
        logits(c0 + 3, 1)
        weighted_sum(c0, 0)
        probs(c0 + 2, 0)
        weighted_sum(c0 + 1, 1)

    def attn_body2(b, carry):
        attn_step(4 * b)
        attn_step(4 * b + 2)
        return carry

    def attn_body1(b, carry):
        attn_step(2 * (nsc - 2))
        return carry

    lax.fori_loop(0, (nsc - 1) >> 1, attn_body2, 0)
    lax.fori_loop(0, (nsc - 1) & 1, attn_body1, 0)
    weighted_sum(nch - 2, 0)
    probs(nch - 1, 1)
    weighted_sum(nch - 1, 1)

    outs = []
    for h in range(N_HEADS):
        a = acc_ref[h]
        outs.append(a[0:HEAD_DIM] / a[HEAD_DIM:HEAD_DIM + 1])
    o_ref[...] = jnp.concatenate(outs, axis=0).T.astype(bf16)


def _attn_call(qbd, qi, wit, k, vt, ki, topk):
    s = k.shape[0]
    nq = s // TQ
    whole = pl.BlockSpec(memory_space=pltpu.VMEM)
    il = [pltpu.VMEM((TK, IDX_HEADS * TQ), f32)] * N_STAGE
    pre = [pltpu.VMEM((TK, TQ), f32)] * N_STAGE
    return pl.pallas_call(
        functools.partial(_attn_kernel, topk=topk),
        grid=(nq,),
        in_specs=[pl.BlockSpec((1, N_PAIR, LANE, 2 * TQ), lambda i: (i, 0, 0, 0)),
                  pl.BlockSpec((1, IDX_K, IDX_HEADS * TQ), lambda i: (i, 0, 0)),
                  pl.BlockSpec((SUBLANE, TQ), lambda i: (0, i)),
                  whole, whole, whole],
        out_specs=pl.BlockSpec((TQ, ATTN_WIDTH), lambda i: (i, 0)),
        out_shape=jax.ShapeDtypeStruct((s, ATTN_WIDTH), bf16),
        scratch_shapes=[pltpu.VMEM((s, TQ), i32),
                        pltpu.VMEM((TK, LANE), bf16),
                        pltpu.VMEM((N_PAIR, LANE, 2 * TQ), bf16),
                        *il, *pre,
                        pltpu.VMEM((N_PAIR, TK, 2 * TQ), f32),
                        pltpu.VMEM((N_PAIR, TK, 2 * TQ), f32),
                        pltpu.VMEM((N_PAIR, 1, 2 * TQ), f32),
                        pltpu.VMEM((N_PAIR, 1, 2 * TQ), f32),
                        pltpu.VMEM((N_PAIR, TK, 2 * TQ), bf16),
                        pltpu.VMEM((N_PAIR, TK, 2 * TQ), bf16),
                        pltpu.VMEM((N_PAIR, 1, 2 * TQ), f32),
                        pltpu.VMEM((N_PAIR, 1, 2 * TQ), f32),
                        pltpu.VMEM((N_PAIR, 1, 2 * TQ), f32),
                        pltpu.VMEM((N_HEADS, PV_ROWS, TQ), f32)],
        compiler_params=pltpu.CompilerParams(dimension_semantics=("arbitrary",),
                                             vmem_limit_bytes=VMEM_LIMIT),
        name="attn",
    )(qbd, qi, wit, k, vt, ki)


HALO_POOL = 16


def _mix_kernel(x_ref, attn_ref, u_ref, uh_ref, ga_ref, gp_ref, mod_ref,
                wab_ref, wg_ref, ps_ref, wpb_ref, wo_ref, o_ref, *, tm):
    i = pl.program_id(0)
    y_attn = jnp.dot(attn_ref[...], wab_ref[...], preferred_element_type=f32)

    u = u_ref[...]
    halo = jnp.where(i > 0, uh_ref[...], 0.0)
    a = jnp.concatenate([halo, u], axis=0)
    tpos = (i * tm + lax.broadcasted_iota(i32, (tm, POOL_GROUP_DIM), 0) + 1).astype(f32)
    mixed = []
    for g, wdw in enumerate(POOL_WINDOWS):
        ag = a[:, g * POOL_GROUP_DIM:(g + 1) * POOL_GROUP_DIM]
        ug = ag[HALO_POOL:HALO_POOL + tm]
        ssum = ug
        for j in range(1, wdw):
            ssum = ssum + ag[HALO_POOL - j:HALO_POOL - j + tm]
        pooled = ssum / jnp.minimum(tpos, float(wdw)) - ug
        mixed.append(jnp.dot(pooled.astype(bf16), wg_ref[g], preferred_element_type=f32))
    mixed = jnp.concatenate(mixed, axis=1) * ps_ref[...]
    y_pool = jnp.dot(mixed.astype(bf16), wpb_ref[...], preferred_element_type=f32)

    merged = ga_ref[...].astype(f32) * y_attn + gp_ref[...].astype(f32) * y_pool
    o = jnp.dot(merged.astype(bf16), wo_ref[...], preferred_element_type=f32)
    gate = mod_ref[0:1, 2 * D_MODEL:3 * D_MODEL]
    o_ref[...] = x_ref[...] + gate * o


def _mix_call(x2, attn, u, ga, gp, mod, wab, wg, ps, wpb, wo, tm):
    s = x2.shape[0]
    const2 = lambda i: (0, 0)
    hb = tm // HALO_POOL
    return pl.pallas_call(
        functools.partial(_mix_kernel, tm=tm),
        grid=(s // tm,),
        in_specs=[pl.BlockSpec((tm, D_MODEL), lambda i: (i, 0)),
                  pl.BlockSpec((tm, ATTN_WIDTH), lambda i: (i, 0)),
                  pl.BlockSpec((tm, POOL_WIDTH), lambda i: (i, 0)),
                  pl.BlockSpec((HALO_POOL, POOL_WIDTH), lambda i: (jnp.maximum(i * hb - 1, 0), 0)),
                  pl.BlockSpec((tm, D_MODEL), lambda i: (i, 0)),
                  pl.BlockSpec((tm, D_MODEL), lambda i: (i, 0)),
                  pl.BlockSpec((SUBLANE, N_MOD * D_MODEL), const2),
                  pl.BlockSpec((ATTN_WIDTH, D_MODEL), const2),
                  pl.BlockSpec((len(POOL_WINDOWS), POOL_GROUP_DIM, POOL_GROUP_DIM), lambda i: (0, 0, 0)),
                  pl.BlockSpec((1, POOL_WIDTH), const2),
                  pl.BlockSpec((POOL_WIDTH, D_MODEL), const2),
                  pl.BlockSpec((D_MODEL, D_MODEL), const2)],
        out_specs=pl.BlockSpec((tm, D_MODEL), lambda i: (i, 0)),
        out_shape=jax.ShapeDtypeStruct((s, D_MODEL), f32),
        compiler_params=pltpu.CompilerParams(dimension_semantics=("parallel",),
                                             vmem_limit_bytes=VMEM_LIMIT),
        name="mix",
    )(x2, attn, u, u, ga, gp, mod, wab, wg, ps, wpb, wo)


HALO_CONV = 8


def _ffn_kernel(x_ref, xh_ref, mod_ref, g2_ref, wup_ref, cw_ref, cb_ref, wdn_ref, o_ref, *, tm):
    i = pl.program_id(0)
    shift = mod_ref[0:1, 3 * D_MODEL:4 * D_MODEL]
    scale = mod_ref[0:1, 4 * D_MODEL:5 * D_MODEL]
    gate = mod_ref[0:1, 5 * D_MODEL:6 * D_MODEL]
    g2 = g2_ref[...]
    x = x_ref[...]
    h = _rms_modulate(x, g2, shift, scale)
    hh = jnp.where(i > 0, _rms_modulate(xh_ref[...], g2, shift, scale), 0.0)
    ha = jnp.concatenate([hh, h], axis=0).astype(bf16)
    up = jnp.dot(ha, wup_ref[...], preferred_element_type=f32)
    cw = cw_ref[...]
    y = cb_ref[...] + cw[0:1, :] * up[HALO_CONV - 2:HALO_CONV - 2 + tm]
    y = y + cw[1:2, :] * up[HALO_CONV - 1:HALO_CONV - 1 + tm]
    y = y + cw[2:3, :] * up[HALO_CONV:HALO_CONV + tm]
    a = y[:, 0:D_FF]
    b = y[:, D_FF:2 * D_FF]
    gated = (a * _sigmoid(a)) * b
    o = jnp.dot(gated.astype(bf16), wdn_ref[...], preferred_element_type=f32)
    o_ref[...] = x + gate * o


def _ffn_call(x1, mod, g2, wup, cw, cb, wdn, tm):
    s = x1.shape[0]
    const2 = lambda i: (0, 0)
    hb = tm // HALO_CONV
    return pl.pallas_call(
        functools.partial(_ffn_kernel, tm=tm),
        grid=(s // tm,),
        in_specs=[pl.BlockSpec((tm, D_MODEL), lambda i: (i, 0)),
                  pl.BlockSpec((HALO_CONV, D_MODEL), lambda i: (jnp.maximum(i * hb - 1, 0), 0)),
                  pl.BlockSpec((SUBLANE, N_MOD * D_MODEL), const2),
                  pl.BlockSpec((1, D_MODEL), const2),
                  pl.BlockSpec((D_MODEL, 2 * D_FF), const2),
                  pl.BlockSpec((3, 2 * D_FF), const2),
                  pl.BlockSpec((1, 2 * D_FF), const2),
                  pl.BlockSpec((D_FF, D_MODEL), const2)],
        out_specs=pl.BlockSpec((tm, D_MODEL), lambda i: (i, 0)),
        out_shape=jax.ShapeDtypeStruct((s, D_MODEL), f32),
        compiler_params=pltpu.CompilerParams(dimension_semantics=("parallel",),
                                             vmem_limit_bytes=VMEM_LIMIT),
        name="ffn",
    )(x1, x1, mod, g2, wup, cw, cb, wdn)


def _pack_w_in(w):
    q, k, v, qi, ki, wi, u, ga, gp = (w[:, 0:512], w[:, 512:1024], w[:, 1024:1536], w[:, 1536:1792],
                                      w[:, 1792:1856], w[:, 1856:1860], w[:, 1860:2372],
                                      w[:, 2372:3396], w[:, 3396:4420])
    wi_pad = jnp.pad(wi, ((0, 0), (0, LANE - IDX_HEADS)))
    return jnp.concatenate([q, k, v, qi, ki, ki, wi_pad, u, ga, gp], axis=1).astype(bf16)


def _pack_w_idx(w):
    return jnp.pad(w[:, 1536:1860], ((0, 0), (0, 3 * LANE - (1860 - 1536))))


def kernel(x, c, w_ada, b_ada, norm1_g, w_in, q_norm_g, k_norm_g, w_attn_br, w_pool_grp,
           pool_scale, w_pool_br, w_out, norm2_g, w_up, conv_w, conv_b, w_down):
    bsz, s, d = x.shape
    assert bsz == 1 and d == D_MODEL and s % (N_STAGE * TK) == 0
    depth = w_ada.shape[0]
    topk = min(TOPK_MAX, s // 4)
    x2 = x.reshape(s, d)
    c8 = jnp.pad(c, ((0, SUBLANE - bsz), (0, 0)))
    for l in range(depth):
        mod = _mod_call(c8, w_ada[l], b_ada[l].reshape(1, -1))
        qbd, k, vt, qi, ki, wit, u, ga, gp = _proj_call(
            x2, mod, norm1_g[l].reshape(1, -1), _pack_w_in(w_in[l]),
            jnp.tile(q_norm_g[l], N_HEADS).reshape(1, -1),
            jnp.tile(k_norm_g[l], N_HEADS).reshape(1, -1), _pack_w_idx(w_in[l]), tm=512)
        attn = _attn_call(qbd, qi, wit, k, vt, ki, topk)
        x2 = _mix_call(x2, attn, u, ga, gp, mod, w_attn_br[l].astype(bf16),
                       w_pool_grp[l].astype(bf16), pool_scale[l].reshape(1, -1),
                       w_pool_br[l].astype(bf16), w_out[l].astype(bf16), tm=512)
        x2 = _ffn_call(x2, mod, norm2_g[l].reshape(1, -1), w_up[l].astype(bf16), conv_w[l],
                       conv_b[l].reshape(1, -1), w_down[l].astype(bf16), tm=256)
    return x2.reshape(bsz, s, d)
```

```python
import functools
import math

import jax
import jax.numpy as jnp
from jax import lax
from jax.experimental import pallas as pl
from jax.experimental.pallas import tpu as pltpu

f32 = jnp.float32
bf16 = jnp.bfloat16
i32 = jnp.int32

D_MODEL = 1024
N_HEADS = 8
HEAD_DIM = 64
ATTN_WIDTH = N_HEADS * HEAD_DIM
IDX_HEADS = 4
IDX_DIM = 64
TOPK_MAX = 256
POOL_WINDOWS = (2, 4, 8, 16)
POOL_GROUP_DIM = 128
POOL_WIDTH = 512
D_FF = 2816
EPS = 1e-6
N_MOD = 6

LANE = 128
SUBLANE = 8
VMEM_LIMIT = 58 * 1024 * 1024

LOG2E = 1.4426950408889634
INT_MIN = -2147483648
NEG_INF = float("-inf")

C_Q, C_K, C_V, C_QI, C_KI, C_WI, C_U, C_GA, C_GP, C_END = (
    0, 512, 1024, 1536, 1792, 1920, 2048, 2560, 3584, 4608)

TQ = 128
TK = 256
N_PAIR = N_HEADS // 2
IDX_K = 4 * LANE


def _sigmoid(x):
    return 1.0 / (1.0 + jnp.exp(-x))


def _rms_modulate(x, g, shift, scale):
    y = x * lax.rsqrt(jnp.mean(x * x, axis=-1, keepdims=True) + EPS)
    return (y * g) * (1.0 + scale) + shift


def _mod_kernel(c_ref, w_ref, b_ref, o_ref):
    c = c_ref[...]
    sc = c * _sigmoid(c)
    o_ref[...] = jnp.dot(sc, w_ref[...], precision=lax.Precision.HIGHEST,
                         preferred_element_type=f32) + b_ref[...]


def _mod_call(c8, w_ada, b_ada):
    n = w_ada.shape[1]
    tn = 1024
    return pl.pallas_call(
        _mod_kernel,
        grid=(n // tn,),
        in_specs=[pl.BlockSpec((SUBLANE, D_MODEL), lambda j: (0, 0)),
                  pl.BlockSpec((D_MODEL, tn), lambda j: (0, j)),
                  pl.BlockSpec((1, tn), lambda j: (0, j))],
        out_specs=pl.BlockSpec((SUBLANE, tn), lambda j: (0, j)),
        out_shape=jax.ShapeDtypeStruct((SUBLANE, n), f32),
        name="mod",
    )(c8, w_ada, b_ada)


def _split3(x):
    hi = x.astype(bf16).astype(f32)
    r = x - hi
    mid = r.astype(bf16).astype(f32)
    lo = (r - mid).astype(bf16).astype(f32)
    return hi, mid, lo


def _head_norm(z, g, bd):
    z2 = z * z
    hi = z2.astype(bf16)
    lo = (z2 - hi.astype(f32)).astype(bf16)
    ms = jnp.dot(hi, bd, preferred_element_type=f32) + jnp.dot(lo, bd, preferred_element_type=f32)
    return (z * lax.rsqrt(ms + EPS)) * g


def _proj_kernel(x_ref, mod_ref, g1_ref, w_ref, qg_ref, kg_ref, widx_ref,
                 qbd_ref, k_ref, vt_ref, qi_ref, ki_ref, wit_ref, u_ref, ga_ref, gp_ref, *, tm):
    x = x_ref[...]
    shift = mod_ref[0:1, 0:D_MODEL]
    scale = mod_ref[0:1, D_MODEL:2 * D_MODEL]
    h = _rms_modulate(x, g1_ref[...], shift, scale)
    proj = jnp.dot(h.astype(bf16), w_ref[...], preferred_element_type=f32)

    r = lax.broadcasted_iota(i32, (ATTN_WIDTH, ATTN_WIDTH), 0)
    c = lax.broadcasted_iota(i32, (ATTN_WIDTH, ATTN_WIDTH), 1)
    bd = jnp.where((r >> 6) == (c >> 6), 1.0 / HEAD_DIM, 0.0).astype(bf16)

    q = _head_norm(proj[:, C_Q:C_K], qg_ref[...], bd) * (HEAD_DIM ** -0.5 * LOG2E)
    k = _head_norm(proj[:, C_K:C_V], kg_ref[...], bd)
    k_ref[...] = k.astype(bf16)

    low = lax.broadcasted_iota(i32, (TQ, LANE), 1) < HEAD_DIM
    low_tm = lax.broadcasted_iota(i32, (tm, LANE), 1) < HEAD_DIM
    pidx = jnp.dot(h, widx_ref[...], precision=lax.Precision.HIGHEST, preferred_element_type=f32)
    qi = pidx[:, 0:256] * (IDX_DIM ** -0.5)
    up64 = lambda z: pltpu.roll(z, HEAD_DIM, axis=1)
    for g in range(tm // TQ):
        rows = slice(g * TQ, (g + 1) * TQ)
        for p in range(N_PAIR):
            qp = q[rows, p * LANE:(p + 1) * LANE]
            bd_q = jnp.concatenate([jnp.where(low, qp, 0.0), jnp.where(low, 0.0, qp)], axis=0)
            qbd_ref[g, p] = bd_q.T.astype(bf16)
        for hh in range(IDX_HEADS):
            seg = qi[rows, (hh // 2) * LANE:(hh // 2 + 1) * LANE]
            own = low if hh % 2 == 0 else jnp.logical_not(low)
            qh, qm, ql = _split3(jnp.where(own, seg, 0.0))
            both = lambda z: z + up64(z)
            in_low = lambda z: jnp.where(low, both(z), 0.0)
            hcols = slice(hh * TQ, (hh + 1) * TQ)
            for kt, term in enumerate((both(qh), both(qm), in_low(ql), in_low(qh))):
                qi_ref[g, kt * LANE:(kt + 1) * LANE, hcols] = term.T.astype(bf16)

    v = proj[:, C_V:C_QI]
    vt = v.T.astype(bf16)
    for cc in range(tm // TK):
        for hh in range(N_HEADS):
            vt_ref[cc, hh] = vt[hh * HEAD_DIM:(hh + 1) * HEAD_DIM, cc * TK:(cc + 1) * TK]

    kw = pidx[:, 256:384]
    kh, km, kl = _split3(jnp.where(low_tm, kw, 0.0))
    ki_ref[:, 0:LANE] = (kh + up64(km)).astype(bf16)
    ki_ref[:, LANE:2 * LANE] = kl.astype(bf16)
    wt = (kw * (IDX_HEADS ** -0.5)).T
    wit_ref[...] = wt[HEAD_DIM:HEAD_DIM + SUBLANE, :]
    u_ref[...] = proj[:, C_U:C_GA]
    ga_ref[...] = _sigmoid(proj[:, C_GA:C_GP]).astype(bf16)
    gp_ref[...] = _sigmoid(proj[:, C_GP:C_END]).astype(bf16)


def _proj_call(x2, mod, g1, w_in_p, qg, kg, widx, tm):
    s = x2.shape[0]
    nq = s // TQ
    const = lambda i: (0, 0)
    return pl.pallas_call(
        functools.partial(_proj_kernel, tm=tm),
        grid=(s // tm,),
        in_specs=[pl.BlockSpec((tm, D_MODEL), lambda i: (i, 0)),
                  pl.BlockSpec((SUBLANE, N_MOD * D_MODEL), const),
                  pl.BlockSpec((1, D_MODEL), const),
                  pl.BlockSpec((D_MODEL, C_END), const),
                  pl.BlockSpec((1, ATTN_WIDTH), const),
                  pl.BlockSpec((1, ATTN_WIDTH), const),
                  pl.BlockSpec((D_MODEL, 3 * LANE), const)],
        out_specs=[pl.BlockSpec((tm // TQ, N_PAIR, LANE, 2 * TQ), lambda i: (i, 0, 0, 0)),
                   pl.BlockSpec((tm, ATTN_WIDTH), lambda i: (i, 0)),
                   pl.BlockSpec((tm // TK, N_HEADS, HEAD_DIM, TK), lambda i: (i, 0, 0, 0)),
                   pl.BlockSpec((tm // TQ, IDX_K, IDX_HEADS * TQ), lambda i: (i, 0, 0)),
                   pl.BlockSpec((tm, 2 * LANE), lambda i: (i, 0)),
                   pl.BlockSpec((SUBLANE, tm), lambda i: (0, i)),
                   pl.BlockSpec((tm, POOL_WIDTH), lambda i: (i, 0)),
                   pl.BlockSpec((tm, D_MODEL), lambda i: (i, 0)),
                   pl.BlockSpec((tm, D_MODEL), lambda i: (i, 0))],
        out_shape=[jax.ShapeDtypeStruct((nq, N_PAIR, LANE, 2 * TQ), bf16),
                   jax.ShapeDtypeStruct((s, ATTN_WIDTH), bf16),
                   jax.ShapeDtypeStruct((s // TK, N_HEADS, HEAD_DIM, TK), bf16),
                   jax.ShapeDtypeStruct((nq, IDX_K, IDX_HEADS * TQ), bf16),
                   jax.ShapeDtypeStruct((s, 2 * LANE), bf16),
                   jax.ShapeDtypeStruct((SUBLANE, s), f32),
                   jax.ShapeDtypeStruct((s, POOL_WIDTH), f32),
                   jax.ShapeDtypeStruct((s, D_MODEL), bf16),
                   jax.ShapeDtypeStruct((s, D_MODEL), bf16)],
        compiler_params=pltpu.CompilerParams(dimension_semantics=("parallel",),
                                             vmem_limit_bytes=VMEM_LIMIT),
        name="proj",
    )(x2, mod, g1, w_in_p, qg, kg, widx)


def _slope2(h):
    return (2.0 ** (-8.0 * (h + 1) / N_HEADS)) * LOG2E


def _tree(op, xs):
    xs = list(xs)
    while len(xs) > 1:
        nxt = [op(xs[a], xs[a + 1]) for a in range(0, len(xs) - 1, 2)]
        if len(xs) % 2:
            nxt.append(xs[-1])
        xs = nxt
    return xs[0]


def _row_groups(x):
    return [x[j * SUBLANE:(j + 1) * SUBLANE] for j in range(x.shape[0] // SUBLANE)]


N_CNT_ACC = 8
N_STAGE = 4
PV_ROWS = HEAD_DIM + 16


def _attn_kernel(qbd_ref, qi_ref, wi_ref, k_ref, vt_ref, ki_ref, o_ref,
                 keys_ref, pos_ref, slf_ref, il0_ref, il1_ref, il2_ref, il3_ref,
                 pre0_ref, pre1_ref, pre2_ref, pre3_ref, sm0_ref, sm1_ref, cmax0_ref, cmax1_ref,
                 p0_ref, p1_ref, alpha0_ref, alpha1_ref, m_ref, acc_ref, *, topk):
    il_ref = (il0_ref, il1_ref, il2_ref, il3_ref)
    pre_ref = (pre0_ref, pre1_ref, pre2_ref, pre3_ref)
    sm_ref = (sm0_ref, sm1_ref)
    cmax_ref = (cmax0_ref, cmax1_ref)
    p_ref = (p0_ref, p1_ref)
    alpha_ref = (alpha0_ref, alpha1_ref)
    i = pl.program_id(0)
    nsc = (i + 4) >> 2
    nch = 2 * nsc
    nbody = (i + 8) >> 3

    def chunk_start(c):
        return pl.multiple_of(c * TK, TK)

    def pipeline(produce, consume, carry):
        for j in range(N_STAGE):
            produce(j, j)

        def body(b, carry):
            c0 = N_STAGE * b
            for j in range(N_STAGE):
                carry = consume(c0 + j, j, carry)
                produce(c0 + N_STAGE + j, j)
            return carry

        carry = lax.fori_loop(0, nbody - 1, body, carry)
        c0 = N_STAGE * (nbody - 1)
        for j in range(N_STAGE):
            carry = consume(c0 + j, j, carry)
        return carry

    @pl.when(i == 0)
    def _():
        col = lax.broadcasted_iota(i32, (TK, LANE), 1)
        row = lax.broadcasted_iota(i32, (TK, LANE), 0).astype(f32)
        pos_ref[...] = jnp.where(col < 3, row, 0.0).astype(bf16)
        term = lax.broadcasted_iota(i32, (LANE, 2 * TQ), 0)
        lane = lax.broadcasted_iota(i32, (LANE, 2 * TQ), 1)
        for p in range(N_PAIR):
            sl = jnp.where(lane < TQ, _slope2(2 * p), _slope2(2 * p + 1)) + jnp.zeros((LANE, 2 * TQ), f32)
            hi, mid, lo = _split3(sl)
            slf_ref[p] = jnp.where(term == 0, hi, jnp.where(term == 1, mid,
                                   jnp.where(term == 2, lo, 0.0))).astype(bf16)

    qi = qi_ref[0]
    w = wi_ref[...]
    d0 = (lax.broadcasted_iota(i32, (TK, TQ), 1) - lax.broadcasted_iota(i32, (TK, TQ), 0))

    def idx_matmul(c, slot):
        rows = pl.ds(chunk_start(c), TK)
        kx = ki_ref[rows, 0:LANE]
        ky = ki_ref[rows, LANE:2 * LANE]
        il_ref[slot][...] = jnp.dot(jnp.concatenate([kx, kx, kx, ky], axis=1), qi,
                                    preferred_element_type=f32)

    def idx_keys(c, slot, carry):
        r0 = chunk_start(c)
        sc = jnp.maximum(il_ref[slot][:, 0:TQ], 0.0) * w[0:1, :]
        for hh in range(1, IDX_HEADS):
            sc = sc + jnp.maximum(il_ref[slot][:, hh * TQ:(hh + 1) * TQ], 0.0) * w[hh:hh + 1, :]
        b = lax.bitcast_convert_type(sc, i32)
        key = jnp.where(b < 0, -(b & 0x7FFFFFFF), b)
        valid = d0 >= (r0 - i * TQ)
        keys_ref[pl.ds(r0, TK), :] = jnp.where(valid, key, INT_MIN)
        return carry

    pipeline(idx_matmul, idx_keys, 0)

    def count_ge(cand):
        def body(b, accs):
            accs = list(accs)
            for part in range(N_STAGE):
                r0 = pl.multiple_of((b * N_STAGE + part) * TK, TK)
                ind = jnp.where(keys_ref[pl.ds(r0, TK), :] >= cand, 1, 0)
                for j, g in enumerate(_row_groups(ind)):
                    accs[j % N_CNT_ACC] = accs[j % N_CNT_ACC] + g
            return tuple(accs)
        accs = lax.fori_loop(0, nbody, body,
                             tuple(jnp.zeros((SUBLANE, TQ), i32) for _ in range(N_CNT_ACC)))
        return jnp.sum(_tree(jnp.add, accs), axis=0, keepdims=True)

    def bit_step(bi, st):
        t, above = st
        cand = t + lax.shift_left(jnp.int32(1), 31 - bi)
        cnt = count_ge(cand)
        up = cnt >= topk
        return jnp.where(up, cand, t), jnp.where(up, above, cnt)

    t, above = lax.fori_loop(0, 32, bit_step,
                             (jnp.full((1, TQ), INT_MIN, i32), jnp.zeros((1, TQ), i32)))
    t = jnp.maximum(t, INT_MIN + 1)
    r_tie = (topk - above).astype(f32)

    tri = jnp.where(lax.broadcasted_iota(i32, (TK, TK), 0) >= lax.broadcasted_iota(i32, (TK, TK), 1),
                    1.0, 0.0).astype(bf16)

    def tie_rank(c, slot):
        e = jnp.where(keys_ref[pl.ds(chunk_start(c), TK), :] == t, 1.0, 0.0).astype(bf16)
        pre_ref[slot][...] = jnp.dot(tri, e, preferred_element_type=f32)

    def mask_out(c, slot, rank):
        r0 = chunk_start(c)
        kk = keys_ref[pl.ds(r0, TK), :]
        pre = pre_ref[slot][...] + rank
        nm = jnp.where(kk > t, 0.0, jnp.where(kk == t, jnp.where(pre <= r_tie, 0.0, NEG_INF), NEG_INF))
        keys_ref[pl.ds(r0, TK), :] = lax.bitcast_convert_type(nm, i32)
        return pre[TK - 1:TK, :]

    pipeline(tie_rank, mask_out, jnp.zeros((1, TQ), f32))

    m_ref[...] = jnp.full(m_ref.shape, NEG_INF, f32)
    acc_ref[...] = jnp.zeros(acc_ref.shape, f32)
    lane2 = lax.broadcasted_iota(i32, (1, 2 * TQ), 1)
    ones_rows = jnp.ones((PV_ROWS - HEAD_DIM, TK), bf16)

    def logits(c, slot):
        rows = pl.ds(chunk_start(c), TK)
        nm = lax.bitcast_convert_type(keys_ref[rows, :], f32)
        nm2 = jnp.concatenate([nm, nm], axis=1)
        for p in range(N_PAIR):
            lhs = jnp.concatenate([k_ref[rows, p * LANE:(p + 1) * LANE], pos_ref[...]], axis=1)
            rhs = jnp.concatenate([qbd_ref[0, p], slf_ref[p]], axis=0)
            sm = jnp.dot(lhs, rhs, preferred_element_type=f32) + nm2
            sm_ref[slot][p] = sm
            cmax_ref[slot][p] = jnp.max(_tree(jnp.maximum, _row_groups(sm)), axis=0, keepdims=True)

    def probs(c, slot):
        r0f = jnp.asarray(c * TK, dtype=f32)
        for p in range(N_PAIR):
            coff = jnp.where(lane2 < TQ, _slope2(2 * p), _slope2(2 * p + 1)) * r0f
            m_old = m_ref[p]
            m_new = jnp.maximum(m_old, cmax_ref[slot][p] + coff)
            m_safe = jnp.where(m_new == NEG_INF, 0.0, m_new)
            alpha_ref[slot][p] = jnp.where(m_old == NEG_INF, 0.0, jnp.exp2(m_old - m_safe))
            p_ref[slot][p] = jnp.exp2((sm_ref[slot][p] - (m_safe - coff)).astype(bf16))
            m_ref[p] = m_new

    def weighted_sum(c, slot):
        for h in range(N_HEADS):
            lanes = slice((h % 2) * TQ, (h % 2 + 1) * TQ)
            lhs = jnp.concatenate([vt_ref[c, h], ones_rows], axis=0)
            pv = jnp.dot(lhs, p_ref[slot][h // 2, :, lanes], preferred_element_type=f32)
            acc_ref[h] = acc_ref[h] * alpha_ref[slot][h // 2, :, lanes] + pv

    logits(0, 0)
    probs(0, 0)
    logits(1, 1)

    def attn_step(c0):
        probs(c0 + 1, 1)
        logits(c0 + 2, 0)
        logits(c0 + 3, 1)
        weighted_sum(c0, 0)
        probs(c0 + 2, 0)
        weighted_sum(c0 + 1, 1)

    def attn_body2(b, carry):
        attn_step(4 * b)
        attn_step(4 * b + 2)
        return carry

    def attn_body1(b, carry):
        attn_step(2 * (nsc - 2))
        return carry

    lax.fori_loop(0, (nsc - 1) >> 1, attn_body2, 0)
    lax.fori_loop(0, (nsc - 1) & 1, attn_body1, 0)
    weighted_sum(nch - 2, 0)
    probs(nch - 1, 1)
    weighted_sum(nch - 1, 1)

    outs = []
    for h in range(N_HEADS):
        a = acc_ref[h]
        outs.append(a[0:HEAD_DIM] / a[HEAD_DIM:HEAD_DIM + 1])
    o_ref[...] = jnp.concatenate(outs, axis=0).T.astype(bf16)


def _attn_call(qbd, qi, wit, k, vt, ki, topk):
    s = k.shape[0]
    nq = s // TQ
    whole = pl.BlockSpec(memory_space=pltpu.VMEM)
    il = [pltpu.VMEM((TK, IDX_HEADS * TQ), f32)] * N_STAGE
    pre = [pltpu.VMEM((TK, TQ), f32)] * N_STAGE
    return pl.pallas_call(
        functools.partial(_attn_kernel, topk=topk),
        grid=(nq,),
        in_specs=[pl.BlockSpec((1, N_PAIR, LANE, 2 * TQ), lambda i: (i, 0, 0, 0)),
                  pl.BlockSpec((1, IDX_K, IDX_HEADS * TQ), lambda i: (i, 0, 0)),
                  pl.BlockSpec((SUBLANE, TQ), lambda i: (0, i)),
                  whole, whole, whole],
        out_specs=pl.BlockSpec((TQ, ATTN_WIDTH), lambda i: (i, 0)),
        out_shape=jax.ShapeDtypeStruct((s, ATTN_WIDTH), bf16),
        scratch_shapes=[pltpu.VMEM((s, TQ), i32),
                        pltpu.VMEM((TK, LANE), bf16),
                        pltpu.VMEM((N_PAIR, LANE, 2 * TQ), bf16),
                        *il, *pre,
                        pltpu.VMEM((N_PAIR, TK, 2 * TQ), f32),
                        pltpu.VMEM((N_PAIR, TK, 2 * TQ), f32),
                        pltpu.VMEM((N_PAIR, 1, 2 * TQ), f32),
                        pltpu.VMEM((N_PAIR, 1, 2 * TQ), f32),
                        pltpu.VMEM((N_PAIR, TK, 2 * TQ), bf16),
                        pltpu.VMEM((N_PAIR, TK, 2 * TQ), bf16),
                        pltpu.VMEM((N_PAIR, 1, 2 * TQ), f32),
                        pltpu.VMEM((N_PAIR, 1, 2 * TQ), f32),
                        pltpu.VMEM((N_PAIR, 1, 2 * TQ), f32),
                        pltpu.VMEM((N_HEADS, PV_ROWS, TQ), f32)],
        compiler_params=pltpu.CompilerParams(dimension_semantics=("arbitrary",),
                                             vmem_limit_bytes=VMEM_LIMIT),
        name="attn",
    )(qbd, qi, wit, k, vt, ki)


HALO_POOL = 16


def _mix_kernel(x_ref, attn_ref, u_ref, uh_ref, ga_ref, gp_ref, mod_ref,
                wab_ref, wg_ref, ps_ref, wpb_ref, wo_ref, o_ref, *, tm):
    i = pl.program_id(0)
    y_attn = jnp.dot(attn_ref[...], wab_ref[...], preferred_element_type=f32)

    u = u_ref[...]
    halo = jnp.where(i > 0, uh_ref[...], 0.0)
    a = jnp.concatenate([halo, u], axis=0)
    tpos = (i * tm + lax.broadcasted_iota(i32, (tm, POOL_GROUP_DIM), 0) + 1).astype(f32)
    mixed = []
    for g, wdw in enumerate(POOL_WINDOWS):
        ag = a[:, g * POOL_GROUP_DIM:(g + 1) * POOL_GROUP_DIM]
        ug = ag[HALO_POOL:HALO_POOL + tm]
        ssum = ug
        for j in range(1, wdw):
            ssum = ssum + ag[HALO_POOL - j:HALO_POOL - j + tm]
        pooled = ssum / jnp.minimum(tpos, float(wdw)) - ug
        mixed.append(jnp.dot(pooled.astype(bf16), wg_ref[g], preferred_element_type=f32))
    mixed = jnp.concatenate(mixed, axis=1) * ps_ref[...]
    y_pool = jnp.dot(mixed.astype(bf16), wpb_ref[...], preferred_element_type=f32)

    merged = ga_ref[...].astype(f32) * y_attn + gp_ref[...].astype(f32) * y_pool
    o = jnp.dot(merged.astype(bf16), wo_ref[...], preferred_element_type=f32)
    gate = mod_ref[0:1, 2 * D_MODEL:3 * D_MODEL]
    o_ref[...] = x_ref[...] + gate * o


def _mix_call(x2, attn, u, ga, gp, mod, wab, wg, ps, wpb, wo, tm):
    s = x2.shape[0]
    const2 = lambda i: (0, 0)
    hb = tm // HALO_POOL
    return pl.pallas_call(
        functools.partial(_mix_kernel, tm=tm),
        grid=(s // tm,),
        in_specs=[pl.BlockSpec((tm, D_MODEL), lambda i: (i, 0)),
                  pl.BlockSpec((tm, ATTN_WIDTH), lambda i: (i, 0)),
                  pl.BlockSpec((tm, POOL_WIDTH), lambda i: (i, 0)),
                  pl.BlockSpec((HALO_POOL, POOL_WIDTH), lambda i: (jnp.maximum(i * hb - 1, 0), 0)),
                  pl.BlockSpec((tm, D_MODEL), lambda i: (i, 0)),
                  pl.BlockSpec((tm, D_MODEL), lambda i: (i, 0)),
                  pl.BlockSpec((SUBLANE, N_MOD * D_MODEL), const2),
                  pl.BlockSpec((ATTN_WIDTH, D_MODEL), const2),
                  pl.BlockSpec((len(POOL_WINDOWS), POOL_GROUP_DIM, POOL_GROUP_DIM), lambda i: (0, 0, 0)),
                  pl.BlockSpec((1, POOL_WIDTH), const2),
                  pl.BlockSpec((POOL_WIDTH, D_MODEL), const2),
                  pl.BlockSpec((D_MODEL, D_MODEL), const2)],
        out_specs=pl.BlockSpec((tm, D_MODEL), lambda i: (i, 0)),
        out_shape=jax.ShapeDtypeStruct((s, D_MODEL), f32),
        compiler_params=pltpu.CompilerParams(dimension_semantics=("parallel",),
                                             vmem_limit_bytes=VMEM_LIMIT),
        name="mix",
    )(x2, attn, u, u, ga, gp, mod, wab, wg, ps, wpb, wo)


HALO_CONV = 8


def _ffn_kernel(x_ref, xh_ref, mod_ref, g2_ref, wup_ref, cw_ref, cb_ref, wdn_ref, o_ref, *, tm):
    i = pl.program_id(0)
    shift = mod_ref[0:1, 3 * D_MODEL:4 * D_MODEL]
    scale = mod_ref[0:1, 4 * D_MODEL:5 * D_MODEL]
    gate = mod_ref[0:1, 5 * D_MODEL:6 * D_MODEL]
    g2 = g2_ref[...]
    x = x_ref[...]
    h = _rms_modulate(x, g2, shift, scale)
    hh = jnp.where(i > 0, _rms_modulate(xh_ref[...], g2, shift, scale), 0.0)
    ha = jnp.concatenate([hh, h], axis=0).astype(bf16)
    up = jnp.dot(ha, wup_ref[...], preferred_element_type=f32)
    cw = cw_ref[...]
    y = cb_ref[...] + cw[0:1, :] * up[HALO_CONV - 2:HALO_CONV - 2 + tm]
    y = y + cw[1:2, :] * up[HALO_CONV - 1:HALO_CONV - 1 + tm]
    y = y + cw[2:3, :] * up[HALO_CONV:HALO_CONV + tm]
    a = y[:, 0:D_FF]
    b = y[:, D_FF:2 * D_FF]
    gated = (a * _sigmoid(a)) * b
    o = jnp.dot(gated.astype(bf16), wdn_ref[...], preferred_element_type=f32)
    o_ref[...] = x + gate * o


def _ffn_call(x1, mod, g2, wup, cw, cb, wdn, tm):
    s = x1.shape[0]
    const2 = lambda i: (0, 0)
    hb = tm // HALO_CONV
    return pl.pallas_call(
        functools.partial(_ffn_kernel, tm=tm),
        grid=(s // tm,),
        in_specs=[pl.BlockSpec((tm, D_MODEL), lambda i: (i, 0)),
                  pl.BlockSpec((HALO_CONV, D_MODEL), lambda i: (jnp.maximum(i * hb - 1, 0), 0)),
                  pl.BlockSpec((SUBLANE, N_MOD * D_MODEL), const2),
                  pl.BlockSpec((1, D_MODEL), const2),
                  pl.BlockSpec((D_MODEL, 2 * D_FF), const2),
                  pl.BlockSpec((3, 2 * D_FF), const2),
                  pl.BlockSpec((1, 2 * D_FF), const2),
                  pl.BlockSpec((D_FF, D_MODEL), const2)],
        out_specs=pl.BlockSpec((tm, D_MODEL), lambda i: (i, 0)),
        out_shape=jax.ShapeDtypeStruct((s, D_MODEL), f32),
        compiler_params=pltpu.CompilerParams(dimension_semantics=("parallel",),
                                             vmem_limit_bytes=VMEM_LIMIT),
        name="ffn",
    )(x1, x1, mod, g2, wup, cw, cb, wdn)


def _pack_w_in(w):
    q, k, v, qi, ki, wi, u, ga, gp = (w[:, 0:512], w[:, 512:1024], w[:, 1024:1536], w[:, 1536:1792],
                                      w[:, 1792:1856], w[:, 1856:1860], w[:, 1860:2372],
                                      w[:, 2372:3396], w[:, 3396:4420])
    wi_pad = jnp.pad(wi, ((0, 0), (0, LANE - IDX_HEADS)))
    return jnp.concatenate([q, k, v, qi, ki, ki, wi_pad, u, ga, gp], axis=1).astype(bf16)


def _pack_w_idx(w):
    return jnp.pad(w[:, 1536:1860], ((0, 0), (0, 3 * LANE - (1860 - 1536))))


def kernel(x, c, w_ada, b_ada, norm1_g, w_in, q_norm_g, k_norm_g, w_attn_br, w_pool_grp,
           pool_scale, w_pool_br, w_out, norm2_g, w_up, conv_w, conv_b, w_down):
    bsz, s, d = x.shape
    assert bsz == 1 and d == D_MODEL and s % (N_STAGE * TK) == 0
    depth = w_ada.shape[0]
    topk = min(TOPK_MAX, s // 4)
    x2 = x.reshape(s, d)
    c8 = jnp.pad(c, ((0, SUBLANE - bsz), (0, 0)))
    for l in range(depth):
        mod = _mod_call(c8, w_ada[l], b_ada[l].reshape(1, -1))
        qbd, k, vt, qi, ki, wit, u, ga, gp = _proj_call(
            x2, mod, norm1_g[l].reshape(1, -1), _pack_w_in(w_in[l]),
            jnp.tile(q_norm_g[l], N_HEADS).reshape(1, -1),
            jnp.tile(k_norm_g[l], N_HEADS).reshape(1, -1), _pack_w_idx(w_in[l]), tm=512)
        attn = _attn_call(qbd, qi, wit, k, vt, ki, topk)
        x2 = _mix_call(x2, attn, u, ga, gp, mod, w_attn_br[l].astype(bf16),
                       w_pool_grp[l].astype(bf16), pool_scale[l].reshape(1, -1),
                       w_pool_br[l].astype(bf16), w_out[l].astype(bf16), tm=512)
        x2 = _ffn_call(x2, mod, norm2_g[l].reshape(1, -1), w_up[l].astype(bf16), conv_w[l],
                       conv_b[l].reshape(1, -1), w_down[l].astype(bf16), tm=256)
    return x2.reshape(bsz, s, d)
```

```python
import functools
import math

import jax
import jax.numpy as jnp
from jax import lax
from jax.experimental import pallas as pl
from jax.experimental.pallas import tpu as pltpu

f32 = jnp.float32
bf16 = jnp.bfloat16
i32 = jnp.int32

D_MODEL = 1024
N_HEADS = 8
HEAD_DIM = 64
ATTN_WIDTH = N_HEADS * HEAD_DIM
IDX_HEADS = 4
IDX_DIM = 64
TOPK_MAX = 256
POOL_WINDOWS = (2, 4, 8, 16)
POOL_GROUP_DIM = 128
POOL_WIDTH = 512
D_FF = 2816
EPS = 1e-6
N_MOD = 6

LANE = 128
SUBLANE = 8
VMEM_LIMIT = 58 * 1024 * 1024

LOG2E = 1.4426950408889634
INT_MIN = -2147483648
NEG_INF = float("-inf")

IN_IDX0, IN_IDX1, IN_END = 1536, 1860, 4420
C_Q, C_K, C_V, C_QKV = 0, 512, 1024, 1536
C_U, C_GA, C_GP, C_REST = 0, 512, 1536, 2560

TQ = 128
TK = 256
N_PAIR = N_HEADS // 2
IDX_K = 4 * LANE


def _sigmoid(x):
    return 1.0 / (1.0 + jnp.exp(-x))


def _rms_modulate(x, g, shift, scale):
    y = x * lax.rsqrt(jnp.mean(x * x, axis=-1, keepdims=True) + EPS)
    return (y * g) * (1.0 + scale) + shift


def _mod_kernel(c_ref, w_ref, b_ref, o_ref):
    c = c_ref[...]
    sc = c * _sigmoid(c)
    o_ref[...] = jnp.dot(sc, w_ref[...], precision=lax.Precision.HIGHEST,
                         preferred_element_type=f32) + b_ref[...]


def _mod_call(c8, w_ada, b_ada):
    n = w_ada.shape[1]
    tn = 1024
    return pl.pallas_call(
        _mod_kernel,
        grid=(n // tn,),
        in_specs=[pl.BlockSpec((SUBLANE, D_MODEL), lambda j: (0, 0)),
                  pl.BlockSpec((D_MODEL, tn), lambda j: (0, j)),
                  pl.BlockSpec((1, tn), lambda j: (0, j))],
        out_specs=pl.BlockSpec((SUBLANE, tn), lambda j: (0, j)),
        out_shape=jax.ShapeDtypeStruct((SUBLANE, n), f32),
        name="mod",
    )(c8, w_ada, b_ada)


def _split3(x):
    hi = x.astype(bf16).astype(f32)
    r = x - hi
    mid = r.astype(bf16).astype(f32)
    lo = (r - mid).astype(bf16).astype(f32)
    return hi, mid, lo


def _head_norm(z, g, bd):
    z2 = z * z
    hi = z2.astype(bf16)
    lo = (z2 - hi.astype(f32)).astype(bf16)
    ms = jnp.dot(hi, bd, preferred_element_type=f32) + jnp.dot(lo, bd, preferred_element_type=f32)
    return (z * lax.rsqrt(ms + EPS)) * g


def _proj_kernel(x_ref, mod_ref, g1_ref, wa_ref, wb_ref, qg_ref, kg_ref, widx_ref,
                 qbd_ref, k_ref, vt_ref, qi_ref, ki_ref, wit_ref, u_ref, ga_ref, gp_ref, *, tm):
    x = x_ref[...]
    shift = mod_ref[0:1, 0:D_MODEL]
    scale = mod_ref[0:1, D_MODEL:2 * D_MODEL]
    h = _rms_modulate(x, g1_ref[...], shift, scale)
    hb = h.astype(bf16)
    proj = jnp.dot(hb, wa_ref[...], preferred_element_type=f32)
    rest = jnp.dot(hb, wb_ref[...], preferred_element_type=f32)

    r = lax.broadcasted_iota(i32, (ATTN_WIDTH, ATTN_WIDTH), 0)
    c = lax.broadcasted_iota(i32, (ATTN_WIDTH, ATTN_WIDTH), 1)
    bd = jnp.where((r >> 6) == (c >> 6), 1.0 / HEAD_DIM, 0.0).astype(bf16)

    q = _head_norm(proj[:, C_Q:C_K], qg_ref[...], bd) * (HEAD_DIM ** -0.5 * LOG2E)
    k = _head_norm(proj[:, C_K:C_V], kg_ref[...], bd)
    k_ref[...] = k.astype(bf16)

    low = lax.broadcasted_iota(i32, (TQ, LANE), 1) < HEAD_DIM
    low_tm = lax.broadcasted_iota(i32, (tm, LANE), 1) < HEAD_DIM
    pidx = jnp.dot(h, widx_ref[...], precision=lax.Precision.HIGHEST, preferred_element_type=f32)
    qi = pidx[:, 0:256] * (IDX_DIM ** -0.5)
    up64 = lambda z: pltpu.roll(z, HEAD_DIM, axis=1)
    for g in range(tm // TQ):
        rows = slice(g * TQ, (g + 1) * TQ)
        for p in range(N_PAIR):
            qp = q[rows, p * LANE:(p + 1) * LANE]
            bd_q = jnp.concatenate([jnp.where(low, qp, 0.0), jnp.where(low, 0.0, qp)], axis=0)
            qbd_ref[g, p] = bd_q.T.astype(bf16)
        for hh in range(IDX_HEADS):
            seg = qi[rows, (hh // 2) * LANE:(hh // 2 + 1) * LANE]
            own = low if hh % 2 == 0 else jnp.logical_not(low)
            qh, qm, ql = _split3(jnp.where(own, seg, 0.0))
            both = lambda z: z + up64(z)
            in_low = lambda z: jnp.where(low, both(z), 0.0)
            hcols = slice(hh * TQ, (hh + 1) * TQ)
            for kt, term in enumerate((both(qh), both(qm), in_low(ql), in_low(qh))):
                qi_ref[g, kt * LANE:(kt + 1) * LANE, hcols] = term.T.astype(bf16)

    v = proj[:, C_V:C_QKV]
    vt = v.T.astype(bf16)
    for cc in range(tm // TK):
        for hh in range(N_HEADS):
            vt_ref[cc, hh] = vt[hh * HEAD_DIM:(hh + 1) * HEAD_DIM, cc * TK:(cc + 1) * TK]

    kw = pidx[:, 256:384]
    kh, km, kl = _split3(jnp.where(low_tm, kw, 0.0))
    ki_ref[:, 0:LANE] = (kh + up64(km)).astype(bf16)
    ki_ref[:, LANE:2 * LANE] = kl.astype(bf16)
    wt = (kw * (IDX_HEADS ** -0.5)).T
    wit_ref[...] = wt[HEAD_DIM:HEAD_DIM + SUBLANE, :]
    u_ref[...] = rest[:, C_U:C_GA]
    ga_ref[...] = _sigmoid(rest[:, C_GA:C_GP]).astype(bf16)
    gp_ref[...] = _sigmoid(rest[:, C_GP:C_REST]).astype(bf16)


def _proj_call(x2, mod, g1, w_qkv, w_rest, qg, kg, widx, tm):
    s = x2.shape[0]
    nq = s // TQ
    const = lambda i: (0, 0)
    return pl.pallas_call(
        functools.partial(_proj_kernel, tm=tm),
        grid=(s // tm,),
        in_specs=[pl.BlockSpec((tm, D_MODEL), lambda i: (i, 0)),
                  pl.BlockSpec((SUBLANE, N_MOD * D_MODEL), const),
                  pl.BlockSpec((1, D_MODEL), const),
                  pl.BlockSpec((D_MODEL, C_QKV), const),
                  pl.BlockSpec((D_MODEL, C_REST), const),
                  pl.BlockSpec((1, ATTN_WIDTH), const),
                  pl.BlockSpec((1, ATTN_WIDTH), const),
                  pl.BlockSpec((D_MODEL, 3 * LANE), const)],
        out_specs=[pl.BlockSpec((tm // TQ, N_PAIR, LANE, 2 * TQ), lambda i: (i, 0, 0, 0)),
                   pl.BlockSpec((tm, ATTN_WIDTH), lambda i: (i, 0)),
                   pl.BlockSpec((tm // TK, N_HEADS, HEAD_DIM, TK), lambda i: (i, 0, 0, 0)),
                   pl.BlockSpec((tm // TQ, IDX_K, IDX_HEADS * TQ), lambda i: (i, 0, 0)),
                   pl.BlockSpec((tm, 2 * LANE), lambda i: (i, 0)),
                   pl.BlockSpec((SUBLANE, tm), lambda i: (0, i)),
                   pl.BlockSpec((tm, POOL_WIDTH), lambda i: (i, 0)),
                   pl.BlockSpec((tm, D_MODEL), lambda i: (i, 0)),
                   pl.BlockSpec((tm, D_MODEL), lambda i: (i, 0))],
        out_shape=[jax.ShapeDtypeStruct((nq, N_PAIR, LANE, 2 * TQ), bf16),
                   jax.ShapeDtypeStruct((s, ATTN_WIDTH), bf16),
                   jax.ShapeDtypeStruct((s // TK, N_HEADS, HEAD_DIM, TK), bf16),
                   jax.ShapeDtypeStruct((nq, IDX_K, IDX_HEADS * TQ), bf16),
                   jax.ShapeDtypeStruct((s, 2 * LANE), bf16),
                   jax.ShapeDtypeStruct((SUBLANE, s), f32),
                   jax.ShapeDtypeStruct((s, POOL_WIDTH), f32),
                   jax.ShapeDtypeStruct((s, D_MODEL), bf16),
                   jax.ShapeDtypeStruct((s, D_MODEL), bf16)],
        compiler_params=pltpu.CompilerParams(dimension_semantics=("parallel",),
                                             vmem_limit_bytes=VMEM_LIMIT),
        name="proj",
    )(x2, mod, g1, w_qkv, w_rest, qg, kg, widx)


def _slope2(h):
    return (2.0 ** (-8.0 * (h + 1) / N_HEADS)) * LOG2E


def _tree(op, xs):
    xs = list(xs)
    while len(xs) > 1:
        nxt = [op(xs[a], xs[a + 1]) for a in range(0, len(xs) - 1, 2)]
        if len(xs) % 2:
            nxt.append(xs[-1])
        xs = nxt
    return xs[0]


def _row_groups(x):
    return [x[j * SUBLANE:(j + 1) * SUBLANE] for j in range(x.shape[0] // SUBLANE)]


N_CNT_ACC = 8
N_STAGE = 4
PV_ROWS = HEAD_DIM + 16


def _attn_kernel(qbd_ref, qi_ref, wi_ref, k_ref, vt_ref, ki_ref, o_ref,
                 keys_ref, pos_ref, slf_ref, il0_ref, il1_ref,
                 pre0_ref, pre1_ref, pre2_ref, pre3_ref, sm0_ref, sm1_ref, cmax0_ref, cmax1_ref,
                 p0_ref, p1_ref, alpha0_ref, alpha1_ref, m_ref, acc_ref, *, topk):
    il_ref = (il0_ref, il1_ref)
    pre_ref = (pre0_ref, pre1_ref, pre2_ref, pre3_ref)
    sm_ref = (sm0_ref, sm1_ref)
    cmax_ref = (cmax0_ref, cmax1_ref)
    p_ref = (p0_ref, p1_ref)
    alpha_ref = (alpha0_ref, alpha1_ref)
    i = pl.program_id(0)
    nsc = (i + 4) >> 2
    nch = 2 * nsc
    nbody = (i + 8) >> 3

    def chunk_start(c):
        return pl.multiple_of(c * TK, TK)

    def pipeline(produce, consume, carry):
        for j in range(N_STAGE):
            produce(j, j)

        def body(b, carry):
            c0 = N_STAGE * b
            for j in range(N_STAGE):
                carry = consume(c0 + j, j, carry)
                produce(c0 + N_STAGE + j, j)
            return carry

        carry = lax.fori_loop(0, nbody - 1, body, carry)
        c0 = N_STAGE * (nbody - 1)
        for j in range(N_STAGE):
            carry = consume(c0 + j, j, carry)
        return carry

    @pl.when(i == 0)
    def _():
        col = lax.broadcasted_iota(i32, (TK, LANE), 1)
        row = lax.broadcasted_iota(i32, (TK, LANE), 0).astype(f32)
        pos_ref[...] = jnp.where(col < 3, row, 0.0).astype(bf16)
        term = lax.broadcasted_iota(i32, (LANE, 2 * TQ), 0)
        lane = lax.broadcasted_iota(i32, (LANE, 2 * TQ), 1)
        for p in range(N_PAIR):
            sl = jnp.where(lane < TQ, _slope2(2 * p), _slope2(2 * p + 1)) + jnp.zeros((LANE, 2 * TQ), f32)
            hi, mid, lo = _split3(sl)
            slf_ref[p] = jnp.where(term == 0, hi, jnp.where(term == 1, mid,
                                   jnp.where(term == 2, lo, 0.0))).astype(bf16)

    qi = qi_ref[0]
    w = wi_ref[...]
    d0 = (lax.broadcasted_iota(i32, (TK, TQ), 1) - lax.broadcasted_iota(i32, (TK, TQ), 0))

    def idx_matmul(u, slot):
        rows = pl.ds(pl.multiple_of(u * 2 * TK, 2 * TK), 2 * TK)
        kx = ki_ref[rows, 0:LANE]
        ky = ki_ref[rows, LANE:2 * LANE]
        il_ref[slot][...] = jnp.dot(jnp.concatenate([kx, kx, kx, ky], axis=1), qi,
                                    preferred_element_type=f32)

    def idx_keys(c, slot, half):
        r0 = chunk_start(c)
        rows = slice(half * TK, (half + 1) * TK)
        sc = jnp.maximum(il_ref[slot][rows, 0:TQ], 0.0) * w[0:1, :]
        for hh in range(1, IDX_HEADS):
            sc = sc + jnp.maximum(il_ref[slot][rows, hh * TQ:(hh + 1) * TQ], 0.0) * w[hh:hh + 1, :]
        b = lax.bitcast_convert_type(sc, i32)
        key = jnp.where(b < 0, -(b & 0x7FFFFFFF), b)
        valid = d0 >= (r0 - i * TQ)
        keys_ref[pl.ds(r0, TK), :] = jnp.where(valid, key, INT_MIN)

    idx_matmul(0, 0)
    idx_matmul(1, 1)

    def idx_body(b, carry):
        c0 = N_STAGE * b
        idx_keys(c0, 0, 0)
        idx_keys(c0 + 1, 0, 1)
        idx_matmul(2 * b + 2, 0)
        idx_keys(c0 + 2, 1, 0)
        idx_keys(c0 + 3, 1, 1)
        idx_matmul(2 * b + 3, 1)
        return carry

    lax.fori_loop(0, nbody - 1, idx_body, 0)
    c_last = N_STAGE * (nbody - 1)
    for j in range(N_STAGE):
        idx_keys(c_last + j, j // 2, j % 2)

    def count_ge(cand):
        def body(b, accs):
            accs = list(accs)
            for part in range(N_STAGE):
                r0 = pl.multiple_of((b * N_STAGE + part) * TK, TK)
                ind = jnp.where(keys_ref[pl.ds(r0, TK), :] >= cand, 1, 0)
                for j, g in enumerate(_row_groups(ind)):
                    accs[j % N_CNT_ACC] = accs[j % N_CNT_ACC] + g
            return tuple(accs)
        accs = lax.fori_loop(0, nbody, body,
                             tuple(jnp.zeros((SUBLANE, TQ), i32) for _ in range(N_CNT_ACC)))
        return jnp.sum(_tree(jnp.add, accs), axis=0, keepdims=True)

    def bit_step(bi, st):
        t, above = st
        cand = t + lax.shift_left(jnp.int32(1), 31 - bi)
        cnt = count_ge(cand)
        up = cnt >= topk
        return jnp.where(up, cand, t), jnp.where(up, above, cnt)

    t, above = lax.fori_loop(0, 32, bit_step,
                             (jnp.full((1, TQ), INT_MIN, i32), jnp.zeros((1, TQ), i32)))
    t = jnp.maximum(t, INT_MIN + 1)
    r_tie = (topk - above).astype(f32)

    tri = jnp.where(lax.broadcasted_iota(i32, (TK, TK), 0) >= lax.broadcasted_iota(i32, (TK, TK), 1),
                    1.0, 0.0).astype(bf16)

    def tie_rank(c, slot):
        e = jnp.where(keys_ref[pl.ds(chunk_start(c), TK), :] == t, 1.0, 0.0).astype(bf16)
        pre_ref[slot][...] = jnp.dot(tri, e, preferred_element_type=f32)

    def mask_out(c, slot, rank):
        r0 = chunk_start(c)
        kk = keys_ref[pl.ds(r0, TK), :]
        pre = pre_ref[slot][...] + rank
        nm = jnp.where(kk > t, 0.0, jnp.where(kk == t, jnp.where(pre <= r_tie, 0.0, NEG_INF), NEG_INF))
        keys_ref[pl.ds(r0, TK), :] = lax.bitcast_convert_type(nm, i32)
        return pre[TK - 1:TK, :]

    pipeline(tie_rank, mask_out, jnp.zeros((1, TQ), f32))

    m_ref[...] = jnp.full(m_ref.shape, NEG_INF, f32)
    acc_ref[...] = jnp.zeros(acc_ref.shape, f32)
    lane2 = lax.broadcasted_iota(i32, (1, 2 * TQ), 1)
    ones_rows = jnp.ones((PV_ROWS - HEAD_DIM, TK), bf16)

    def logits(c, slot):
        rows = pl.ds(chunk_start(c), TK)
        nm = lax.bitcast_convert_type(keys_ref[rows, :], f32)
        nm2 = jnp.concatenate([nm, nm], axis=1)
        for p in range(N_PAIR):
            lhs = jnp.concatenate([k_ref[rows, p * LANE:(p + 1) * LANE], pos_ref[...]], axis=1)
            rhs = jnp.concatenate([qbd_ref[0, p], slf_ref[p]], axis=0)
            sm = jnp.dot(lhs, rhs, preferred_element_type=f32) + nm2
            sm_ref[slot][p] = sm
            cmax_ref[slot][p] = jnp.max(_tree(jnp.maximum, _row_groups(sm)), axis=0, keepdims=True)

    def probs(c, slot):
        r0f = jnp.asarray(c * TK, dtype=f32)
        for p in range(N_PAIR):
            coff = jnp.where(lane2 < TQ, _slope2(2 * p), _slope2(2 * p + 1)) * r0f
            m_old = m_ref[p]
            m_new = jnp.maximum(m_old, cmax_ref[slot][p] + coff)
            m_safe = jnp.where(m_new == NEG_INF, 0.0, m_new)
            alpha_ref[slot][p] = jnp.where(m_old == NEG_INF, 0.0, jnp.exp2(m_old - m_safe))
            p_ref[slot][p] = jnp.exp2((sm_ref[slot][p] - (m_safe - coff)).astype(bf16))
            m_ref[p] = m_new

    def weighted_sum(c, slot):
        for h in range(N_HEADS):
            lanes = slice((h % 2) * TQ, (h % 2 + 1) * TQ)
            lhs = jnp.concatenate([vt_ref[c, h], ones_rows], axis=0)
            pv = jnp.dot(lhs, p_ref[slot][h // 2, :, lanes], preferred_element_type=f32)
            acc_ref[h] = acc_ref[h] * alpha_ref[slot][h // 2, :, lanes] + pv

    logits(0, 0)
    probs(0, 0)
    logits(1, 1)

    def attn_step(c0):
        probs(c0 + 1, 1)
        logits(c0 + 2, 0)
        logits(c0 + 3, 1)
        weighted_sum(c0, 0)
        probs(c0 + 2, 0)
        weighted_sum(c0 + 1, 1)

    def attn_body2(b, carry):
        attn_step(4 * b)
        attn_step(4 * b + 2)
        return carry

    def attn_body1(b, carry):
        attn_step(2 * (nsc - 2))
        return carry

    lax.fori_loop(0, (nsc - 1) >> 1, attn_body2, 0)
    lax.fori_loop(0, (nsc - 1) & 1, attn_body1, 0)
    weighted_sum(nch - 2, 0)
    probs(nch - 1, 1)
    weighted_sum(nch - 1, 1)

    outs = []
    for h in range(N_HEADS):
        a = acc_ref[h]
        outs.append(a[0:HEAD_DIM] / a[HEAD_DIM:HEAD_DIM + 1])
    o_ref[...] = jnp.concatenate(outs, axis=0).T.astype(bf16)


def _attn_call(qbd, qi, wit, k, vt, ki, topk):
    s = k.shape[0]
    nq = s // TQ
    whole = pl.BlockSpec(memory_space=pltpu.VMEM)
    il = [pltpu.VMEM((2 * TK, IDX_HEADS * TQ), f32)] * 2
    pre = [pltpu.VMEM((TK, TQ), f32)] * N_STAGE
    return pl.pallas_call(
        functools.partial(_attn_kernel, topk=topk),
        grid=(nq,),
        in_specs=[pl.BlockSpec((1, N_PAIR, LANE, 2 * TQ), lambda i: (i, 0, 0, 0)),
                  pl.BlockSpec((1, IDX_K, IDX_HEADS * TQ), lambda i: (i, 0, 0)),
                  pl.BlockSpec((SUBLANE, TQ), lambda i: (0, i)),
                  whole, whole, whole],
        out_specs=pl.BlockSpec((TQ, ATTN_WIDTH), lambda i: (i, 0)),
        out_shape=jax.ShapeDtypeStruct((s, ATTN_WIDTH), bf16),
        scratch_shapes=[pltpu.VMEM((s, TQ), i32),
                        pltpu.VMEM((TK, LANE), bf16),
                        pltpu.VMEM((N_PAIR, LANE, 2 * TQ), bf16),
                        *il, *pre,
                        pltpu.VMEM((N_PAIR, TK, 2 * TQ), f32),
                        pltpu.VMEM((N_PAIR, TK, 2 * TQ), f32),
                        pltpu.VMEM((N_PAIR, 1, 2 * TQ), f32),
                        pltpu.VMEM((N_PAIR, 1, 2 * TQ), f32),
                        pltpu.VMEM((N_PAIR, TK, 2 * TQ), bf16),
                        pltpu.VMEM((N_PAIR, TK, 2 * TQ), bf16),
                        pltpu.VMEM((N_PAIR, 1, 2 * TQ), f32),
                        pltpu.VMEM((N_PAIR, 1, 2 * TQ), f32),
                        pltpu.VMEM((N_PAIR, 1, 2 * TQ), f32),
                        pltpu.VMEM((N_HEADS, PV_ROWS, TQ), f32)],
        compiler_params=pltpu.CompilerParams(dimension_semantics=("arbitrary",),
                                             vmem_limit_bytes=VMEM_LIMIT),
        name="attn",
    )(qbd, qi, wit, k, vt, ki)


HALO_POOL = 16


def _mix_kernel(x_ref, attn_ref, u_ref, uh_ref, ga_ref, gp_ref, mod_ref,
                wab_ref, wg_ref, ps_ref, wpb_ref, wo_ref, o_ref, *, tm):
    i = pl.program_id(0)
    y_attn = jnp.dot(attn_ref[...], wab_ref[...], preferred_element_type=f32)

    u = u_ref[...]
    halo = jnp.where(i > 0, uh_ref[...], 0.0)
    a = jnp.concatenate([halo, u], axis=0)
    tpos = (i * tm + lax.broadcasted_iota(i32, (tm, POOL_GROUP_DIM), 0) + 1).astype(f32)
    mixed = []
    for g, wdw in enumerate(POOL_WINDOWS):
        ag = a[:, g * POOL_GROUP_DIM:(g + 1) * POOL_GROUP_DIM]
        ug = ag[HALO_POOL:HALO_POOL + tm]
        ssum = ug
        for j in range(1, wdw):
            ssum = ssum + ag[HALO_POOL - j:HALO_POOL - j + tm]
        pooled = ssum / jnp.minimum(tpos, float(wdw)) - ug
        mixed.append(jnp.dot(pooled.astype(bf16), wg_ref[g], preferred_element_type=f32))
    mixed = jnp.concatenate(mixed, axis=1) * ps_ref[...]
    y_pool = jnp.dot(mixed.astype(bf16), wpb_ref[...], preferred_element_type=f32)

    merged = ga_ref[...].astype(f32) * y_attn + gp_ref[...].astype(f32) * y_pool
    o = jnp.dot(merged.astype(bf16), wo_ref[...], preferred_element_type=f32)
    gate = mod_ref[0:1, 2 * D_MODEL:3 * D_MODEL]
    o_ref[...] = x_ref[...] + gate * o


def _mix_call(x2, attn, u, ga, gp, mod, wab, wg, ps, wpb, wo, tm):
    s = x2.shape[0]
    const2 = lambda i: (0, 0)
    hb = tm // HALO_POOL
    return pl.pallas_call(
        functools.partial(_mix_kernel, tm=tm),
        grid=(s // tm,),
        in_specs=[pl.BlockSpec((tm, D_MODEL), lambda i: (i, 0)),
                  pl.BlockSpec((tm, ATTN_WIDTH), lambda i: (i, 0)),
                  pl.BlockSpec((tm, POOL_WIDTH), lambda i: (i, 0)),
                  pl.BlockSpec((HALO_POOL, POOL_WIDTH), lambda i: (jnp.maximum(i * hb - 1, 0), 0)),
                  pl.BlockSpec((tm, D_MODEL), lambda i: (i, 0)),
                  pl.BlockSpec((tm, D_MODEL), lambda i: (i, 0)),
                  pl.BlockSpec((SUBLANE, N_MOD * D_MODEL), const2),
                  pl.BlockSpec((ATTN_WIDTH, D_MODEL), const2),
                  pl.BlockSpec((len(POOL_WINDOWS), POOL_GROUP_DIM, POOL_GROUP_DIM), lambda i: (0, 0, 0)),
                  pl.BlockSpec((1, POOL_WIDTH), const2),
                  pl.BlockSpec((POOL_WIDTH, D_MODEL), const2),
                  pl.BlockSpec((D_MODEL, D_MODEL), const2)],
        out_specs=pl.BlockSpec((tm, D_MODEL), lambda i: (i, 0)),
        out_shape=jax.ShapeDtypeStruct((s, D_MODEL), f32),
        compiler_params=pltpu.CompilerParams(dimension_semantics=("parallel",),
                                             vmem_limit_bytes=VMEM_LIMIT),
        name="mix",
    )(x2, attn, u, u, ga, gp, mod, wab, wg, ps, wpb, wo)


HALO_CONV = 8


def _ffn_kernel(x_ref, xh_ref, mod_ref, g2_ref, wup_ref, cw_ref, cb_ref, wdn_ref, o_ref, *, tm):
    i = pl.program_id(0)
    shift = mod_ref[0:1, 3 * D_MODEL:4 * D_MODEL]
    scale = mod_ref[0:1, 4 * D_MODEL:5 * D_MODEL]
    gate = mod_ref[0:1, 5 * D_MODEL:6 * D_MODEL]
    g2 = g2_ref[...]
    x = x_ref[...]
    h = _rms_modulate(x, g2, shift, scale)
    hh = jnp.where(i > 0, _rms_modulate(xh_ref[...], g2, shift, scale), 0.0)
    ha = jnp.concatenate([hh, h], axis=0).astype(bf16)
    up = jnp.dot(ha, wup_ref[...], preferred_element_type=f32)
    cw = cw_ref[...]
    y = cb_ref[...] + cw[0:1, :] * up[HALO_CONV - 2:HALO_CONV - 2 + tm]
    y = y + cw[1:2, :] * up[HALO_CONV - 1:HALO_CONV - 1 + tm]
    y = y + cw[2:3, :] * up[HALO_CONV:HALO_CONV + tm]
    a = y[:, 0:D_FF]
    b = y[:, D_FF:2 * D_FF]
    gated = (a * _sigmoid(a)) * b
    o = jnp.dot(gated.astype(bf16), wdn_ref[...], preferred_element_type=f32)
    o_ref[...] = x + gate * o


def _ffn_call(x1, mod, g2, wup, cw, cb, wdn, tm):
    s = x1.shape[0]
    const2 = lambda i: (0, 0)
    hb = tm // HALO_CONV
    return pl.pallas_call(
        functools.partial(_ffn_kernel, tm=tm),
        grid=(s // tm,),
        in_specs=[pl.BlockSpec((tm, D_MODEL), lambda i: (i, 0)),
                  pl.BlockSpec((HALO_CONV, D_MODEL), lambda i: (jnp.maximum(i * hb - 1, 0), 0)),
                  pl.BlockSpec((SUBLANE, N_MOD * D_MODEL), const2),
                  pl.BlockSpec((1, D_MODEL), const2),
                  pl.BlockSpec((D_MODEL, 2 * D_FF), const2, pipeline_mode=pl.Buffered(1)),
                  pl.BlockSpec((3, 2 * D_FF), const2),
                  pl.BlockSpec((1, 2 * D_FF), const2),
                  pl.BlockSpec((D_FF, D_MODEL), const2, pipeline_mode=pl.Buffered(1))],
        out_specs=pl.BlockSpec((tm, D_MODEL), lambda i: (i, 0)),
        out_shape=jax.ShapeDtypeStruct((s, D_MODEL), f32),
        compiler_params=pltpu.CompilerParams(dimension_semantics=("parallel",),
                                             vmem_limit_bytes=VMEM_LIMIT),
        name="ffn",
    )(x1, x1, mod, g2, wup, cw, cb, wdn)


def _pack_w_idx(w):
    return jnp.pad(w[:, IN_IDX0:IN_IDX1], ((0, 0), (0, 3 * LANE - (IN_IDX1 - IN_IDX0))))


def kernel(x, c, w_ada, b_ada, norm1_g, w_in, q_norm_g, k_norm_g, w_attn_br, w_pool_grp,
           pool_scale, w_pool_br, w_out, norm2_g, w_up, conv_w, conv_b, w_down):
    bsz, s, d = x.shape
    assert bsz == 1 and d == D_MODEL and s % (N_STAGE * TK) == 0
    depth = w_ada.shape[0]
    topk = min(TOPK_MAX, s // 4)
    x2 = x.reshape(s, d)
    c8 = jnp.pad(c, ((0, SUBLANE - bsz), (0, 0)))
    for l in range(depth):
        mod = _mod_call(c8, w_ada[l], b_ada[l].reshape(1, -1))
        qbd, k, vt, qi, ki, wit, u, ga, gp = _proj_call(
            x2, mod, norm1_g[l].reshape(1, -1),
            w_in[l][:, 0:IN_IDX0].astype(bf16), w_in[l][:, IN_IDX1:IN_END].astype(bf16),
            jnp.tile(q_norm_g[l], N_HEADS).reshape(1, -1),
            jnp.tile(k_norm_g[l], N_HEADS).reshape(1, -1), _pack_w_idx(w_in[l]), tm=512)
        attn = _attn_call(qbd, qi, wit, k, vt, ki, topk)
        x2 = _mix_call(x2, attn, u, ga, gp, mod, w_attn_br[l].astype(bf16),
                       w_pool_grp[l].astype(bf16), pool_scale[l].reshape(1, -1),
                       w_pool_br[l].astype(bf16), w_out[l].astype(bf16), tm=512)
        x2 = _ffn_call(x2, mod, norm2_g[l].reshape(1, -1), w_up[l].astype(bf16), conv_w[l],
                       conv_b[l].reshape(1, -1), w_down[l].astype(bf16), tm=512)
    return x2.reshape(bsz, s, d)
```

```python
import functools
import math

import jax
import jax.numpy as jnp
from jax import lax
from jax.experimental import pallas as pl
from jax.experimental.pallas import tpu as pltpu

f32 = jnp.float32
bf16 = jnp.bfloat16
i32 = jnp.int32

D_MODEL = 1024
N_HEADS = 8
HEAD_DIM = 64
ATTN_WIDTH = N_HEADS * HEAD_DIM
IDX_HEADS = 4
IDX_DIM = 64
TOPK_MAX = 256
POOL_WINDOWS = (2, 4, 8, 16)
POOL_GROUP_DIM = 128
POOL_WIDTH = 512
D_FF = 2816
EPS = 1e-6
N_MOD = 6

LANE = 128
SUBLANE = 8
VMEM_LIMIT = 58 * 1024 * 1024

LOG2E = 1.4426950408889634
INT_MIN = -2147483648
NEG_INF = float("-inf")

IN_IDX0, IN_IDX1, IN_END = 1536, 1860, 4420
C_Q, C_K, C_V, C_QKV = 0, 512, 1024, 1536
C_U, C_GA, C_GP, C_REST = 0, 512, 1536, 2560

TQ = 128
TK = 256
N_PAIR = N_HEADS // 2
IDX_K = 4 * LANE


def _sigmoid(x):
    return 1.0 / (1.0 + jnp.exp(-x))


def _rms_modulate(x, g, shift, scale):
    y = x * lax.rsqrt(jnp.mean(x * x, axis=-1, keepdims=True) + EPS)
    return (y * g) * (1.0 + scale) + shift


def _mod_kernel(c_ref, w_ref, b_ref, o_ref):
    c = c_ref[...]
    sc = c * _sigmoid(c)
    o_ref[...] = jnp.dot(sc, w_ref[...], precision=lax.Precision.HIGHEST,
                         preferred_element_type=f32) + b_ref[...]


def _mod_call(c8, w_ada, b_ada):
    n = w_ada.shape[1]
    tn = 1024
    return pl.pallas_call(
        _mod_kernel,
        grid=(n // tn,),
        in_specs=[pl.BlockSpec((SUBLANE, D_MODEL), lambda j: (0, 0)),
                  pl.BlockSpec((D_MODEL, tn), lambda j: (0, j)),
                  pl.BlockSpec((1, tn), lambda j: (0, j))],
        out_specs=pl.BlockSpec((SUBLANE, tn), lambda j: (0, j)),
        out_shape=jax.ShapeDtypeStruct((SUBLANE, n), f32),
        name="mod",
    )(c8, w_ada, b_ada)


def _split3(x):
    hi = x.astype(bf16).astype(f32)
    r = x - hi
    mid = r.astype(bf16).astype(f32)
    lo = (r - mid).astype(bf16).astype(f32)
    return hi, mid, lo


def _head_norm(z, g, bd):
    z2 = z * z
    hi = z2.astype(bf16)
    lo = (z2 - hi.astype(f32)).astype(bf16)
    ms = jnp.dot(hi, bd, preferred_element_type=f32) + jnp.dot(lo, bd, preferred_element_type=f32)
    return (z * lax.rsqrt(ms + EPS)) * g


def _proj_kernel(x_ref, mod_ref, g1_ref, wa_ref, wb_ref, qg_ref, kg_ref, widx_ref,
                 qbd_ref, k_ref, vt_ref, qi_ref, ki_ref, wit_ref, u_ref, ga_ref, gp_ref, *, tm):
    x = x_ref[...]
    shift = mod_ref[0:1, 0:D_MODEL]
    scale = mod_ref[0:1, D_MODEL:2 * D_MODEL]
    h = _rms_modulate(x, g1_ref[...], shift, scale)
    hb = h.astype(bf16)
    proj = jnp.dot(hb, wa_ref[...], preferred_element_type=f32)
    rest = jnp.dot(hb, wb_ref[...], preferred_element_type=f32)

    r = lax.broadcasted_iota(i32, (ATTN_WIDTH, ATTN_WIDTH), 0)
    c = lax.broadcasted_iota(i32, (ATTN_WIDTH, ATTN_WIDTH), 1)
    bd = jnp.where((r >> 6) == (c >> 6), 1.0 / HEAD_DIM, 0.0).astype(bf16)

    q = _head_norm(proj[:, C_Q:C_K], qg_ref[...], bd) * (HEAD_DIM ** -0.5 * LOG2E)
    k = _head_norm(proj[:, C_K:C_V], kg_ref[...], bd)
    k_ref[...] = k.astype(bf16)

    low = lax.broadcasted_iota(i32, (TQ, LANE), 1) < HEAD_DIM
    low_tm = lax.broadcasted_iota(i32, (tm, LANE), 1) < HEAD_DIM
    pidx = jnp.dot(h, widx_ref[...], precision=lax.Precision.HIGHEST, preferred_element_type=f32)
    qi = pidx[:, 0:256] * (IDX_DIM ** -0.5)
    up64 = lambda z: pltpu.roll(z, HEAD_DIM, axis=1)
    for g in range(tm // TQ):
        rows = slice(g * TQ, (g + 1) * TQ)
        for p in range(N_PAIR):
            qp = q[rows, p * LANE:(p + 1) * LANE]
            bd_q = jnp.concatenate([jnp.where(low, qp, 0.0), jnp.where(low, 0.0, qp)], axis=0)
            qbd_ref[g, p] = bd_q.T.astype(bf16)
        for hh in range(IDX_HEADS):
            seg = qi[rows, (hh // 2) * LANE:(hh // 2 + 1) * LANE]
            own = low if hh % 2 == 0 else jnp.logical_not(low)
            qh, qm, ql = _split3(jnp.where(own, seg, 0.0))
            both = lambda z: z + up64(z)
            in_low = lambda z: jnp.where(low, both(z), 0.0)
            hcols = slice(hh * TQ, (hh + 1) * TQ)
            for kt, term in enumerate((both(qh), both(qm), in_low(ql), in_low(qh))):
                qi_ref[g, kt * LANE:(kt + 1) * LANE, hcols] = term.T.astype(bf16)

    v = proj[:, C_V:C_QKV]
    vt = v.T.astype(bf16)
    for cc in range(tm // TK):
        for hh in range(N_HEADS):
            vt_ref[cc, hh] = vt[hh * HEAD_DIM:(hh + 1) * HEAD_DIM, cc * TK:(cc + 1) * TK]

    kw = pidx[:, 256:384]
    kh, km, kl = _split3(jnp.where(low_tm, kw, 0.0))
    ki_ref[:, 0:LANE] = (kh + up64(km)).astype(bf16)
    ki_ref[:, LANE:2 * LANE] = kl.astype(bf16)
    wt = (kw * (IDX_HEADS ** -0.5)).T
    wit_ref[...] = wt[HEAD_DIM:HEAD_DIM + SUBLANE, :]
    u_ref[...] = rest[:, C_U:C_GA]
    ga_ref[...] = _sigmoid(rest[:, C_GA:C_GP]).astype(bf16)
    gp_ref[...] = _sigmoid(rest[:, C_GP:C_REST]).astype(bf16)


def _proj_call(x2, mod, g1, w_qkv, w_rest, qg, kg, widx, tm):
    s = x2.shape[0]
    nq = s // TQ
    const = lambda i: (0, 0)
    return pl.pallas_call(
        functools.partial(_proj_kernel, tm=tm),
        grid=(s // tm,),
        in_specs=[pl.BlockSpec((tm, D_MODEL), lambda i: (i, 0)),
                  pl.BlockSpec((SUBLANE, N_MOD * D_MODEL), const),
                  pl.BlockSpec((1, D_MODEL), const),
                  pl.BlockSpec((D_MODEL, C_QKV), const),
                  pl.BlockSpec((D_MODEL, C_REST), const),
                  pl.BlockSpec((1, ATTN_WIDTH), const),
                  pl.BlockSpec((1, ATTN_WIDTH), const),
                  pl.BlockSpec((D_MODEL, 3 * LANE), const)],
        out_specs=[pl.BlockSpec((tm // TQ, N_PAIR, LANE, 2 * TQ), lambda i: (i, 0, 0, 0)),
                   pl.BlockSpec((tm, ATTN_WIDTH), lambda i: (i, 0)),
                   pl.BlockSpec((tm // TK, N_HEADS, HEAD_DIM, TK), lambda i: (i, 0, 0, 0)),
                   pl.BlockSpec((tm // TQ, IDX_K, IDX_HEADS * TQ), lambda i: (i, 0, 0)),
                   pl.BlockSpec((tm, 2 * LANE), lambda i: (i, 0)),
                   pl.BlockSpec((SUBLANE, tm), lambda i: (0, i)),
                   pl.BlockSpec((tm, POOL_WIDTH), lambda i: (i, 0)),
                   pl.BlockSpec((tm, D_MODEL), lambda i: (i, 0)),
                   pl.BlockSpec((tm, D_MODEL), lambda i: (i, 0))],
        out_shape=[jax.ShapeDtypeStruct((nq, N_PAIR, LANE, 2 * TQ), bf16),
                   jax.ShapeDtypeStruct((s, ATTN_WIDTH), bf16),
                   jax.ShapeDtypeStruct((s // TK, N_HEADS, HEAD_DIM, TK), bf16),
                   jax.ShapeDtypeStruct((nq, IDX_K, IDX_HEADS * TQ), bf16),
                   jax.ShapeDtypeStruct((s, 2 * LANE), bf16),
                   jax.ShapeDtypeStruct((SUBLANE, s), f32),
                   jax.ShapeDtypeStruct((s, POOL_WIDTH), f32),
                   jax.ShapeDtypeStruct((s, D_MODEL), bf16),
                   jax.ShapeDtypeStruct((s, D_MODEL), bf16)],
        compiler_params=pltpu.CompilerParams(dimension_semantics=("parallel",),
                                             vmem_limit_bytes=VMEM_LIMIT),
        name="proj",
    )(x2, mod, g1, w_qkv, w_rest, qg, kg, widx)


def _slope2(h):
    return (2.0 ** (-8.0 * (h + 1) / N_HEADS)) * LOG2E


def _tree(op, xs):
    xs = list(xs)
    while len(xs) > 1:
        nxt = [op(xs[a], xs[a + 1]) for a in range(0, len(xs) - 1, 2)]
        if len(xs) % 2:
            nxt.append(xs[-1])
        xs = nxt
    return xs[0]


def _row_groups(x):
    return [x[j * SUBLANE:(j + 1) * SUBLANE] for j in range(x.shape[0] // SUBLANE)]


N_CNT_ACC = 8
N_STAGE = 4
PV_ROWS = HEAD_DIM + 16


def _attn_kernel(qbd_ref, qi_ref, wi_ref, k_ref, vt_ref, ki_ref, o_ref,
                 keys_ref, pos_ref, slf_ref, il0_ref, il1_ref,
                 pre0_ref, pre1_ref, pre2_ref, pre3_ref, sm0_ref, sm1_ref, cmax0_ref, cmax1_ref,
                 p0_ref, p1_ref, alpha0_ref, alpha1_ref, m_ref, acc_ref, *, topk):
    il_ref = (il0_ref, il1_ref)
    pre_ref = (pre0_ref, pre1_ref, pre2_ref, pre3_ref)
    sm_ref = (sm0_ref, sm1_ref)
    cmax_ref = (cmax0_ref, cmax1_ref)
    p_ref = (p0_ref, p1_ref)
    alpha_ref = (alpha0_ref, alpha1_ref)
    i = pl.program_id(0)
    nsc = (i + 4) >> 2
    nch = 2 * nsc
    nbody = (i + 8) >> 3

    def chunk_start(c):
        return pl.multiple_of(c * TK, TK)

    def pipeline(produce, consume, carry):
        for j in range(N_STAGE):
            produce(j, j)

        def body(b, carry):
            c0 = N_STAGE * b
            for j in range(N_STAGE):
                carry = consume(c0 + j, j, carry)
                produce(c0 + N_STAGE + j, j)
            return carry

        carry = lax.fori_loop(0, nbody - 1, body, carry)
        c0 = N_STAGE * (nbody - 1)
        for j in range(N_STAGE):
            carry = consume(c0 + j, j, carry)
        return carry

    @pl.when(i == 0)
    def _():
        col = lax.broadcasted_iota(i32, (TK, LANE), 1)
        row = lax.broadcasted_iota(i32, (TK, LANE), 0).astype(f32)
        pos_ref[...] = jnp.where(col < 3, row, 0.0).astype(bf16)
        term = lax.broadcasted_iota(i32, (LANE, 2 * TQ), 0)
        lane = lax.broadcasted_iota(i32, (LANE, 2 * TQ), 1)
        for p in range(N_PAIR):
            sl = jnp.where(lane < TQ, _slope2(2 * p), _slope2(2 * p + 1)) + jnp.zeros((LANE, 2 * TQ), f32)
            hi, mid, lo = _split3(sl)
            slf_ref[p] = jnp.where(term == 0, hi, jnp.where(term == 1, mid,
                                   jnp.where(term == 2, lo, 0.0))).astype(bf16)

    qi = qi_ref[0]
    w = wi_ref[...]
    d0 = (lax.broadcasted_iota(i32, (TK, TQ), 1) - lax.broadcasted_iota(i32, (TK, TQ), 0))

    def idx_matmul(u, slot):
        rows = pl.ds(pl.multiple_of(u * 2 * TK, 2 * TK), 2 * TK)
        kx = ki_ref[rows, 0:LANE]
        ky = ki_ref[rows, LANE:2 * LANE]
        il_ref[slot][...] = jnp.dot(jnp.concatenate([kx, kx, kx, ky], axis=1), qi,
                                    preferred_element_type=f32)

    def idx_keys(c, slot, half):
        r0 = chunk_start(c)
        rows = slice(half * TK, (half + 1) * TK)
        sc = jnp.maximum(il_ref[slot][rows, 0:TQ], 0.0) * w[0:1, :]
        for hh in range(1, IDX_HEADS):
            sc = sc + jnp.maximum(il_ref[slot][rows, hh * TQ:(hh + 1) * TQ], 0.0) * w[hh:hh + 1, :]
        b = lax.bitcast_convert_type(sc, i32)
        key = jnp.where(b < 0, -(b & 0x7FFFFFFF), b)
        valid = d0 >= (r0 - i * TQ)
        keys_ref[pl.ds(r0, TK), :] = jnp.where(valid, key, INT_MIN)

    idx_matmul(0, 0)
    idx_matmul(1, 1)

    def idx_body(b, carry):
        c0 = N_STAGE * b
        idx_keys(c0, 0, 0)
        idx_keys(c0 + 1, 0, 1)
        idx_matmul(2 * b + 2, 0)
        idx_keys(c0 + 2, 1, 0)
        idx_keys(c0 + 3, 1, 1)
        idx_matmul(2 * b + 3, 1)
        return carry

    lax.fori_loop(0, nbody - 1, idx_body, 0)
    c_last = N_STAGE * (nbody - 1)
    for j in range(N_STAGE):
        idx_keys(c_last + j, j // 2, j % 2)

    def count_ge(cand):
        def body(b, accs):
            accs = list(accs)
            for part in range(N_STAGE):
                r0 = pl.multiple_of((b * N_STAGE + part) * TK, TK)
                ind = jnp.where(keys_ref[pl.ds(r0, TK), :] >= cand, 1, 0)
                for j, g in enumerate(_row_groups(ind)):
                    accs[j % N_CNT_ACC] = accs[j % N_CNT_ACC] + g
            return tuple(accs)
        accs = lax.fori_loop(0, nbody, body,
                             tuple(jnp.zeros((SUBLANE, TQ), i32) for _ in range(N_CNT_ACC)))
        return jnp.sum(_tree(jnp.add, accs), axis=0, keepdims=True)

    def bit_step(bi, st):
        t, above = st
        cand = t + lax.shift_left(jnp.int32(1), 31 - bi)
        cnt = count_ge(cand)
        up = cnt >= topk
        return jnp.where(up, cand, t), jnp.where(up, above, cnt)

    t, above = lax.fori_loop(0, 32, bit_step,
                             (jnp.full((1, TQ), INT_MIN, i32), jnp.zeros((1, TQ), i32)))
    t = jnp.maximum(t, INT_MIN + 1)
    r_tie = (topk - above).astype(f32)

    tri = jnp.where(lax.broadcasted_iota(i32, (TK, TK), 0) >= lax.broadcasted_iota(i32, (TK, TK), 1),
                    1.0, 0.0).astype(bf16)

    def tie_rank(c, slot):
        e = jnp.where(keys_ref[pl.ds(chunk_start(c), TK), :] == t, 1.0, 0.0).astype(bf16)
        pre_ref[slot][...] = jnp.dot(tri, e, preferred_element_type=f32)

    def mask_out(c, slot, rank):
        r0 = chunk_start(c)
        kk = keys_ref[pl.ds(r0, TK), :]
        pre = pre_ref[slot][...] + rank
        nm = jnp.where(kk > t, 0.0, jnp.where(kk == t, jnp.where(pre <= r_tie, 0.0, NEG_INF), NEG_INF))
        keys_ref[pl.ds(r0, TK), :] = lax.bitcast_convert_type(nm, i32)
        return pre[TK - 1:TK, :]

    pipeline(tie_rank, mask_out, jnp.zeros((1, TQ), f32))

    m_ref[...] = jnp.full(m_ref.shape, NEG_INF, f32)
    acc_ref[...] = jnp.zeros(acc_ref.shape, f32)
    lane2 = lax.broadcasted_iota(i32, (1, 2 * TQ), 1)
    ones_rows = jnp.ones((PV_ROWS - HEAD_DIM, TK), bf16)

    def logits(c, slot):
        rows = pl.ds(chunk_start(c), TK)
        nm = lax.bitcast_convert_type(keys_ref[rows, :], f32)
        nm2 = jnp.concatenate([nm, nm], axis=1)
        for p in range(N_PAIR):
            lhs = jnp.concatenate([k_ref[rows, p * LANE:(p + 1) * LANE], pos_ref[...]], axis=1)
            rhs = jnp.concatenate([qbd_ref[0, p], slf_ref[p]], axis=0)
            sm = jnp.dot(lhs, rhs, preferred_element_type=f32) + nm2
            sm_ref[slot][p] = sm
            cmax_ref[slot][p] = jnp.max(_tree(jnp.maximum, _row_groups(sm)), axis=0, keepdims=True)

    def probs(c, slot):
        r0f = jnp.asarray(c * TK, dtype=f32)
        for p in range(N_PAIR):
            coff = jnp.where(lane2 < TQ, _slope2(2 * p), _slope2(2 * p + 1)) * r0f
            m_old = m_ref[p]
            m_new = jnp.maximum(m_old, cmax_ref[slot][p] + coff)
            m_safe = jnp.where(m_new == NEG_INF, 0.0, m_new)
            alpha_ref[slot][p] = jnp.where(m_old == NEG_INF, 0.0, jnp.exp2(m_old - m_safe))
            p_ref[slot][p] = jnp.exp2((sm_ref[slot][p] - (m_safe - coff)).astype(bf16))
            m_ref[p] = m_new

    def weighted_sum(c, slot):
        for h in range(N_HEADS):
            lanes = slice((h % 2) * TQ, (h % 2 + 1) * TQ)
            lhs = jnp.concatenate([vt_ref[c, h], ones_rows], axis=0)
            pv = jnp.dot(lhs, p_ref[slot][h // 2, :, lanes], preferred_element_type=f32)
            acc_ref[h] = acc_ref[h] * alpha_ref[slot][h // 2, :, lanes] + pv

    logits(0, 0)
    probs(0, 0)
    logits(1, 1)

    def attn_step(c0):
        probs(c0 + 1, 1)
        logits(c0 + 2, 0)
        logits(c0 + 3, 1)
        weighted_sum(c0, 0)
        probs(c0 + 2, 0)
        weighted_sum(c0 + 1, 1)

    n_quad = (nsc - 1) >> 2

    def attn_body4(b, carry):
        for j in range(4):
            attn_step(8 * b + 2 * j)
        return carry

    def attn_body1(j, carry):
        attn_step(8 * n_quad + 2 * j)
        return carry

    lax.fori_loop(0, n_quad, attn_body4, 0)
    lax.fori_loop(0, (nsc - 1) & 3, attn_body1, 0)
    weighted_sum(nch - 2, 0)
    probs(nch - 1, 1)
    weighted_sum(nch - 1, 1)

    outs = []
    for h in range(N_HEADS):
        a = acc_ref[h]
        outs.append(a[0:HEAD_DIM] / a[HEAD_DIM:HEAD_DIM + 1])
    o_ref[...] = jnp.concatenate(outs, axis=0).T.astype(bf16)


def _attn_call(qbd, qi, wit, k, vt, ki, topk):
    s = k.shape[0]
    nq = s // TQ
    whole = pl.BlockSpec(memory_space=pltpu.VMEM)
    il = [pltpu.VMEM((2 * TK, IDX_HEADS * TQ), f32)] * 2
    pre = [pltpu.VMEM((TK, TQ), f32)] * N_STAGE
    return pl.pallas_call(
        functools.partial(_attn_kernel, topk=topk),
        grid=(nq,),
        in_specs=[pl.BlockSpec((1, N_PAIR, LANE, 2 * TQ), lambda i: (i, 0, 0, 0)),
                  pl.BlockSpec((1, IDX_K, IDX_HEADS * TQ), lambda i: (i, 0, 0)),
                  pl.BlockSpec((SUBLANE, TQ), lambda i: (0, i)),
                  whole, whole, whole],
        out_specs=pl.BlockSpec((TQ, ATTN_WIDTH), lambda i: (i, 0)),
        out_shape=jax.ShapeDtypeStruct((s, ATTN_WIDTH), bf16),
        scratch_shapes=[pltpu.VMEM((s, TQ), i32),
                        pltpu.VMEM((TK, LANE), bf16),
                        pltpu.VMEM((N_PAIR, LANE, 2 * TQ), bf16),
                        *il, *pre,
                        pltpu.VMEM((N_PAIR, TK, 2 * TQ), f32),
                        pltpu.VMEM((N_PAIR, TK, 2 * TQ), f32),
                        pltpu.VMEM((N_PAIR, 1, 2 * TQ), f32),
                        pltpu.VMEM((N_PAIR, 1, 2 * TQ), f32),
                        pltpu.VMEM((N_PAIR, TK, 2 * TQ), bf16),
                        pltpu.VMEM((N_PAIR, TK, 2 * TQ), bf16),
                        pltpu.VMEM((N_PAIR, 1, 2 * TQ), f32),
                        pltpu.VMEM((N_PAIR, 1, 2 * TQ), f32),
                        pltpu.VMEM((N_PAIR, 1, 2 * TQ), f32),
                        pltpu.VMEM((N_HEADS, PV_ROWS, TQ), f32)],
        compiler_params=pltpu.CompilerParams(dimension_semantics=("arbitrary",),
                                             vmem_limit_bytes=VMEM_LIMIT),
        name="attn",
    )(qbd, qi, wit, k, vt, ki)


HALO_POOL = 16


def _mix_kernel(x_ref, attn_ref, u_ref, uh_ref, ga_ref, gp_ref, mod_ref,
                wab_ref, wg_ref, ps_ref, wpb_ref, wo_ref, o_ref, *, tm):
    i = pl.program_id(0)
    y_attn = jnp.dot(attn_ref[...], wab_ref[...], preferred_element_type=f32)

    u = u_ref[...]
    halo = jnp.where(i > 0, uh_ref[...], 0.0)
    a = jnp.concatenate([halo, u], axis=0)
    tpos = (i * tm + lax.broadcasted_iota(i32, (tm, POOL_GROUP_DIM), 0) + 1).astype(f32)
    mixed = []
    for g, wdw in enumerate(POOL_WINDOWS):
        ag = a[:, g * POOL_GROUP_DIM:(g + 1) * POOL_GROUP_DIM]
        ug = ag[HALO_POOL:HALO_POOL + tm]
        ssum = ug
        for j in range(1, wdw):
            ssum = ssum + ag[HALO_POOL - j:HALO_POOL - j + tm]
        pooled = ssum / jnp.minimum(tpos, float(wdw)) - ug
        mixed.append(jnp.dot(pooled.astype(bf16), wg_ref[g], preferred_element_type=f32))
    mixed = jnp.concatenate(mixed, axis=1) * ps_ref[...]
    y_pool = jnp.dot(mixed.astype(bf16), wpb_ref[...], preferred_element_type=f32)

    merged = ga_ref[...].astype(f32) * y_attn + gp_ref[...].astype(f32) * y_pool
    o = jnp.dot(merged.astype(bf16), wo_ref[...], preferred_element_type=f32)
    gate = mod_ref[0:1, 2 * D_MODEL:3 * D_MODEL]
    o_ref[...] = x_ref[...] + gate * o


def _mix_call(x2, attn, u, ga, gp, mod, wab, wg, ps, wpb, wo, tm):
    s = x2.shape[0]
    const2 = lambda i: (0, 0)
    hb = tm // HALO_POOL
    return pl.pallas_call(
        functools.partial(_mix_kernel, tm=tm),
        grid=(s // tm,),
        in_specs=[pl.BlockSpec((tm, D_MODEL), lambda i: (i, 0)),
                  pl.BlockSpec((tm, ATTN_WIDTH), lambda i: (i, 0)),
                  pl.BlockSpec((tm, POOL_WIDTH), lambda i: (i, 0)),
                  pl.BlockSpec((HALO_POOL, POOL_WIDTH), lambda i: (jnp.maximum(i * hb - 1, 0), 0)),
                  pl.BlockSpec((tm, D_MODEL), lambda i: (i, 0)),
                  pl.BlockSpec((tm, D_MODEL), lambda i: (i, 0)),
                  pl.BlockSpec((SUBLANE, N_MOD * D_MODEL), const2),
                  pl.BlockSpec((ATTN_WIDTH, D_MODEL), const2),
                  pl.BlockSpec((len(POOL_WINDOWS), POOL_GROUP_DIM, POOL_GROUP_DIM), lambda i: (0, 0, 0)),
                  pl.BlockSpec((1, POOL_WIDTH), const2),
                  pl.BlockSpec((POOL_WIDTH, D_MODEL), const2),
                  pl.BlockSpec((D_MODEL, D_MODEL), const2)],
        out_specs=pl.BlockSpec((tm, D_MODEL), lambda i: (i, 0)),
        out_shape=jax.ShapeDtypeStruct((s, D_MODEL), f32),
        compiler_params=pltpu.CompilerParams(dimension_semantics=("parallel",),
                                             vmem_limit_bytes=VMEM_LIMIT),
        name="mix",
    )(x2, attn, u, u, ga, gp, mod, wab, wg, ps, wpb, wo)


HALO_CONV = 8


def _ffn_kernel(x_ref, xh_ref, mod_ref, g2_ref, wup_ref, cw_ref, cb_ref, wdn_ref, o_ref, *, tm):
    i = pl.program_id(0)
    shift = mod_ref[0:1, 3 * D_MODEL:4 * D_MODEL]
    scale = mod_ref[0:1, 4 * D_MODEL:5 * D_MODEL]
    gate = mod_ref[0:1, 5 * D_MODEL:6 * D_MODEL]
    g2 = g2_ref[...]
    x = x_ref[...]
    h = _rms_modulate(x, g2, shift, scale)
    hh = jnp.where(i > 0, _rms_modulate(xh_ref[...], g2, shift, scale), 0.0)
    ha = jnp.concatenate([hh, h], axis=0).astype(bf16)
    up = jnp.dot(ha, wup_ref[...], preferred_element_type=f32)
    cw = cw_ref[...]
    y = cb_ref[...] + cw[0:1, :] * up[HALO_CONV - 2:HALO_CONV - 2 + tm]
    y = y + cw[1:2, :] * up[HALO_CONV - 1:HALO_CONV - 1 + tm]
    y = y + cw[2:3, :] * up[HALO_CONV:HALO_CONV + tm]
    a = y[:, 0:D_FF]
    b = y[:, D_FF:2 * D_FF]
    gated = (a * _sigmoid(a)) * b
    o = jnp.dot(gated.astype(bf16), wdn_ref[...], preferred_element_type=f32)
    o_ref[...] = x + gate * o


def _ffn_call(x1, mod, g2, wup, cw, cb, wdn, tm):
    s = x1.shape[0]
    const2 = lambda i: (0, 0)
    hb = tm // HALO_CONV
    return pl.pallas_call(
        functools.partial(_ffn_kernel, tm=tm),
        grid=(s // tm,),
        in_specs=[pl.BlockSpec((tm, D_MODEL), lambda i: (i, 0)),
                  pl.BlockSpec((HALO_CONV, D_MODEL), lambda i: (jnp.maximum(i * hb - 1, 0), 0)),
                  pl.BlockSpec((SUBLANE, N_MOD * D_MODEL), const2),
                  pl.BlockSpec((1, D_MODEL), const2),
                  pl.BlockSpec((D_MODEL, 2 * D_FF), const2, pipeline_mode=pl.Buffered(1)),
                  pl.BlockSpec((3, 2 * D_FF), const2),
                  pl.BlockSpec((1, 2 * D_FF), const2),
                  pl.BlockSpec((D_FF, D_MODEL), const2, pipeline_mode=pl.Buffered(1))],
        out_specs=pl.BlockSpec((tm, D_MODEL), lambda i: (i, 0)),
        out_shape=jax.ShapeDtypeStruct((s, D_MODEL), f32),
        compiler_params=pltpu.CompilerParams(dimension_semantics=("parallel",),
                                             vmem_limit_bytes=VMEM_LIMIT),
        name="ffn",
    )(x1, x1, mod, g2, wup, cw, cb, wdn)


def _pack_w_idx(w):
    return jnp.pad(w[:, IN_IDX0:IN_IDX1], ((0, 0), (0, 3 * LANE - (IN_IDX1 - IN_IDX0))))


def kernel(x, c, w_ada, b_ada, norm1_g, w_in, q_norm_g, k_norm_g, w_attn_br, w_pool_grp,
           pool_scale, w_pool_br, w_out, norm2_g, w_up, conv_w, conv_b, w_down):
    bsz, s, d = x.shape
    assert bsz == 1 and d == D_MODEL and s % (N_STAGE * TK) == 0
    depth = w_ada.shape[0]
    topk = min(TOPK_MAX, s // 4)
    x2 = x.reshape(s, d)
    c8 = jnp.pad(c, ((0, SUBLANE - bsz), (0, 0)))
    for l in range(depth):
        mod = _mod_call(c8, w_ada[l], b_ada[l].reshape(1, -1))
        qbd, k, vt, qi, ki, wit, u, ga, gp = _proj_call(
            x2, mod, norm1_g[l].reshape(1, -1),
            w_in[l][:, 0:IN_IDX0].astype(bf16), w_in[l][:, IN_IDX1:IN_END].astype(bf16),
            jnp.tile(q_norm_g[l], N_HEADS).reshape(1, -1),
            jnp.tile(k_norm_g[l], N_HEADS).reshape(1, -1), _pack_w_idx(w_in[l]), tm=512)
        attn = _attn_call(qbd, qi, wit, k, vt, ki, topk)
        x2 = _mix_call(x2, attn, u, ga, gp, mod, w_attn_br[l].astype(bf16),
                       w_pool_grp[l].astype(bf16), pool_scale[l].reshape(1, -1),
                       w_pool_br[l].astype(bf16), w_out[l].astype(bf16), tm=512)
        x2 = _ffn_call(x2, mod, norm2_g[l].reshape(1, -1), w_up[l].astype(bf16), conv_w[l],
                       conv_b[l].reshape(1, -1), w_down[l].astype(bf16), tm=512)
    return x2.reshape(bsz, s, d)
```

```python
import functools
import math

import jax
import jax.numpy as jnp
from jax import lax
from jax.experimental import pallas as pl
from jax.experimental.pallas import tpu as pltpu

f32 = jnp.float32
bf16 = jnp.bfloat16
i32 = jnp.int32

D_MODEL = 1024
N_HEADS = 8
HEAD_DIM = 64
ATTN_WIDTH = N_HEADS * HEAD_DIM
IDX_HEADS = 4
IDX_DIM = 64
TOPK_MAX = 256
POOL_WINDOWS = (2, 4, 8, 16)
POOL_GROUP_DIM = 128
POOL_WIDTH = 512
D_FF = 2816
EPS = 1e-6
N_MOD = 6

LANE = 128
SUBLANE = 8
VMEM_LIMIT = 58 * 1024 * 1024

LOG2E = 1.4426950408889634
INT_MIN = -2147483648
NEG_INF = float("-inf")

IN_IDX0, IN_IDX1, IN_END = 1536, 1860, 4420
C_Q, C_K, C_V, C_QKV = 0, 512, 1024, 1536
C_U, C_GA, C_GP, C_REST = 0, 512, 1536, 2560

TQ = 128
TK = 256
N_PAIR = N_HEADS // 2
IDX_K = 4 * LANE


def _sigmoid(x):
    return 1.0 / (1.0 + jnp.exp(-x))


def _rms_modulate(x, g, shift, scale):
    y = x * lax.rsqrt(jnp.mean(x * x, axis=-1, keepdims=True) + EPS)
    return (y * g) * (1.0 + scale) + shift


def _mod_kernel(c_ref, w_ref, b_ref, o_ref):
    c = c_ref[...]
    sc = c * _sigmoid(c)
    o_ref[...] = jnp.dot(sc, w_ref[...], precision=lax.Precision.HIGHEST,
                         preferred_element_type=f32) + b_ref[...]


def _mod_call(c8, w_ada, b_ada):
    n = w_ada.shape[1]
    tn = 1024
    return pl.pallas_call(
        _mod_kernel,
        grid=(n // tn,),
        in_specs=[pl.BlockSpec((SUBLANE, D_MODEL), lambda j: (0, 0)),
                  pl.BlockSpec((D_MODEL, tn), lambda j: (0, j)),
                  pl.BlockSpec((1, tn), lambda j: (0, j))],
        out_specs=pl.BlockSpec((SUBLANE, tn), lambda j: (0, j)),
        out_shape=jax.ShapeDtypeStruct((SUBLANE, n), f32),
        name="mod",
    )(c8, w_ada, b_ada)


def _split3(x):
    hi = x.astype(bf16).astype(f32)
    r = x - hi
    mid = r.astype(bf16).astype(f32)
    lo = (r - mid).astype(bf16).astype(f32)
    return hi, mid, lo


def _head_norm(z, g, bd):
    z2 = z * z
    hi = z2.astype(bf16)
    lo = (z2 - hi.astype(f32)).astype(bf16)
    ms = jnp.dot(hi, bd, preferred_element_type=f32) + jnp.dot(lo, bd, preferred_element_type=f32)
    return (z * lax.rsqrt(ms + EPS)) * g


def _proj_kernel(x_ref, mod_ref, g1_ref, wa_ref, wb_ref, qg_ref, kg_ref, widx_ref,
                 qbd_ref, k_ref, vt_ref, qi_ref, ki_ref, wit_ref, u_ref, ga_ref, gp_ref, *, tm):
    x = x_ref[...]
    shift = mod_ref[0:1, 0:D_MODEL]
    scale = mod_ref[0:1, D_MODEL:2 * D_MODEL]
    h = _rms_modulate(x, g1_ref[...], shift, scale)
    hb = h.astype(bf16)
    proj = jnp.dot(hb, wa_ref[...], preferred_element_type=f32)
    rest = jnp.dot(hb, wb_ref[...], preferred_element_type=f32)

    r = lax.broadcasted_iota(i32, (ATTN_WIDTH, ATTN_WIDTH), 0)
    c = lax.broadcasted_iota(i32, (ATTN_WIDTH, ATTN_WIDTH), 1)
    bd = jnp.where((r >> 6) == (c >> 6), 1.0 / HEAD_DIM, 0.0).astype(bf16)

    q = _head_norm(proj[:, C_Q:C_K], qg_ref[...], bd) * (HEAD_DIM ** -0.5 * LOG2E)
    k = _head_norm(proj[:, C_K:C_V], kg_ref[...], bd)
    k_ref[...] = k.astype(bf16)

    low = lax.broadcasted_iota(i32, (TQ, LANE), 1) < HEAD_DIM
    low_tm = lax.broadcasted_iota(i32, (tm, LANE), 1) < HEAD_DIM
    pidx = jnp.dot(h, widx_ref[...], precision=lax.Precision.HIGHEST, preferred_element_type=f32)
    qi = pidx[:, 0:256] * (IDX_DIM ** -0.5)
    up64 = lambda z: pltpu.roll(z, HEAD_DIM, axis=1)
    for g in range(tm // TQ):
        rows = slice(g * TQ, (g + 1) * TQ)
        for p in range(N_PAIR):
            qp = q[rows, p * LANE:(p + 1) * LANE]
            bd_q = jnp.concatenate([jnp.where(low, qp, 0.0), jnp.where(low, 0.0, qp)], axis=0)
            qbd_ref[g, p] = bd_q.T.astype(bf16)
        for hh in range(IDX_HEADS):
            seg = qi[rows, (hh // 2) * LANE:(hh // 2 + 1) * LANE]
            own = low if hh % 2 == 0 else jnp.logical_not(low)
            qh, qm, ql = _split3(jnp.where(own, seg, 0.0))
            both = lambda z: z + up64(z)
            in_low = lambda z: jnp.where(low, both(z), 0.0)
            hcols = slice(hh * TQ, (hh + 1) * TQ)
            for kt, term in enumerate((both(qh), both(qm), in_low(ql), in_low(qh))):
                qi_ref[g, kt * LANE:(kt + 1) * LANE, hcols] = term.T.astype(bf16)

    v = proj[:, C_V:C_QKV]
    vt = v.T.astype(bf16)
    for cc in range(tm // TK):
        for hh in range(N_HEADS):
            vt_ref[cc, hh] = vt[hh * HEAD_DIM:(hh + 1) * HEAD_DIM, cc * TK:(cc + 1) * TK]

    kw = pidx[:, 256:384]
    kh, km, kl = _split3(jnp.where(low_tm, kw, 0.0))
    ki_ref[:, 0:LANE] = (kh + up64(km)).astype(bf16)
    ki_ref[:, LANE:2 * LANE] = kl.astype(bf16)
    wt = (kw * (IDX_HEADS ** -0.5)).T
    wit_ref[...] = wt[HEAD_DIM:HEAD_DIM + SUBLANE, :]
    u_ref[...] = rest[:, C_U:C_GA]
    ga_ref[...] = _sigmoid(rest[:, C_GA:C_GP]).astype(bf16)
    gp_ref[...] = _sigmoid(rest[:, C_GP:C_REST]).astype(bf16)


def _proj_call(x2, mod, g1, w_qkv, w_rest, qg, kg, widx, tm):
    s = x2.shape[0]
    nq = s // TQ
    const = lambda i: (0, 0)
    return pl.pallas_call(
        functools.partial(_proj_kernel, tm=tm),
        grid=(s // tm,),
        in_specs=[pl.BlockSpec((tm, D_MODEL), lambda i: (i, 0)),
                  pl.BlockSpec((SUBLANE, N_MOD * D_MODEL), const),
                  pl.BlockSpec((1, D_MODEL), const),
                  pl.BlockSpec((D_MODEL, C_QKV), const),
                  pl.BlockSpec((D_MODEL, C_REST), const),
                  pl.BlockSpec((1, ATTN_WIDTH), const),
                  pl.BlockSpec((1, ATTN_WIDTH), const),
                  pl.BlockSpec((D_MODEL, 3 * LANE), const)],
        out_specs=[pl.BlockSpec((tm // TQ, N_PAIR, LANE, 2 * TQ), lambda i: (i, 0, 0, 0)),
                   pl.BlockSpec((tm, ATTN_WIDTH), lambda i: (i, 0)),
                   pl.BlockSpec((tm // TK, N_HEADS, HEAD_DIM, TK), lambda i: (i, 0, 0, 0)),
                   pl.BlockSpec((tm // TQ, IDX_K, IDX_HEADS * TQ), lambda i: (i, 0, 0)),
                   pl.BlockSpec((tm, 2 * LANE), lambda i: (i, 0)),
                   pl.BlockSpec((SUBLANE, tm), lambda i: (0, i)),
                   pl.BlockSpec((tm, POOL_WIDTH), lambda i: (i, 0)),
                   pl.BlockSpec((tm, D_MODEL), lambda i: (i, 0)),
                   pl.BlockSpec((tm, D_MODEL), lambda i: (i, 0))],
        out_shape=[jax.ShapeDtypeStruct((nq, N_PAIR, LANE, 2 * TQ), bf16),
                   jax.ShapeDtypeStruct((s, ATTN_WIDTH), bf16),
                   jax.ShapeDtypeStruct((s // TK, N_HEADS, HEAD_DIM, TK), bf16),
                   jax.ShapeDtypeStruct((nq, IDX_K, IDX_HEADS * TQ), bf16),
                   jax.ShapeDtypeStruct((s, 2 * LANE), bf16),
                   jax.ShapeDtypeStruct((SUBLANE, s), f32),
                   jax.ShapeDtypeStruct((s, POOL_WIDTH), f32),
                   jax.ShapeDtypeStruct((s, D_MODEL), bf16),
                   jax.ShapeDtypeStruct((s, D_MODEL), bf16)],
        compiler_params=pltpu.CompilerParams(dimension_semantics=("parallel",),
                                             vmem_limit_bytes=VMEM_LIMIT),
        name="proj",
    )(x2, mod, g1, w_qkv, w_rest, qg, kg, widx)


def _slope2(h):
    return (2.0 ** (-8.0 * (h + 1) / N_HEADS)) * LOG2E


def _tree(op, xs):
    xs = list(xs)
    while len(xs) > 1:
        nxt = [op(xs[a], xs[a + 1]) for a in range(0, len(xs) - 1, 2)]
        if len(xs) % 2:
            nxt.append(xs[-1])
        xs = nxt
    return xs[0]


def _row_groups(x):
    return [x[j * SUBLANE:(j + 1) * SUBLANE] for j in range(x.shape[0] // SUBLANE)]


N_CNT_ACC = 8
N_STAGE = 4
PV_ROWS = HEAD_DIM + 16


def _attn_kernel(qbd_ref, qi_ref, wi_ref, k_ref, vt_ref, ki_ref, o_ref,
                 keys_ref, pos_ref, slf_ref, il0_ref, il1_ref,
                 pre0_ref, pre1_ref, pre2_ref, pre3_ref, sm0_ref, sm1_ref, cmax0_ref, cmax1_ref,
                 p0_ref, p1_ref, alpha0_ref, alpha1_ref, m_ref, acc_ref, *, topk):
    il_ref = (il0_ref, il1_ref)
    pre_ref = (pre0_ref, pre1_ref, pre2_ref, pre3_ref)
    sm_ref = (sm0_ref, sm1_ref)
    cmax_ref = (cmax0_ref, cmax1_ref)
    p_ref = (p0_ref, p1_ref)
    alpha_ref = (alpha0_ref, alpha1_ref)
    i = pl.program_id(0)
    nsc = (i + 4) >> 2
    nch = 2 * nsc
    nbody = (i + 8) >> 3
    n_causal = (i + 2) >> 1
    n_full, n_tail = n_causal >> 2, n_causal & 3

    def chunk_start(c):
        return pl.multiple_of(c * TK, TK)

    def pipeline(produce, consume, carry):
        for j in range(N_STAGE):
            produce(j, j)

        def body(b, carry):
            c0 = N_STAGE * b
            for j in range(N_STAGE):
                carry = consume(c0 + j, j, carry)
                produce(c0 + N_STAGE + j, j)
            return carry

        carry = lax.fori_loop(0, nbody - 1, body, carry)
        c0 = N_STAGE * (nbody - 1)
        for j in range(N_STAGE):
            carry = consume(c0 + j, j, carry)
        return carry

    @pl.when(i == 0)
    def _():
        col = lax.broadcasted_iota(i32, (TK, LANE), 1)
        row = lax.broadcasted_iota(i32, (TK, LANE), 0).astype(f32)
        pos_ref[...] = jnp.where(col < 3, row, 0.0).astype(bf16)
        term = lax.broadcasted_iota(i32, (LANE, 2 * TQ), 0)
        lane = lax.broadcasted_iota(i32, (LANE, 2 * TQ), 1)
        for p in range(N_PAIR):
            sl = jnp.where(lane < TQ, _slope2(2 * p), _slope2(2 * p + 1)) + jnp.zeros((LANE, 2 * TQ), f32)
            hi, mid, lo = _split3(sl)
            slf_ref[p] = jnp.where(term == 0, hi, jnp.where(term == 1, mid,
                                   jnp.where(term == 2, lo, 0.0))).astype(bf16)

    qi = qi_ref[0]
    w = wi_ref[...]
    d0 = (lax.broadcasted_iota(i32, (TK, TQ), 1) - lax.broadcasted_iota(i32, (TK, TQ), 0))

    def idx_matmul(u, slot):
        rows = pl.ds(pl.multiple_of(u * 2 * TK, 2 * TK), 2 * TK)
        kx = ki_ref[rows, 0:LANE]
        ky = ki_ref[rows, LANE:2 * LANE]
        il_ref[slot][...] = jnp.dot(jnp.concatenate([kx, kx, kx, ky], axis=1), qi,
                                    preferred_element_type=f32)

    def idx_keys(c, slot, half):
        r0 = chunk_start(c)
        rows = slice(half * TK, (half + 1) * TK)
        sc = jnp.maximum(il_ref[slot][rows, 0:TQ], 0.0) * w[0:1, :]
        for hh in range(1, IDX_HEADS):
            sc = sc + jnp.maximum(il_ref[slot][rows, hh * TQ:(hh + 1) * TQ], 0.0) * w[hh:hh + 1, :]
        b = lax.bitcast_convert_type(sc, i32)
        key = jnp.where(b < 0, -(b & 0x7FFFFFFF), b)
        valid = d0 >= (r0 - i * TQ)
        keys_ref[pl.ds(r0, TK), :] = jnp.where(valid, key, INT_MIN)

    idx_matmul(0, 0)
    idx_matmul(1, 1)

    def idx_body(b, carry):
        c0 = N_STAGE * b
        idx_keys(c0, 0, 0)
        idx_keys(c0 + 1, 0, 1)
        idx_matmul(2 * b + 2, 0)
        idx_keys(c0 + 2, 1, 0)
        idx_keys(c0 + 3, 1, 1)
        idx_matmul(2 * b + 3, 1)
        return carry

    lax.fori_loop(0, nbody - 1, idx_body, 0)
    c_last = N_STAGE * (nbody - 1)
    for j in range(N_STAGE):
        idx_keys(c_last + j, j // 2, j % 2)

    def count_ge(cand):
        def add_chunk(c, accs):
            ind = jnp.where(keys_ref[pl.ds(chunk_start(c), TK), :] >= cand, 1, 0)
            for j, g in enumerate(_row_groups(ind)):
                accs[j % N_CNT_ACC] = accs[j % N_CNT_ACC] + g

        def body(b, accs):
            accs = list(accs)
            for part in range(N_STAGE):
                add_chunk(b * N_STAGE + part, accs)
            return tuple(accs)

        def tail(j, accs):
            accs = list(accs)
            add_chunk(n_full * N_STAGE + j, accs)
            return tuple(accs)

        accs = lax.fori_loop(0, n_full, body,
                             tuple(jnp.zeros((SUBLANE, TQ), i32) for _ in range(N_CNT_ACC)))
        accs = lax.fori_loop(0, n_tail, tail, accs)
        return jnp.sum(_tree(jnp.add, accs), axis=0, keepdims=True)

    def bit_step(bi, st):
        t, above = st
        cand = t + lax.shift_left(jnp.int32(1), 31 - bi)
        cnt = count_ge(cand)
        up = cnt >= topk
        return jnp.where(up, cand, t), jnp.where(up, above, cnt)

    t, above = lax.fori_loop(0, 32, bit_step,
                             (jnp.full((1, TQ), INT_MIN, i32), jnp.zeros((1, TQ), i32)))
    t = jnp.maximum(t, INT_MIN + 1)
    r_tie = (topk - above).astype(f32)

    tri = jnp.where(lax.broadcasted_iota(i32, (TK, TK), 0) >= lax.broadcasted_iota(i32, (TK, TK), 1),
                    1.0, 0.0).astype(bf16)

    def tie_rank(c, slot):
        e = jnp.where(keys_ref[pl.ds(chunk_start(c), TK), :] == t, 1.0, 0.0).astype(bf16)
        pre_ref[slot][...] = jnp.dot(tri, e, preferred_element_type=f32)

    def mask_out(c, slot, rank):
        r0 = chunk_start(c)
        kk = keys_ref[pl.ds(r0, TK), :]
        pre = pre_ref[slot][...] + rank
        nm = jnp.where(kk > t, 0.0, jnp.where(kk == t, jnp.where(pre <= r_tie, 0.0, NEG_INF), NEG_INF))
        keys_ref[pl.ds(r0, TK), :] = lax.bitcast_convert_type(nm, i32)
        return pre[TK - 1:TK, :]

    pipeline(tie_rank, mask_out, jnp.zeros((1, TQ), f32))

    m_ref[...] = jnp.full(m_ref.shape, NEG_INF, f32)
    acc_ref[...] = jnp.zeros(acc_ref.shape, f32)
    lane2 = lax.broadcasted_iota(i32, (1, 2 * TQ), 1)
    ones_rows = jnp.ones((PV_ROWS - HEAD_DIM, TK), bf16)

    def logits(c, slot):
        rows = pl.ds(chunk_start(c), TK)
        nm = lax.bitcast_convert_type(keys_ref[rows, :], f32)
        nm2 = jnp.concatenate([nm, nm], axis=1)
        for p in range(N_PAIR):
            lhs = jnp.concatenate([k_ref[rows, p * LANE:(p + 1) * LANE], pos_ref[...]], axis=1)
            rhs = jnp.concatenate([qbd_ref[0, p], slf_ref[p]], axis=0)
            sm = jnp.dot(lhs, rhs, preferred_element_type=f32) + nm2
            sm_ref[slot][p] = sm
            cmax_ref[slot][p] = jnp.max(_tree(jnp.maximum, _row_groups(sm)), axis=0, keepdims=True)

    def probs(c, slot):
        r0f = jnp.asarray(c * TK, dtype=f32)
        for p in range(N_PAIR):
            coff = jnp.where(lane2 < TQ, _slope2(2 * p), _slope2(2 * p + 1)) * r0f
            m_old = m_ref[p]
            m_new = jnp.maximum(m_old, cmax_ref[slot][p] + coff)
            m_safe = jnp.where(m_new == NEG_INF, 0.0, m_new)
            alpha_ref[slot][p] = jnp.where(m_old == NEG_INF, 0.0, jnp.exp2(m_old - m_safe))
            p_ref[slot][p] = jnp.exp2((sm_ref[slot][p] - (m_safe - coff)).astype(bf16))
            m_ref[p] = m_new

    def weighted_sum(c, slot):
        for h in range(N_HEADS):
            lanes = slice((h % 2) * TQ, (h % 2 + 1) * TQ)
            lhs = jnp.concatenate([vt_ref[c, h], ones_rows], axis=0)
            pv = jnp.dot(lhs, p_ref[slot][h // 2, :, lanes], preferred_element_type=f32)
            acc_ref[h] = acc_ref[h] * alpha_ref[slot][h // 2, :, lanes] + pv

    logits(0, 0)
    probs(0, 0)
    logits(1, 1)

    def attn_step(c0):
        probs(c0 + 1, 1)
        logits(c0 + 2, 0)
        logits(c0 + 3, 1)
        weighted_sum(c0, 0)
        probs(c0 + 2, 0)
        weighted_sum(c0 + 1, 1)

    n_quad = (nsc - 1) >> 2

    def attn_body4(b, carry):
        for j in range(4):
            attn_step(8 * b + 2 * j)
        return carry

    def attn_body1(j, carry):
        attn_step(8 * n_quad + 2 * j)
        return carry

    lax.fori_loop(0, n_quad, attn_body4, 0)
    lax.fori_loop(0, (nsc - 1) & 3, attn_body1, 0)
    weighted_sum(nch - 2, 0)
    probs(nch - 1, 1)
    weighted_sum(nch - 1, 1)

    outs = []
    for h in range(N_HEADS):
        a = acc_ref[h]
        outs.append(a[0:HEAD_DIM] / a[HEAD_DIM:HEAD_DIM + 1])
    o_ref[...] = jnp.concatenate(outs, axis=0).T.astype(bf16)


def _attn_call(qbd, qi, wit, k, vt, ki, topk):
    s = k.shape[0]
    nq = s // TQ
    whole = pl.BlockSpec(memory_space=pltpu.VMEM)
    il = [pltpu.VMEM((2 * TK, IDX_HEADS * TQ), f32)] * 2
    pre = [pltpu.VMEM((TK, TQ), f32)] * N_STAGE
    return pl.pallas_call(
        functools.partial(_attn_kernel, topk=topk),
        grid=(nq,),
        in_specs=[pl.BlockSpec((1, N_PAIR, LANE, 2 * TQ), lambda i: (i, 0, 0, 0)),
                  pl.BlockSpec((1, IDX_K, IDX_HEADS * TQ), lambda i: (i, 0, 0)),
                  pl.BlockSpec((SUBLANE, TQ), lambda i: (0, i)),
                  whole, whole, whole],
        out_specs=pl.BlockSpec((TQ, ATTN_WIDTH), lambda i: (i, 0)),
        out_shape=jax.ShapeDtypeStruct((s, ATTN_WIDTH), bf16),
        scratch_shapes=[pltpu.VMEM((s, TQ), i32),
                        pltpu.VMEM((TK, LANE), bf16),
                        pltpu.VMEM((N_PAIR, LANE, 2 * TQ), bf16),
                        *il, *pre,
                        pltpu.VMEM((N_PAIR, TK, 2 * TQ), f32),
                        pltpu.VMEM((N_PAIR, TK, 2 * TQ), f32),
                        pltpu.VMEM((N_PAIR, 1, 2 * TQ), f32),
                        pltpu.VMEM((N_PAIR, 1, 2 * TQ), f32),
                        pltpu.VMEM((N_PAIR, TK, 2 * TQ), bf16),
                        pltpu.VMEM((N_PAIR, TK, 2 * TQ), bf16),
                        pltpu.VMEM((N_PAIR, 1, 2 * TQ), f32),
                        pltpu.VMEM((N_PAIR, 1, 2 * TQ), f32),
                        pltpu.VMEM((N_PAIR, 1, 2 * TQ), f32),
                        pltpu.VMEM((N_HEADS, PV_ROWS, TQ), f32)],
        compiler_params=pltpu.CompilerParams(dimension_semantics=("arbitrary",),
                                             vmem_limit_bytes=VMEM_LIMIT),
        name="attn",
    )(qbd, qi, wit, k, vt, ki)


HALO_POOL = 16


def _mix_kernel(x_ref, attn_ref, u_ref, uh_ref, ga_ref, gp_ref, mod_ref,
                wab_ref, wg_ref, ps_ref, wpb_ref, wo_ref, o_ref, *, tm):
    i = pl.program_id(0)
    y_attn = jnp.dot(attn_ref[...], wab_ref[...], preferred_element_type=f32)

    u = u_ref[...]
    halo = jnp.where(i > 0, uh_ref[...], 0.0)
    a = jnp.concatenate([halo, u], axis=0)
    tpos = (i * tm + lax.broadcasted_iota(i32, (tm, POOL_GROUP_DIM), 0) + 1).astype(f32)
    mixed = []
    for g, wdw in enumerate(POOL_WINDOWS):
        ag = a[:, g * POOL_GROUP_DIM:(g + 1) * POOL_GROUP_DIM]
        ug = ag[HALO_POOL:HALO_POOL + tm]
        ssum = ug
        for j in range(1, wdw):
            ssum = ssum + ag[HALO_POOL - j:HALO_POOL - j + tm]
        pooled = ssum / jnp.minimum(tpos, float(wdw)) - ug
        mixed.append(jnp.dot(pooled.astype(bf16), wg_ref[g], preferred_element_type=f32))
    mixed = jnp.concatenate(mixed, axis=1) * ps_ref[...]
    y_pool = jnp.dot(mixed.astype(bf16), wpb_ref[...], preferred_element_type=f32)

    merged = ga_ref[...].astype(f32) * y_attn + gp_ref[...].astype(f32) * y_pool
    o = jnp.dot(merged.astype(bf16), wo_ref[...], preferred_element_type=f32)
    gate = mod_ref[0:1, 2 * D_MODEL:3 * D_MODEL]
    o_ref[...] = x_ref[...] + gate * o


def _mix_call(x2, attn, u, ga, gp, mod, wab, wg, ps, wpb, wo, tm):
    s = x2.shape[0]
    const2 = lambda i: (0, 0)
    hb = tm // HALO_POOL
    return pl.pallas_call(
        functools.partial(_mix_kernel, tm=tm),
        grid=(s // tm,),
        in_specs=[pl.BlockSpec((tm, D_MODEL), lambda i: (i, 0)),
                  pl.BlockSpec((tm, ATTN_WIDTH), lambda i: (i, 0)),
                  pl.BlockSpec((tm, POOL_WIDTH), lambda i: (i, 0)),
                  pl.BlockSpec((HALO_POOL, POOL_WIDTH), lambda i: (jnp.maximum(i * hb - 1, 0), 0)),
                  pl.BlockSpec((tm, D_MODEL), lambda i: (i, 0)),
                  pl.BlockSpec((tm, D_MODEL), lambda i: (i, 0)),
                  pl.BlockSpec((SUBLANE, N_MOD * D_MODEL), const2),
                  pl.BlockSpec((ATTN_WIDTH, D_MODEL), const2),
                  pl.BlockSpec((len(POOL_WINDOWS), POOL_GROUP_DIM, POOL_GROUP_DIM), lambda i: (0, 0, 0)),
                  pl.BlockSpec((1, POOL_WIDTH), const2),
                  pl.BlockSpec((POOL_WIDTH, D_MODEL), const2),
                  pl.BlockSpec((D_MODEL, D_MODEL), const2)],
        out_specs=pl.BlockSpec((tm, D_MODEL), lambda i: (i, 0)),
        out_shape=jax.ShapeDtypeStruct((s, D_MODEL), f32),
        compiler_params=pltpu.CompilerParams(dimension_semantics=("parallel",),
                                             vmem_limit_bytes=VMEM_LIMIT),
        name="mix",
    )(x2, attn, u, u, ga, gp, mod, wab, wg, ps, wpb, wo)


HALO_CONV = 8


def _ffn_kernel(x_ref, xh_ref, mod_ref, g2_ref, wup_ref, cw_ref, cb_ref, wdn_ref, o_ref, *, tm):
    i = pl.program_id(0)
    shift = mod_ref[0:1, 3 * D_MODEL:4 * D_MODEL]
    scale = mod_ref[0:1, 4 * D_MODEL:5 * D_MODEL]
    gate = mod_ref[0:1, 5 * D_MODEL:6 * D_MODEL]
    g2 = g2_ref[...]
    x = x_ref[...]
    h = _rms_modulate(x, g2, shift, scale)
    hh = jnp.where(i > 0, _rms_modulate(xh_ref[...], g2, shift, scale), 0.0)
    ha = jnp.concatenate([hh, h], axis=0).astype(bf16)
    up = jnp.dot(ha, wup_ref[...], preferred_element_type=f32)
    cw = cw_ref[...]
    y = cb_ref[...] + cw[0:1, :] * up[HALO_CONV - 2:HALO_CONV - 2 + tm]
    y = y + cw[1:2, :] * up[HALO_CONV - 1:HALO_CONV - 1 + tm]
    y = y + cw[2:3, :] * up[HALO_CONV:HALO_CONV + tm]
    a = y[:, 0:D_FF]
    b = y[:, D_FF:2 * D_FF]
    gated = (a * _sigmoid(a)) * b
    o = jnp.dot(gated.astype(bf16), wdn_ref[...], preferred_element_type=f32)
    o_ref[...] = x + gate * o


def _ffn_call(x1, mod, g2, wup, cw, cb, wdn, tm):
    s = x1.shape[0]
    const2 = lambda i: (0, 0)
    hb = tm // HALO_CONV
    return pl.pallas_call(
        functools.partial(_ffn_kernel, tm=tm),
        grid=(s // tm,),
        in_specs=[pl.BlockSpec((tm, D_MODEL), lambda i: (i, 0)),
                  pl.BlockSpec((HALO_CONV, D_MODEL), lambda i: (jnp.maximum(i * hb - 1, 0), 0)),
                  pl.BlockSpec((SUBLANE, N_MOD * D_MODEL), const2),
                  pl.BlockSpec((1, D_MODEL), const2),
                  pl.BlockSpec((D_MODEL, 2 * D_FF), const2, pipeline_mode=pl.Buffered(1)),
                  pl.BlockSpec((3, 2 * D_FF), const2),
                  pl.BlockSpec((1, 2 * D_FF), const2),
                  pl.BlockSpec((D_FF, D_MODEL), const2, pipeline_mode=pl.Buffered(1))],
        out_specs=pl.BlockSpec((tm, D_MODEL), lambda i: (i, 0)),
        out_shape=jax.ShapeDtypeStruct((s, D_MODEL), f32),
        compiler_params=pltpu.CompilerParams(dimension_semantics=("parallel",),
                                             vmem_limit_bytes=VMEM_LIMIT),
        name="ffn",
    )(x1, x1, mod, g2, wup, cw, cb, wdn)


def _pack_w_idx(w):
    return jnp.pad(w[:, IN_IDX0:IN_IDX1], ((0, 0), (0, 3 * LANE - (IN_IDX1 - IN_IDX0))))


def kernel(x, c, w_ada, b_ada, norm1_g, w_in, q_norm_g, k_norm_g, w_attn_br, w_pool_grp,
           pool_scale, w_pool_br, w_out, norm2_g, w_up, conv_w, conv_b, w_down):
    bsz, s, d = x.shape
    assert bsz == 1 and d == D_MODEL and s % (N_STAGE * TK) == 0
    depth = w_ada.shape[0]
    topk = min(TOPK_MAX, s // 4)
    x2 = x.reshape(s, d)
    c8 = jnp.pad(c, ((0, SUBLANE - bsz), (0, 0)))
    for l in range(depth):
        mod = _mod_call(c8, w_ada[l], b_ada[l].reshape(1, -1))
        qbd, k, vt, qi, ki, wit, u, ga, gp = _proj_call(
            x2, mod, norm1_g[l].reshape(1, -1),
            w_in[l][:, 0:IN_IDX0].astype(bf16), w_in[l][:, IN_IDX1:IN_END].astype(bf16),
            jnp.tile(q_norm_g[l], N_HEADS).reshape(1, -1),
            jnp.tile(k_norm_g[l], N_HEADS).reshape(1, -1), _pack_w_idx(w_in[l]), tm=512)
        attn = _attn_call(qbd, qi, wit, k, vt, ki, topk)
        x2 = _mix_call(x2, attn, u, ga, gp, mod, w_attn_br[l].astype(bf16),
                       w_pool_grp[l].astype(bf16), pool_scale[l].reshape(1, -1),
                       w_pool_br[l].astype(bf16), w_out[l].astype(bf16), tm=512)
        x2 = _ffn_call(x2, mod, norm2_g[l].reshape(1, -1), w_up[l].astype(bf16), conv_w[l],
                       conv_b[l].reshape(1, -1), w_down[l].astype(bf16), tm=512)
    return x2.reshape(bsz, s, d)
```

```python
import functools

import jax
import jax.numpy as jnp
from jax import lax
from jax.experimental import pallas as pl
from jax.experimental.pallas import tpu as pltpu

f32 = jnp.float32
bf16 = jnp.bfloat16
i32 = jnp.int32

D_MODEL = 1024
N_HEADS = 8
HEAD_DIM = 64
ATTN_WIDTH = N_HEADS * HEAD_DIM
IDX_HEADS = 4
IDX_DIM = 64
TOPK_MAX = 256
POOL_WINDOWS = (2, 4, 8, 16)
POOL_GROUP_DIM = 128
POOL_WIDTH = 512
D_FF = 2816
EPS = 1e-6
N_MOD = 6

LANE = 128
SUBLANE = 8
VMEM_LIMIT = 58 * 1024 * 1024

LOG2E = 1.4426950408889634
INT_MIN = -2147483648
NEG_INF = float("-inf")

IN_IDX0 = 3 * ATTN_WIDTH
IN_IDX1 = IN_IDX0 + IDX_HEADS * IDX_DIM + IDX_DIM + IDX_HEADS
IN_END = IN_IDX1 + POOL_WIDTH + 2 * D_MODEL
C_Q, C_K, C_V, C_QKV = 0, ATTN_WIDTH, 2 * ATTN_WIDTH, 3 * ATTN_WIDTH
C_U, C_GA, C_GP, C_REST = 0, POOL_WIDTH, POOL_WIDTH + D_MODEL, POOL_WIDTH + 2 * D_MODEL

TQ = 128
TK = 256
N_PAIR = N_HEADS // 2
IDX_K = 4 * LANE


def _sigmoid(x):
    return 1.0 / (1.0 + jnp.exp(-x))


def _rms_modulate(x, g, shift, scale):
    y = x * lax.rsqrt(jnp.mean(x * x, axis=-1, keepdims=True) + EPS)
    return (y * g) * (1.0 + scale) + shift


def _mod_kernel(c_ref, w_ref, b_ref, o_ref):
    c = c_ref[...]
    sc = c * _sigmoid(c)
    o_ref[...] = jnp.dot(sc, w_ref[...], precision=lax.Precision.HIGHEST,
                         preferred_element_type=f32) + b_ref[...]


def _mod_call(c8, w_ada, b_ada):
    n = w_ada.shape[1]
    tn = 1024
    return pl.pallas_call(
        _mod_kernel,
        grid=(n // tn,),
        in_specs=[pl.BlockSpec((SUBLANE, D_MODEL), lambda j: (0, 0)),
                  pl.BlockSpec((D_MODEL, tn), lambda j: (0, j)),
                  pl.BlockSpec((1, tn), lambda j: (0, j))],
        out_specs=pl.BlockSpec((SUBLANE, tn), lambda j: (0, j)),
        out_shape=jax.ShapeDtypeStruct((SUBLANE, n), f32),
        name="mod",
    )(c8, w_ada, b_ada)


def _split3(x):
    hi = x.astype(bf16).astype(f32)
    r = x - hi
    mid = r.astype(bf16).astype(f32)
    lo = (r - mid).astype(bf16).astype(f32)
    return hi, mid, lo


def _head_norm(z, g, bd):
    z2 = z * z
    hi = z2.astype(bf16)
    lo = (z2 - hi.astype(f32)).astype(bf16)
    ms = jnp.dot(hi, bd, preferred_element_type=f32) + jnp.dot(lo, bd, preferred_element_type=f32)
    return (z * lax.rsqrt(ms + EPS)) * g


def _proj_kernel(x_ref, mod_ref, g1_ref, wa_ref, wb_ref, qg_ref, kg_ref, widx_ref,
                 qbd_ref, k_ref, vt_ref, qi_ref, ki_ref, wit_ref, u_ref, ga_ref, gp_ref, *, tm):
    x = x_ref[...]
    shift = mod_ref[0:1, 0:D_MODEL]
    scale = mod_ref[0:1, D_MODEL:2 * D_MODEL]
    h = _rms_modulate(x, g1_ref[...], shift, scale)
    hb = h.astype(bf16)
    proj = jnp.dot(hb, wa_ref[...], preferred_element_type=f32)
    rest = jnp.dot(hb, wb_ref[...], preferred_element_type=f32)

    r = lax.broadcasted_iota(i32, (ATTN_WIDTH, ATTN_WIDTH), 0)
    c = lax.broadcasted_iota(i32, (ATTN_WIDTH, ATTN_WIDTH), 1)
    head_shift = HEAD_DIM.bit_length() - 1
    bd = jnp.where((r >> head_shift) == (c >> head_shift), 1.0 / HEAD_DIM, 0.0).astype(bf16)

    q = _head_norm(proj[:, C_Q:C_K], qg_ref[...], bd) * (HEAD_DIM ** -0.5 * LOG2E)
    k = _head_norm(proj[:, C_K:C_V], kg_ref[...], bd)
    k_ref[...] = k.astype(bf16)

    low = lax.broadcasted_iota(i32, (TQ, LANE), 1) < HEAD_DIM
    low_tm = lax.broadcasted_iota(i32, (tm, LANE), 1) < HEAD_DIM
    pidx = jnp.dot(h, widx_ref[...], precision=lax.Precision.HIGHEST, preferred_element_type=f32)
    qi = pidx[:, 0:IDX_HEADS * IDX_DIM] * (IDX_DIM ** -0.5)
    up64 = lambda z: pltpu.roll(z, HEAD_DIM, axis=1)
    for g in range(tm // TQ):
        rows = slice(g * TQ, (g + 1) * TQ)
        for p in range(N_PAIR):
            qp = q[rows, p * LANE:(p + 1) * LANE]
            bd_q = jnp.concatenate([jnp.where(low, qp, 0.0), jnp.where(low, 0.0, qp)], axis=0)
            qbd_ref[g, p] = bd_q.T.astype(bf16)
        for hh in range(IDX_HEADS):
            seg = qi[rows, (hh // 2) * LANE:(hh // 2 + 1) * LANE]
            own = low if hh % 2 == 0 else jnp.logical_not(low)
            qh, qm, ql = _split3(jnp.where(own, seg, 0.0))
            both = lambda z: z + up64(z)
            in_low = lambda z: jnp.where(low, both(z), 0.0)
            hcols = slice(hh * TQ, (hh + 1) * TQ)
            for kt, term in enumerate((both(qh), both(qm), in_low(ql), in_low(qh))):
                qi_ref[g, kt * LANE:(kt + 1) * LANE, hcols] = term.T.astype(bf16)

    v = proj[:, C_V:C_QKV]
    vt = v.T.astype(bf16)
    for cc in range(tm // TK):
        for hh in range(N_HEADS):
            vt_ref[cc, hh] = vt[hh * HEAD_DIM:(hh + 1) * HEAD_DIM, cc * TK:(cc + 1) * TK]

    kw = pidx[:, IDX_HEADS * IDX_DIM:3 * LANE]
    kh, km, kl = _split3(jnp.where(low_tm, kw, 0.0))
    ki_ref[:, 0:LANE] = (kh + up64(km)).astype(bf16)
    ki_ref[:, LANE:2 * LANE] = kl.astype(bf16)
    wt = (kw * (IDX_HEADS ** -0.5)).T
    wit_ref[...] = wt[HEAD_DIM:HEAD_DIM + SUBLANE, :]
    u_ref[...] = rest[:, C_U:C_GA]
    ga_ref[...] = _sigmoid(rest[:, C_GA:C_GP]).astype(bf16)
    gp_ref[...] = _sigmoid(rest[:, C_GP:C_REST]).astype(bf16)


def _proj_call(x2, mod, g1, w_qkv, w_rest, qg, kg, widx, tm):
    s = x2.shape[0]
    nq = s // TQ
    const = lambda i: (0, 0)
    return pl.pallas_call(
        functools.partial(_proj_kernel, tm=tm),
        grid=(s // tm,),
        in_specs=[pl.BlockSpec((tm, D_MODEL), lambda i: (i, 0)),
                  pl.BlockSpec((SUBLANE, N_MOD * D_MODEL), const),
                  pl.BlockSpec((1, D_MODEL), const),
                  pl.BlockSpec((D_MODEL, C_QKV), const),
                  pl.BlockSpec((D_MODEL, C_REST), const),
                  pl.BlockSpec((1, ATTN_WIDTH), const),
                  pl.BlockSpec((1, ATTN_WIDTH), const),
                  pl.BlockSpec((D_MODEL, 3 * LANE), const)],
        out_specs=[pl.BlockSpec((tm // TQ, N_PAIR, LANE, 2 * TQ), lambda i: (i, 0, 0, 0)),
                   pl.BlockSpec((tm, ATTN_WIDTH), lambda i: (i, 0)),
                   pl.BlockSpec((tm // TK, N_HEADS, HEAD_DIM, TK), lambda i: (i, 0, 0, 0)),
                   pl.BlockSpec((tm // TQ, IDX_K, IDX_HEADS * TQ), lambda i: (i, 0, 0)),
                   pl.BlockSpec((tm, 2 * LANE), lambda i: (i, 0)),
                   pl.BlockSpec((SUBLANE, tm), lambda i: (0, i)),
                   pl.BlockSpec((tm, POOL_WIDTH), lambda i: (i, 0)),
                   pl.BlockSpec((tm, D_MODEL), lambda i: (i, 0)),
                   pl.BlockSpec((tm, D_MODEL), lambda i: (i, 0))],
        out_shape=[jax.ShapeDtypeStruct((nq, N_PAIR, LANE, 2 * TQ), bf16),
                   jax.ShapeDtypeStruct((s, ATTN_WIDTH), bf16),
                   jax.ShapeDtypeStruct((s // TK, N_HEADS, HEAD_DIM, TK), bf16),
                   jax.ShapeDtypeStruct((nq, IDX_K, IDX_HEADS * TQ), bf16),
                   jax.ShapeDtypeStruct((s, 2 * LANE), bf16),
                   jax.ShapeDtypeStruct((SUBLANE, s), f32),
                   jax.ShapeDtypeStruct((s, POOL_WIDTH), f32),
                   jax.ShapeDtypeStruct((s, D_MODEL), bf16),
                   jax.ShapeDtypeStruct((s, D_MODEL), bf16)],
        compiler_params=pltpu.CompilerParams(dimension_semantics=("parallel",),
                                             vmem_limit_bytes=VMEM_LIMIT),
        name="proj",
    )(x2, mod, g1, w_qkv, w_rest, qg, kg, widx)


def _slope2(h):
    return (2.0 ** (-8.0 * (h + 1) / N_HEADS)) * LOG2E


def _tree(op, xs):
    xs = list(xs)
    while len(xs) > 1:
        nxt = [op(xs[a], xs[a + 1]) for a in range(0, len(xs) - 1, 2)]
        if len(xs) % 2:
            nxt.append(xs[-1])
        xs = nxt
    return xs[0]


def _row_groups(x):
    return [x[j * SUBLANE:(j + 1) * SUBLANE] for j in range(x.shape[0] // SUBLANE)]


N_CNT_ACC = 8
N_STAGE = 4
PV_ROWS = HEAD_DIM + 16


def _attn_kernel(qbd_ref, qi_ref, wi_ref, k_ref, vt_ref, ki_ref, o_ref,
                 keys_ref, pos_ref, slf_ref, il0_ref, il1_ref,
                 pre0_ref, pre1_ref, pre2_ref, pre3_ref, sm0_ref, sm1_ref, cmax0_ref, cmax1_ref,
                 p0_ref, p1_ref, alpha0_ref, alpha1_ref, m_ref, acc_ref, *, topk):
    il_ref = (il0_ref, il1_ref)
    pre_ref = (pre0_ref, pre1_ref, pre2_ref, pre3_ref)
    sm_ref = (sm0_ref, sm1_ref)
    cmax_ref = (cmax0_ref, cmax1_ref)
    p_ref = (p0_ref, p1_ref)
    alpha_ref = (alpha0_ref, alpha1_ref)
    i = pl.program_id(0)
    nsc = (i + 4) >> 2
    nch = 2 * nsc
    nbody = (i + 8) >> 3
    n_causal = (i + 2) >> 1
    n_full, n_tail = n_causal >> 2, n_causal & 3

    def chunk_start(c):
        return pl.multiple_of(c * TK, TK)

    def pipeline(produce, consume, carry):
        for j in range(N_STAGE):
            produce(j, j)

        def body(b, carry):
            c0 = N_STAGE * b
            for j in range(N_STAGE):
                carry = consume(c0 + j, j, carry)
                produce(c0 + N_STAGE + j, j)
            return carry

        carry = lax.fori_loop(0, nbody - 1, body, carry)
        c0 = N_STAGE * (nbody - 1)
        for j in range(N_STAGE):
            carry = consume(c0 + j, j, carry)
        return carry

    @pl.when(i == 0)
    def _():
        col = lax.broadcasted_iota(i32, (TK, LANE), 1)
        row = lax.broadcasted_iota(i32, (TK, LANE), 0).astype(f32)
        pos_ref[...] = jnp.where(col < 3, row, 0.0).astype(bf16)
        term = lax.broadcasted_iota(i32, (LANE, 2 * TQ), 0)
        lane = lax.broadcasted_iota(i32, (LANE, 2 * TQ), 1)
        for p in range(N_PAIR):
            sl = jnp.where(lane < TQ, _slope2(2 * p), _slope2(2 * p + 1)) + jnp.zeros((LANE, 2 * TQ), f32)
            hi, mid, lo = _split3(sl)
            slf_ref[p] = jnp.where(term == 0, hi, jnp.where(term == 1, mid,
                                   jnp.where(term == 2, lo, 0.0))).astype(bf16)

    qi = qi_ref[0]
    w = wi_ref[...]
    d0 = (lax.broadcasted_iota(i32, (TK, TQ), 1) - lax.broadcasted_iota(i32, (TK, TQ), 0))

    def idx_matmul(u, slot):
        rows = pl.ds(pl.multiple_of(u * 2 * TK, 2 * TK), 2 * TK)
        kx = ki_ref[rows, 0:LANE]
        ky = ki_ref[rows, LANE:2 * LANE]
        il_ref[slot][...] = jnp.dot(jnp.concatenate([kx, kx, kx, ky], axis=1), qi,
                                    preferred_element_type=f32)

    def idx_keys(c, slot, half):
        r0 = chunk_start(c)
        rows = slice(half * TK, (half + 1) * TK)
        sc = jnp.maximum(il_ref[slot][rows, 0:TQ], 0.0) * w[0:1, :]
        for hh in range(1, IDX_HEADS):
            sc = sc + jnp.maximum(il_ref[slot][rows, hh * TQ:(hh + 1) * TQ], 0.0) * w[hh:hh + 1, :]
        b = lax.bitcast_convert_type(sc, i32)
        key = jnp.where(b < 0, -(b & 0x7FFFFFFF), b)
        valid = d0 >= (r0 - i * TQ)
        keys_ref[pl.ds(r0, TK), :] = jnp.where(valid, key, INT_MIN)

    idx_matmul(0, 0)
    idx_matmul(1, 1)

    def idx_body(b, carry):
        c0 = N_STAGE * b
        idx_keys(c0, 0, 0)
        idx_keys(c0 + 1, 0, 1)
        idx_matmul(2 * b + 2, 0)
        idx_keys(c0 + 2, 1, 0)
        idx_keys(c0 + 3, 1, 1)
        idx_matmul(2 * b + 3, 1)
        return carry

    lax.fori_loop(0, nbody - 1, idx_body, 0)
    c_last = N_STAGE * (nbody - 1)
    for j in range(N_STAGE):
        idx_keys(c_last + j, j // 2, j % 2)

    def count_ge(cand):
        def add_chunk(c, accs):
            ind = jnp.where(keys_ref[pl.ds(chunk_start(c), TK), :] >= cand, 1, 0)
            for j, g in enumerate(_row_groups(ind)):
                accs[j % N_CNT_ACC] = accs[j % N_CNT_ACC] + g

        def body(b, accs):
            accs = list(accs)
            for part in range(N_STAGE):
                add_chunk(b * N_STAGE + part, accs)
            return tuple(accs)

        def tail(j, accs):
            accs = list(accs)
            add_chunk(n_full * N_STAGE + j, accs)
            return tuple(accs)

        accs = lax.fori_loop(0, n_full, body,
                             tuple(jnp.zeros((SUBLANE, TQ), i32) for _ in range(N_CNT_ACC)))
        accs = lax.fori_loop(0, n_tail, tail, accs)
        return jnp.sum(_tree(jnp.add, accs), axis=0, keepdims=True)

    def bit_step(bi, st):
        t, above = st
        cand = t + lax.shift_left(jnp.int32(1), 31 - bi)
        cnt = count_ge(cand)
        up = cnt >= topk
        return jnp.where(up, cand, t), jnp.where(up, above, cnt)

    t, above = lax.fori_loop(0, 32, bit_step,
                             (jnp.full((1, TQ), INT_MIN, i32), jnp.zeros((1, TQ), i32)))
    t = jnp.maximum(t, INT_MIN + 1)
    r_tie = (topk - above).astype(f32)

    tri = jnp.where(lax.broadcasted_iota(i32, (TK, TK), 0) >= lax.broadcasted_iota(i32, (TK, TK), 1),
                    1.0, 0.0).astype(bf16)

    def tie_rank(c, slot):
        e = jnp.where(keys_ref[pl.ds(chunk_start(c), TK), :] == t, 1.0, 0.0).astype(bf16)
        pre_ref[slot][...] = jnp.dot(tri, e, preferred_element_type=f32)

    def mask_out(c, slot, rank):
        r0 = chunk_start(c)
        kk = keys_ref[pl.ds(r0, TK), :]
        pre = pre_ref[slot][...] + rank
        nm = jnp.where(kk > t, 0.0, jnp.where(kk == t, jnp.where(pre <= r_tie, 0.0, NEG_INF), NEG_INF))
        keys_ref[pl.ds(r0, TK), :] = lax.bitcast_convert_type(nm, i32)
        return pre[TK - 1:TK, :]

    pipeline(tie_rank, mask_out, jnp.zeros((1, TQ), f32))

    m_ref[...] = jnp.full(m_ref.shape, NEG_INF, f32)
    acc_ref[...] = jnp.zeros(acc_ref.shape, f32)
    lane2 = lax.broadcasted_iota(i32, (1, 2 * TQ), 1)
    ones_rows = jnp.ones((PV_ROWS - HEAD_DIM, TK), bf16)

    def logits(c, slot):
        rows = pl.ds(chunk_start(c), TK)
        nm = lax.bitcast_convert_type(keys_ref[rows, :], f32)
        nm2 = jnp.concatenate([nm, nm], axis=1)
        for p in range(N_PAIR):
            lhs = jnp.concatenate([k_ref[rows, p * LANE:(p + 1) * LANE], pos_ref[...]], axis=1)
            rhs = jnp.concatenate([qbd_ref[0, p], slf_ref[p]], axis=0)
            sm = jnp.dot(lhs, rhs, preferred_element_type=f32) + nm2
            sm_ref[slot][p] = sm
            cmax_ref[slot][p] = jnp.max(_tree(jnp.maximum, _row_groups(sm)), axis=0, keepdims=True)

    def probs(c, slot):
        r0f = jnp.asarray(c * TK, dtype=f32)
        for p in range(N_PAIR):
            coff = jnp.where(lane2 < TQ, _slope2(2 * p), _slope2(2 * p + 1)) * r0f
            m_old = m_ref[p]
            m_new = jnp.maximum(m_old, cmax_ref[slot][p] + coff)
            m_safe = jnp.where(m_new == NEG_INF, 0.0, m_new)
            alpha_ref[slot][p] = jnp.where(m_old == NEG_INF, 0.0, jnp.exp2(m_old - m_safe))
            p_ref[slot][p] = jnp.exp2((sm_ref[slot][p] - (m_safe - coff)).astype(bf16))
            m_ref[p] = m_new

    def weighted_sum(c, slot):
        for h in range(N_HEADS):
            lanes = slice((h % 2) * TQ, (h % 2 + 1) * TQ)
            lhs = jnp.concatenate([vt_ref[c, h], ones_rows], axis=0)
            pv = jnp.dot(lhs, p_ref[slot][h // 2, :, lanes], preferred_element_type=f32)
            acc_ref[h] = acc_ref[h] * alpha_ref[slot][h // 2, :, lanes] + pv

    logits(0, 0)
    probs(0, 0)
    logits(1, 1)

    def attn_step(c0):
        probs(c0 + 1, 1)
        logits(c0 + 2, 0)
        logits(c0 + 3, 1)
        weighted_sum(c0, 0)
        probs(c0 + 2, 0)
        weighted_sum(c0 + 1, 1)

    n_quad = (nsc - 1) >> 2

    def attn_body4(b, carry):
        for j in range(4):
            attn_step(8 * b + 2 * j)
        return carry

    def attn_body1(j, carry):
        attn_step(8 * n_quad + 2 * j)
        return carry

    lax.fori_loop(0, n_quad, attn_body4, 0)
    lax.fori_loop(0, (nsc - 1) & 3, attn_body1, 0)
    weighted_sum(nch - 2, 0)
    probs(nch - 1, 1)
    weighted_sum(nch - 1, 1)

    outs = []
    for h in range(N_HEADS):
        a = acc_ref[h]
        outs.append(a[0:HEAD_DIM] / a[HEAD_DIM:HEAD_DIM + 1])
    o_ref[...] = jnp.concatenate(outs, axis=0).T.astype(bf16)


def _attn_call(qbd, qi, wit, k, vt, ki, topk):
    s = k.shape[0]
    nq = s // TQ
    whole = pl.BlockSpec(memory_space=pltpu.VMEM)
    il = [pltpu.VMEM((2 * TK, IDX_HEADS * TQ), f32)] * 2
    pre = [pltpu.VMEM((TK, TQ), f32)] * N_STAGE
    return pl.pallas_call(
        functools.partial(_attn_kernel, topk=topk),
        grid=(nq,),
        in_specs=[pl.BlockSpec((1, N_PAIR, LANE, 2 * TQ), lambda i: (i, 0, 0, 0)),
                  pl.BlockSpec((1, IDX_K, IDX_HEADS * TQ), lambda i: (i, 0, 0)),
                  pl.BlockSpec((SUBLANE, TQ), lambda i: (0, i)),
                  whole, whole, whole],
        out_specs=pl.BlockSpec((TQ, ATTN_WIDTH), lambda i: (i, 0)),
        out_shape=jax.ShapeDtypeStruct((s, ATTN_WIDTH), bf16),
        scratch_shapes=[pltpu.VMEM((s, TQ), i32),
                        pltpu.VMEM((TK, LANE), bf16),
                        pltpu.VMEM((N_PAIR, LANE, 2 * TQ), bf16),
                        *il, *pre,
                        pltpu.VMEM((N_PAIR, TK, 2 * TQ), f32),
                        pltpu.VMEM((N_PAIR, TK, 2 * TQ), f32),
                        pltpu.VMEM((N_PAIR, 1, 2 * TQ), f32),
                        pltpu.VMEM((N_PAIR, 1, 2 * TQ), f32),
                        pltpu.VMEM((N_PAIR, TK, 2 * TQ), bf16),
                        pltpu.VMEM((N_PAIR, TK, 2 * TQ), bf16),
                        pltpu.VMEM((N_PAIR, 1, 2 * TQ), f32),
                        pltpu.VMEM((N_PAIR, 1, 2 * TQ), f32),
                        pltpu.VMEM((N_PAIR, 1, 2 * TQ), f32),
                        pltpu.VMEM((N_HEADS, PV_ROWS, TQ), f32)],
        compiler_params=pltpu.CompilerParams(dimension_semantics=("arbitrary",),
                                             vmem_limit_bytes=VMEM_LIMIT),
        name="attn",
    )(qbd, qi, wit, k, vt, ki)


HALO_POOL = 16


def _mix_kernel(x_ref, attn_ref, u_ref, uh_ref, ga_ref, gp_ref, mod_ref,
                wab_ref, wg_ref, ps_ref, wpb_ref, wo_ref, o_ref, *, tm):
    i = pl.program_id(0)
    y_attn = jnp.dot(attn_ref[...], wab_ref[...], preferred_element_type=f32)

    u = u_ref[...]
    halo = jnp.where(i > 0, uh_ref[...], 0.0)
    a = jnp.concatenate([halo, u], axis=0)
    tpos = (i * tm + lax.broadcasted_iota(i32, (tm, POOL_GROUP_DIM), 0) + 1).astype(f32)
    mixed = []
    for g, wdw in enumerate(POOL_WINDOWS):
        ag = a[:, g * POOL_GROUP_DIM:(g + 1) * POOL_GROUP_DIM]
        ug = ag[HALO_POOL:HALO_POOL + tm]
        ssum = ug
        for j in range(1, wdw):
            ssum = ssum + ag[HALO_POOL - j:HALO_POOL - j + tm]
        pooled = ssum / jnp.minimum(tpos, float(wdw)) - ug
        mixed.append(jnp.dot(pooled.astype(bf16), wg_ref[g], preferred_element_type=f32))
    mixed = jnp.concatenate(mixed, axis=1) * ps_ref[...]
    y_pool = jnp.dot(mixed.astype(bf16), wpb_ref[...], preferred_element_type=f32)

    merged = ga_ref[...].astype(f32) * y_attn + gp_ref[...].astype(f32) * y_pool
    o = jnp.dot(merged.astype(bf16), wo_ref[...], preferred_element_type=f32)
    gate = mod_ref[0:1, 2 * D_MODEL:3 * D_MODEL]
    o_ref[...] = x_ref[...] + gate * o


def _mix_call(x2, attn, u, ga, gp, mod, wab, wg, ps, wpb, wo, tm):
    s = x2.shape[0]
    const2 = lambda i: (0, 0)
    hb = tm // HALO_POOL
    return pl.pallas_call(
        functools.partial(_mix_kernel, tm=tm),
        grid=(s // tm,),
        in_specs=[pl.BlockSpec((tm, D_MODEL), lambda i: (i, 0)),
                  pl.BlockSpec((tm, ATTN_WIDTH), lambda i: (i, 0)),
                  pl.BlockSpec((tm, POOL_WIDTH), lambda i: (i, 0)),
                  pl.BlockSpec((HALO_POOL, POOL_WIDTH), lambda i: (jnp.maximum(i * hb - 1, 0), 0)),
                  pl.BlockSpec((tm, D_MODEL), lambda i: (i, 0)),
                  pl.BlockSpec((tm, D_MODEL), lambda i: (i, 0)),
                  pl.BlockSpec((SUBLANE, N_MOD * D_MODEL), const2),
                  pl.BlockSpec((ATTN_WIDTH, D_MODEL), const2),
                  pl.BlockSpec((len(POOL_WINDOWS), POOL_GROUP_DIM, POOL_GROUP_DIM), lambda i: (0, 0, 0)),
                  pl.BlockSpec((1, POOL_WIDTH), const2),
                  pl.BlockSpec((POOL_WIDTH, D_MODEL), const2),
                  pl.BlockSpec((D_MODEL, D_MODEL), const2)],
        out_specs=pl.BlockSpec((tm, D_MODEL), lambda i: (i, 0)),
        out_shape=jax.ShapeDtypeStruct((s, D_MODEL), f32),
        compiler_params=pltpu.CompilerParams(dimension_semantics=("parallel",),
                                             vmem_limit_bytes=VMEM_LIMIT),
        name="mix",
    )(x2, attn, u, u, ga, gp, mod, wab, wg, ps, wpb, wo)


HALO_CONV = 8


def _ffn_kernel(x_ref, xh_ref, mod_ref, g2_ref, wup_ref, cw_ref, cb_ref, wdn_ref, o_ref, *, tm):
    i = pl.program_id(0)
    shift = mod_ref[0:1, 3 * D_MODEL:4 * D_MODEL]
    scale = mod_ref[0:1, 4 * D_MODEL:5 * D_MODEL]
    gate = mod_ref[0:1, 5 * D_MODEL:6 * D_MODEL]
    g2 = g2_ref[...]
    x = x_ref[...]
    h = _rms_modulate(x, g2, shift, scale)
    hh = jnp.where(i > 0, _rms_modulate(xh_ref[...], g2, shift, scale), 0.0)
    ha = jnp.concatenate([hh, h], axis=0).astype(bf16)
    up = jnp.dot(ha, wup_ref[...], preferred_element_type=f32)
    cw = cw_ref[...]
    y = cb_ref[...] + cw[0:1, :] * up[HALO_CONV - 2:HALO_CONV - 2 + tm]
    y = y + cw[1:2, :] * up[HALO_CONV - 1:HALO_CONV - 1 + tm]
    y = y + cw[2:3, :] * up[HALO_CONV:HALO_CONV + tm]
    a = y[:, 0:D_FF]
    b = y[:, D_FF:2 * D_FF]
    gated = (a * _sigmoid(a)) * b
    o = jnp.dot(gated.astype(bf16), wdn_ref[...], preferred_element_type=f32)
    o_ref[...] = x + gate * o


def _ffn_call(x1, mod, g2, wup, cw, cb, wdn, tm):
    s = x1.shape[0]
    const2 = lambda i: (0, 0)
    hb = tm // HALO_CONV
    return pl.pallas_call(
        functools.partial(_ffn_kernel, tm=tm),
        grid=(s // tm,),
        in_specs=[pl.BlockSpec((tm, D_MODEL), lambda i: (i, 0)),
                  pl.BlockSpec((HALO_CONV, D_MODEL), lambda i: (jnp.maximum(i * hb - 1, 0), 0)),
                  pl.BlockSpec((SUBLANE, N_MOD * D_MODEL), const2),
                  pl.BlockSpec((1, D_MODEL), const2),
                  pl.BlockSpec((D_MODEL, 2 * D_FF), const2, pipeline_mode=pl.Buffered(1)),
                  pl.BlockSpec((3, 2 * D_FF), const2),
                  pl.BlockSpec((1, 2 * D_FF), const2),
                  pl.BlockSpec((D_FF, D_MODEL), const2, pipeline_mode=pl.Buffered(1))],
        out_specs=pl.BlockSpec((tm, D_MODEL), lambda i: (i, 0)),
        out_shape=jax.ShapeDtypeStruct((s, D_MODEL), f32),
        compiler_params=pltpu.CompilerParams(dimension_semantics=("parallel",),
                                             vmem_limit_bytes=VMEM_LIMIT),
        name="ffn",
    )(x1, x1, mod, g2, wup, cw, cb, wdn)


def _pack_w_idx(w):
    return jnp.pad(w[:, IN_IDX0:IN_IDX1], ((0, 0), (0, 3 * LANE - (IN_IDX1 - IN_IDX0))))


def kernel(x, c, w_ada, b_ada, norm1_g, w_in, q_norm_g, k_norm_g, w_attn_br, w_pool_grp,
           pool_scale, w_pool_br, w_out, norm2_g, w_up, conv_w, conv_b, w_down):
    bsz, s, d = x.shape
    assert bsz == 1 and d == D_MODEL and s % (N_STAGE * TK) == 0
    depth = w_ada.shape[0]
    topk = min(TOPK_MAX, s // 4)
    x2 = x.reshape(s, d)
    c8 = jnp.pad(c, ((0, SUBLANE - bsz), (0, 0)))
    for l in range(depth):
        mod = _mod_call(c8, w_ada[l], b_ada[l].reshape(1, -1))
        qbd, k, vt, qi, ki, wit, u, ga, gp = _proj_call(
            x2, mod, norm1_g[l].reshape(1, -1),
            w_in[l][:, 0:IN_IDX0].astype(bf16), w_in[l][:, IN_IDX1:IN_END].astype(bf16),
            jnp.tile(q_norm_g[l], N_HEADS).reshape(1, -1),
            jnp.tile(k_norm_g[l], N_HEADS).reshape(1, -1), _pack_w_idx(w_in[l]), tm=512)
        attn = _attn_call(qbd, qi, wit, k, vt, ki, topk)
        x2 = _mix_call(x2, attn, u, ga, gp, mod, w_attn_br[l].astype(bf16),
                       w_pool_grp[l].astype(bf16), pool_scale[l].reshape(1, -1),
                       w_pool_br[l].astype(bf16), w_out[l].astype(bf16), tm=512)
        x2 = _ffn_call(x2, mod, norm2_g[l].reshape(1, -1), w_up[l].astype(bf16), conv_w[l],
                       conv_b[l].reshape(1, -1), w_down[l].astype(bf16), tm=512)
    return x2.reshape(bsz, s, d)
```

```python
import functools

import jax
import jax.numpy as jnp
from jax import lax
from jax.experimental import pallas as pl
from jax.experimental.pallas import tpu as pltpu

f32 = jnp.float32
bf16 = jnp.bfloat16
i32 = jnp.int32

D_MODEL = 1024
N_HEADS = 8
HEAD_DIM = 64
ATTN_WIDTH = N_HEADS * HEAD_DIM
IDX_HEADS = 4
IDX_DIM = 64
TOPK_MAX = 256
POOL_WINDOWS = (2, 4, 8, 16)
POOL_GROUP_DIM = 128
POOL_WIDTH = 512
D_FF = 2816
EPS = 1e-6
N_MOD = 6

LANE = 128
SUBLANE = 8
VMEM_LIMIT = 58 * 1024 * 1024

LOG2E = 1.4426950408889634
INT_MIN = -2147483648
NEG_INF = float("-inf")

IN_IDX0 = 3 * ATTN_WIDTH
IN_IDX1 = IN_IDX0 + IDX_HEADS * IDX_DIM + IDX_DIM + IDX_HEADS
IN_END = IN_IDX1 + POOL_WIDTH + 2 * D_MODEL
C_Q, C_K, C_V, C_QKV = 0, ATTN_WIDTH, 2 * ATTN_WIDTH, 3 * ATTN_WIDTH
C_U, C_GA, C_GP, C_REST = 0, POOL_WIDTH, POOL_WIDTH + D_MODEL, POOL_WIDTH + 2 * D_MODEL

TQ = 128
TK = 256
N_PAIR = N_HEADS // 2
IDX_K = 4 * LANE


def _sigmoid(x):
    return 1.0 / (1.0 + jnp.exp(-x))


def _rms_modulate(x, g, shift, scale):
    y = x * lax.rsqrt(jnp.mean(x * x, axis=-1, keepdims=True) + EPS)
    return (y * g) * (1.0 + scale) + shift


def _mod_kernel(c_ref, w_ref, b_ref, o_ref):
    c = c_ref[...]
    sc = c * _sigmoid(c)
    o_ref[...] = jnp.dot(sc, w_ref[...], precision=lax.Precision.HIGHEST,
                         preferred_element_type=f32) + b_ref[...]


def _mod_call(c8, w_ada, b_ada):
    n = w_ada.shape[1]
    tn = 1024
    return pl.pallas_call(
        _mod_kernel,
        grid=(n // tn,),
        in_specs=[pl.BlockSpec((SUBLANE, D_MODEL), lambda j: (0, 0)),
                  pl.BlockSpec((D_MODEL, tn), lambda j: (0, j)),
                  pl.BlockSpec((1, tn), lambda j: (0, j))],
        out_specs=pl.BlockSpec((SUBLANE, tn), lambda j: (0, j)),
        out_shape=jax.ShapeDtypeStruct((SUBLANE, n), f32),
        name="mod",
    )(c8, w_ada, b_ada)


def _split3(x):
    hi = x.astype(bf16).astype(f32)
    r = x - hi
    mid = r.astype(bf16).astype(f32)
    lo = (r - mid).astype(bf16).astype(f32)
    return hi, mid, lo


def _head_norm(z, g, bd):
    z2 = z * z
    hi = z2.astype(bf16)
    lo = (z2 - hi.astype(f32)).astype(bf16)
    ms = jnp.dot(hi, bd, preferred_element_type=f32) + jnp.dot(lo, bd, preferred_element_type=f32)
    return (z * lax.rsqrt(ms + EPS)) * g


def _proj_kernel(x_ref, mod_ref, g1_ref, wa_ref, wb_ref, qg_ref, kg_ref, widx_ref,
                 qbd_ref, k_ref, vt_ref, qi_ref, ki_ref, wit_ref, u_ref, ga_ref, gp_ref, *, tm):
    x = x_ref[...]
    shift = mod_ref[0:1, 0:D_MODEL]
    scale = mod_ref[0:1, D_MODEL:2 * D_MODEL]
    h = _rms_modulate(x, g1_ref[...], shift, scale)
    hb = h.astype(bf16)
    proj = jnp.dot(hb, wa_ref[...], preferred_element_type=f32)
    rest = jnp.dot(hb, wb_ref[...], preferred_element_type=f32)

    r = lax.broadcasted_iota(i32, (ATTN_WIDTH, ATTN_WIDTH), 0)
    c = lax.broadcasted_iota(i32, (ATTN_WIDTH, ATTN_WIDTH), 1)
    head_shift = HEAD_DIM.bit_length() - 1
    bd = jnp.where((r >> head_shift) == (c >> head_shift), 1.0 / HEAD_DIM, 0.0).astype(bf16)

    q = _head_norm(proj[:, C_Q:C_K], qg_ref[...], bd) * (HEAD_DIM ** -0.5 * LOG2E)
    k = _head_norm(proj[:, C_K:C_V], kg_ref[...], bd)
    k_ref[...] = k.astype(bf16)

    low = lax.broadcasted_iota(i32, (TQ, LANE), 1) < HEAD_DIM
    low_tm = lax.broadcasted_iota(i32, (tm, LANE), 1) < HEAD_DIM
    pidx = jnp.dot(h, widx_ref[...], precision=lax.Precision.HIGHEST, preferred_element_type=f32)
    qi = pidx[:, 0:IDX_HEADS * IDX_DIM] * (IDX_DIM ** -0.5)
    up64 = lambda z: pltpu.roll(z, HEAD_DIM, axis=1)
    for g in range(tm // TQ):
        rows = slice(g * TQ, (g + 1) * TQ)
        for p in range(N_PAIR):
            qp = q[rows, p * LANE:(p + 1) * LANE]
            bd_q = jnp.concatenate([jnp.where(low, qp, 0.0), jnp.where(low, 0.0, qp)], axis=0)
            qbd_ref[g, p] = bd_q.T.astype(bf16)
        for hh in range(IDX_HEADS):
            seg = qi[rows, (hh // 2) * LANE:(hh // 2 + 1) * LANE]
            own = low if hh % 2 == 0 else jnp.logical_not(low)
            qh, qm, ql = _split3(jnp.where(own, seg, 0.0))
            both = lambda z: z + up64(z)
            in_low = lambda z: jnp.where(low, both(z), 0.0)
            hcols = slice(hh * TQ, (hh + 1) * TQ)
            for kt, term in enumerate((both(qh), both(qm), in_low(ql), in_low(qh))):
                qi_ref[g, kt * LANE:(kt + 1) * LANE, hcols] = term.T.astype(bf16)

    v = proj[:, C_V:C_QKV]
    vt = v.T.astype(bf16)
    for cc in range(tm // TK):
        for hh in range(N_HEADS):
            vt_ref[cc, hh] = vt[hh * HEAD_DIM:(hh + 1) * HEAD_DIM, cc * TK:(cc + 1) * TK]

    kw = pidx[:, IDX_HEADS * IDX_DIM:3 * LANE]
    kh, km, kl = _split3(jnp.where(low_tm, kw, 0.0))
    ki_ref[:, 0:LANE] = (kh + up64(km)).astype(bf16)
    ki_ref[:, LANE:2 * LANE] = kl.astype(bf16)
    wt = (kw * (IDX_HEADS ** -0.5)).T
    wit_ref[...] = wt[HEAD_DIM:HEAD_DIM + SUBLANE, :]
    u_ref[...] = rest[:, C_U:C_GA]
    ga_ref[...] = _sigmoid(rest[:, C_GA:C_GP]).astype(bf16)
    gp_ref[...] = _sigmoid(rest[:, C_GP:C_REST]).astype(bf16)


def _proj_call(x2, mod, g1, w_qkv, w_rest, qg, kg, widx, tm):
    s = x2.shape[0]
    nq = s // TQ
    const = lambda i: (0, 0)
    return pl.pallas_call(
        functools.partial(_proj_kernel, tm=tm),
        grid=(s // tm,),
        in_specs=[pl.BlockSpec((tm, D_MODEL), lambda i: (i, 0)),
                  pl.BlockSpec((SUBLANE, N_MOD * D_MODEL), const),
                  pl.BlockSpec((1, D_MODEL), const),
                  pl.BlockSpec((D_MODEL, C_QKV), const),
                  pl.BlockSpec((D_MODEL, C_REST), const),
                  pl.BlockSpec((1, ATTN_WIDTH), const),
                  pl.BlockSpec((1, ATTN_WIDTH), const),
                  pl.BlockSpec((D_MODEL, 3 * LANE), const)],
        out_specs=[pl.BlockSpec((tm // TQ, N_PAIR, LANE, 2 * TQ), lambda i: (i, 0, 0, 0)),
                   pl.BlockSpec((tm, ATTN_WIDTH), lambda i: (i, 0)),
                   pl.BlockSpec((tm // TK, N_HEADS, HEAD_DIM, TK), lambda i: (i, 0, 0, 0)),
                   pl.BlockSpec((tm // TQ, IDX_K, IDX_HEADS * TQ), lambda i: (i, 0, 0)),
                   pl.BlockSpec((tm, 2 * LANE), lambda i: (i, 0)),
                   pl.BlockSpec((SUBLANE, tm), lambda i: (0, i)),
                   pl.BlockSpec((tm, POOL_WIDTH), lambda i: (i, 0)),
                   pl.BlockSpec((tm, D_MODEL), lambda i: (i, 0)),
                   pl.BlockSpec((tm, D_MODEL), lambda i: (i, 0))],
        out_shape=[jax.ShapeDtypeStruct((nq, N_PAIR, LANE, 2 * TQ), bf16),
                   jax.ShapeDtypeStruct((s, ATTN_WIDTH), bf16),
                   jax.ShapeDtypeStruct((s // TK, N_HEADS, HEAD_DIM, TK), bf16),
                   jax.ShapeDtypeStruct((nq, IDX_K, IDX_HEADS * TQ), bf16),
                   jax.ShapeDtypeStruct((s, 2 * LANE), bf16),
                   jax.ShapeDtypeStruct((SUBLANE, s), f32),
                   jax.ShapeDtypeStruct((s, POOL_WIDTH), f32),
                   jax.ShapeDtypeStruct((s, D_MODEL), bf16),
                   jax.ShapeDtypeStruct((s, D_MODEL), bf16)],
        compiler_params=pltpu.CompilerParams(dimension_semantics=("parallel",),
                                             vmem_limit_bytes=VMEM_LIMIT),
        name="proj",
    )(x2, mod, g1, w_qkv, w_rest, qg, kg, widx)


def _slope2(h):
    return (2.0 ** (-8.0 * (h + 1) / N_HEADS)) * LOG2E


def _tree(op, xs):
    xs = list(xs)
    while len(xs) > 1:
        nxt = [op(xs[a], xs[a + 1]) for a in range(0, len(xs) - 1, 2)]
        if len(xs) % 2:
            nxt.append(xs[-1])
        xs = nxt
    return xs[0]


def _row_groups(x):
    return [x[j * SUBLANE:(j + 1) * SUBLANE] for j in range(x.shape[0] // SUBLANE)]


N_CNT_ACC = 8
N_STAGE = 4
EARLY_CANDS = (0, 1 << 30, INT_MIN + (1 << 30))
PV_ROWS = HEAD_DIM + 16


def _attn_kernel(qbd_ref, qi_ref, wi_ref, k_ref, vt_ref, ki_ref, o_ref,
                 keys_ref, cnt0_ref, pos_ref, slf_ref, il0_ref, il1_ref,
                 pre0_ref, pre1_ref, pre2_ref, pre3_ref, sm0_ref, sm1_ref, cmax0_ref, cmax1_ref,
                 p0_ref, p1_ref, alpha0_ref, alpha1_ref, m_ref, acc_ref, *, topk):
    il_ref = (il0_ref, il1_ref)
    pre_ref = (pre0_ref, pre1_ref, pre2_ref, pre3_ref)
    sm_ref = (sm0_ref, sm1_ref)
    cmax_ref = (cmax0_ref, cmax1_ref)
    p_ref = (p0_ref, p1_ref)
    alpha_ref = (alpha0_ref, alpha1_ref)
    i = pl.program_id(0)
    nsc = (i + 4) >> 2
    nch = 2 * nsc
    nbody = (i + 8) >> 3
    n_causal = (i + 2) >> 1
    n_full, n_tail = n_causal >> 2, n_causal & 3

    def chunk_start(c):
        return pl.multiple_of(c * TK, TK)

    def pipeline(produce, consume, carry):
        for j in range(N_STAGE):
            produce(j, j)

        def body(b, carry):
            c0 = N_STAGE * b
            for j in range(N_STAGE):
                carry = consume(c0 + j, j, carry)
                produce(c0 + N_STAGE + j, j)
            return carry

        carry = lax.fori_loop(0, nbody - 1, body, carry)
        c0 = N_STAGE * (nbody - 1)
        for j in range(N_STAGE):
            carry = consume(c0 + j, j, carry)
        return carry

    @pl.when(i == 0)
    def _():
        col = lax.broadcasted_iota(i32, (TK, LANE), 1)
        row = lax.broadcasted_iota(i32, (TK, LANE), 0).astype(f32)
        pos_ref[...] = jnp.where(col < 3, row, 0.0).astype(bf16)
        term = lax.broadcasted_iota(i32, (LANE, 2 * TQ), 0)
        lane = lax.broadcasted_iota(i32, (LANE, 2 * TQ), 1)
        for p in range(N_PAIR):
            sl = jnp.where(lane < TQ, _slope2(2 * p), _slope2(2 * p + 1)) + jnp.zeros((LANE, 2 * TQ), f32)
            hi, mid, lo = _split3(sl)
            slf_ref[p] = jnp.where(term == 0, hi, jnp.where(term == 1, mid,
                                   jnp.where(term == 2, lo, 0.0))).astype(bf16)

    qi = qi_ref[0]
    w = wi_ref[...]
    d0 = (lax.broadcasted_iota(i32, (TK, TQ), 1) - lax.broadcasted_iota(i32, (TK, TQ), 0))

    def idx_matmul(u, slot):
        rows = pl.ds(pl.multiple_of(u * 2 * TK, 2 * TK), 2 * TK)
        kx = ki_ref[rows, 0:LANE]
        ky = ki_ref[rows, LANE:2 * LANE]
        il_ref[slot][...] = jnp.dot(jnp.concatenate([kx, kx, kx, ky], axis=1), qi,
                                    preferred_element_type=f32)

    def idx_keys(c, slot, half):
        r0 = chunk_start(c)
        rows = slice(half * TK, (half + 1) * TK)
        sc = jnp.maximum(il_ref[slot][rows, 0:TQ], 0.0) * w[0:1, :]
        for hh in range(1, IDX_HEADS):
            sc = sc + jnp.maximum(il_ref[slot][rows, hh * TQ:(hh + 1) * TQ], 0.0) * w[hh:hh + 1, :]
        b = lax.bitcast_convert_type(sc, i32)
        key = jnp.where(b < 0, -(b & 0x7FFFFFFF), b)
        valid = d0 >= (r0 - i * TQ)
        key = jnp.where(valid, key, INT_MIN)
        keys_ref[pl.ds(r0, TK), :] = key
        for n, cand in enumerate(EARLY_CANDS):
            cnt0_ref[n] = cnt0_ref[n] + _tree(jnp.add, _row_groups(jnp.where(key >= cand, 1, 0)))

    cnt0_ref[...] = jnp.zeros(cnt0_ref.shape, i32)

    idx_matmul(0, 0)
    idx_matmul(1, 1)

    def idx_body(b, carry):
        c0 = N_STAGE * b
        idx_keys(c0, 0, 0)
        idx_keys(c0 + 1, 0, 1)
        idx_matmul(2 * b + 2, 0)
        idx_keys(c0 + 2, 1, 0)
        idx_keys(c0 + 3, 1, 1)
        idx_matmul(2 * b + 3, 1)
        return carry

    lax.fori_loop(0, nbody - 1, idx_body, 0)
    c_last = N_STAGE * (nbody - 1)
    for j in range(N_STAGE):
        idx_keys(c_last + j, j // 2, j % 2)

    def count_ge(cand):
        def add_chunk(c, accs):
            ind = jnp.where(keys_ref[pl.ds(chunk_start(c), TK), :] >= cand, 1, 0)
            for j, g in enumerate(_row_groups(ind)):
                accs[j % N_CNT_ACC] = accs[j % N_CNT_ACC] + g

        def body(b, accs):
            accs = list(accs)
            for part in range(N_STAGE):
                add_chunk(b * N_STAGE + part, accs)
            return tuple(accs)

        def tail(j, accs):
            accs = list(accs)
            add_chunk(n_full * N_STAGE + j, accs)
            return tuple(accs)

        accs = lax.fori_loop(0, n_full, body,
                             tuple(jnp.zeros((SUBLANE, TQ), i32) for _ in range(N_CNT_ACC)))
        accs = lax.fori_loop(0, n_tail, tail, accs)
        return jnp.sum(_tree(jnp.add, accs), axis=0, keepdims=True)

    def bit_step(bi, st):
        t, above = st
        cand = t + lax.shift_left(jnp.int32(1), 31 - bi)
        cnt = count_ge(cand)
        up = cnt >= topk
        return jnp.where(up, cand, t), jnp.where(up, above, cnt)

    c_zero, c_pos, c_neg = (jnp.sum(cnt0_ref[n], axis=0, keepdims=True) for n in range(len(EARLY_CANDS)))
    up0 = c_zero >= topk
    t = jnp.where(up0, 0, INT_MIN)
    above = jnp.where(up0, 0, c_zero)
    c_one = jnp.where(up0, c_pos, c_neg)
    up1 = c_one >= topk
    t = jnp.where(up1, t + (1 << 30), t)
    above = jnp.where(up1, above, c_one)
    t, above = lax.fori_loop(2, 32, bit_step, (t, above))
    t = jnp.maximum(t, INT_MIN + 1)
    r_tie = (topk - above).astype(f32)

    tri = jnp.where(lax.broadcasted_iota(i32, (TK, TK), 0) >= lax.broadcasted_iota(i32, (TK, TK), 1),
                    1.0, 0.0).astype(bf16)

    def tie_rank(c, slot):
        e = jnp.where(keys_ref[pl.ds(chunk_start(c), TK), :] == t, 1.0, 0.0).astype(bf16)
        pre_ref[slot][...] = jnp.dot(tri, e, preferred_element_type=f32)

    def mask_out(c, slot, rank):
        r0 = chunk_start(c)
        kk = keys_ref[pl.ds(r0, TK), :]
        pre = pre_ref[slot][...] + rank
        nm = jnp.where(kk > t, 0.0, jnp.where(kk == t, jnp.where(pre <= r_tie, 0.0, NEG_INF), NEG_INF))
        keys_ref[pl.ds(r0, TK), :] = lax.bitcast_convert_type(nm, i32)
        return pre[TK - 1:TK, :]

    pipeline(tie_rank, mask_out, jnp.zeros((1, TQ), f32))

    m_ref[...] = jnp.full(m_ref.shape, NEG_INF, f32)
    acc_ref[...] = jnp.zeros(acc_ref.shape, f32)
    lane2 = lax.broadcasted_iota(i32, (1, 2 * TQ), 1)
    ones_rows = jnp.ones((PV_ROWS - HEAD_DIM, TK), bf16)

    def logits(c, slot):
        rows = pl.ds(chunk_start(c), TK)
        nm = lax.bitcast_convert_type(keys_ref[rows, :], f32)
        nm2 = jnp.concatenate([nm, nm], axis=1)
        for p in range(N_PAIR):
            lhs = jnp.concatenate([k_ref[rows, p * LANE:(p + 1) * LANE], pos_ref[...]], axis=1)
            rhs = jnp.concatenate([qbd_ref[0, p], slf_ref[p]], axis=0)
            sm = jnp.dot(lhs, rhs, preferred_element_type=f32) + nm2
            sm_ref[slot][p] = sm
            cmax_ref[slot][p] = jnp.max(_tree(jnp.maximum, _row_groups(sm)), axis=0, keepdims=True)

    def probs(c, slot):
        r0f = jnp.asarray(c * TK, dtype=f32)
        for p in range(N_PAIR):
            coff = jnp.where(lane2 < TQ, _slope2(2 * p), _slope2(2 * p + 1)) * r0f
            m_old = m_ref[p]
            m_new = jnp.maximum(m_old, cmax_ref[slot][p] + coff)
            m_safe = jnp.where(m_new == NEG_INF, 0.0, m_new)
            alpha_ref[slot][p] = jnp.where(m_old == NEG_INF, 0.0, jnp.exp2(m_old - m_safe))
            p_ref[slot][p] = jnp.exp2((sm_ref[slot][p] - (m_safe - coff)).astype(bf16))
            m_ref[p] = m_new

    def weighted_sum(c, slot):
        for h in range(N_HEADS):
            lanes = slice((h % 2) * TQ, (h % 2 + 1) * TQ)
            lhs = jnp.concatenate([vt_ref[c, h], ones_rows], axis=0)
            pv = jnp.dot(lhs, p_ref[slot][h // 2, :, lanes], preferred_element_type=f32)
            acc_ref[h] = acc_ref[h] * alpha_ref[slot][h // 2, :, lanes] + pv

    logits(0, 0)
    probs(0, 0)
    logits(1, 1)

    def attn_step(c0):
        probs(c0 + 1, 1)
        logits(c0 + 2, 0)
        logits(c0 + 3, 1)
        weighted_sum(c0, 0)
        probs(c0 + 2, 0)
        weighted_sum(c0 + 1, 1)

    n_quad = (nsc - 1) >> 2

    def attn_body4(b, carry):
        for j in range(4):
            attn_step(8 * b + 2 * j)
        return carry

    def attn_body1(j, carry):
        attn_step(8 * n_quad + 2 * j)
        return carry

    lax.fori_loop(0, n_quad, attn_body4, 0)
    lax.fori_loop(0, (nsc - 1) & 3, attn_body1, 0)
    weighted_sum(nch - 2, 0)
    probs(nch - 1, 1)
    weighted_sum(nch - 1, 1)

    outs = []
    for h in range(N_HEADS):
        a = acc_ref[h]
        outs.append(a[0:HEAD_DIM] / a[HEAD_DIM:HEAD_DIM + 1])
    o_ref[...] = jnp.concatenate(outs, axis=0).T.astype(bf16)


def _attn_call(qbd, qi, wit, k, vt, ki, topk):
    s = k.shape[0]
    nq = s // TQ
    whole = pl.BlockSpec(memory_space=pltpu.VMEM)
    il = [pltpu.VMEM((2 * TK, IDX_HEADS * TQ), f32)] * 2
    pre = [pltpu.VMEM((TK, TQ), f32)] * N_STAGE
    return pl.pallas_call(
        functools.partial(_attn_kernel, topk=topk),
        grid=(nq,),
        in_specs=[pl.BlockSpec((1, N_PAIR, LANE, 2 * TQ), lambda i: (i, 0, 0, 0)),
                  pl.BlockSpec((1, IDX_K, IDX_HEADS * TQ), lambda i: (i, 0, 0)),
                  pl.BlockSpec((SUBLANE, TQ), lambda i: (0, i)),
                  whole, whole, whole],
        out_specs=pl.BlockSpec((TQ, ATTN_WIDTH), lambda i: (i, 0)),
        out_shape=jax.ShapeDtypeStruct((s, ATTN_WIDTH), bf16),
        scratch_shapes=[pltpu.VMEM((s, TQ), i32),
                        pltpu.VMEM((len(EARLY_CANDS), SUBLANE, TQ), i32),
                        pltpu.VMEM((TK, LANE), bf16),
                        pltpu.VMEM((N_PAIR, LANE, 2 * TQ), bf16),
                        *il, *pre,
                        pltpu.VMEM((N_PAIR, TK, 2 * TQ), f32),
                        pltpu.VMEM((N_PAIR, TK, 2 * TQ), f32),
                        pltpu.VMEM((N_PAIR, 1, 2 * TQ), f32),
                        pltpu.VMEM((N_PAIR, 1, 2 * TQ), f32),
                        pltpu.VMEM((N_PAIR, TK, 2 * TQ), bf16),
                        pltpu.VMEM((N_PAIR, TK, 2 * TQ), bf16),
                        pltpu.VMEM((N_PAIR, 1, 2 * TQ), f32),
                        pltpu.VMEM((N_PAIR, 1, 2 * TQ), f32),
                        pltpu.VMEM((N_PAIR, 1, 2 * TQ), f32),
                        pltpu.VMEM((N_HEADS, PV_ROWS, TQ), f32)],
        compiler_params=pltpu.CompilerParams(dimension_semantics=("arbitrary",),
                                             vmem_limit_bytes=VMEM_LIMIT),
        name="attn",
    )(qbd, qi, wit, k, vt, ki)


HALO_POOL = 16


def _mix_kernel(x_ref, attn_ref, u_ref, uh_ref, ga_ref, gp_ref, mod_ref,
                wab_ref, wg_ref, ps_ref, wpb_ref, wo_ref, o_ref, *, tm):
    i = pl.program_id(0)
    y_attn = jnp.dot(attn_ref[...], wab_ref[...], preferred_element_type=f32)

    u = u_ref[...]
    halo = jnp.where(i > 0, uh_ref[...], 0.0)
    a = jnp.concatenate([halo, u], axis=0)
    tpos = (i * tm + lax.broadcasted_iota(i32, (tm, POOL_GROUP_DIM), 0) + 1).astype(f32)
    mixed = []
    for g, wdw in enumerate(POOL_WINDOWS):
        ag = a[:, g * POOL_GROUP_DIM:(g + 1) * POOL_GROUP_DIM]
        ug = ag[HALO_POOL:HALO_POOL + tm]
        ssum = ug
        for j in range(1, wdw):
            ssum = ssum + ag[HALO_POOL - j:HALO_POOL - j + tm]
        pooled = ssum / jnp.minimum(tpos, float(wdw)) - ug
        mixed.append(jnp.dot(pooled.astype(bf16), wg_ref[g], preferred_element_type=f32))
    mixed = jnp.concatenate(mixed, axis=1) * ps_ref[...]
    y_pool = jnp.dot(mixed.astype(bf16), wpb_ref[...], preferred_element_type=f32)

    merged = ga_ref[...].astype(f32) * y_attn + gp_ref[...].astype(f32) * y_pool
    o = jnp.dot(merged.astype(bf16), wo_ref[...], preferred_element_type=f32)
    gate = mod_ref[0:1, 2 * D_MODEL:3 * D_MODEL]
    o_ref[...] = x_ref[...] + gate * o


def _mix_call(x2, attn, u, ga, gp, mod, wab, wg, ps, wpb, wo, tm):
    s = x2.shape[0]
    const2 = lambda i: (0, 0)
    hb = tm // HALO_POOL
    return pl.pallas_call(
        functools.partial(_mix_kernel, tm=tm),
        grid=(s // tm,),
        in_specs=[pl.BlockSpec((tm, D_MODEL), lambda i: (i, 0)),
                  pl.BlockSpec((tm, ATTN_WIDTH), lambda i: (i, 0)),
                  pl.BlockSpec((tm, POOL_WIDTH), lambda i: (i, 0)),
                  pl.BlockSpec((HALO_POOL, POOL_WIDTH), lambda i: (jnp.maximum(i * hb - 1, 0), 0)),
                  pl.BlockSpec((tm, D_MODEL), lambda i: (i, 0)),
                  pl.BlockSpec((tm, D_MODEL), lambda i: (i, 0)),
                  pl.BlockSpec((SUBLANE, N_MOD * D_MODEL), const2),
                  pl.BlockSpec((ATTN_WIDTH, D_MODEL), const2),
                  pl.BlockSpec((len(POOL_WINDOWS), POOL_GROUP_DIM, POOL_GROUP_DIM), lambda i: (0, 0, 0)),
                  pl.BlockSpec((1, POOL_WIDTH), const2),
                  pl.BlockSpec((POOL_WIDTH, D_MODEL), const2),
                  pl.BlockSpec((D_MODEL, D_MODEL), const2)],
        out_specs=pl.BlockSpec((tm, D_MODEL), lambda i: (i, 0)),
        out_shape=jax.ShapeDtypeStruct((s, D_MODEL), f32),
        compiler_params=pltpu.CompilerParams(dimension_semantics=("parallel",),
                                             vmem_limit_bytes=VMEM_LIMIT),
        name="mix",
    )(x2, attn, u, u, ga, gp, mod, wab, wg, ps, wpb, wo)


HALO_CONV = 8


def _ffn_kernel(x_ref, xh_ref, mod_ref, g2_ref, wup_ref, cw_ref, cb_ref, wdn_ref, o_ref, *, tm):
    i = pl.program_id(0)
    shift = mod_ref[0:1, 3 * D_MODEL:4 * D_MODEL]
    scale = mod_ref[0:1, 4 * D_MODEL:5 * D_MODEL]
    gate = mod_ref[0:1, 5 * D_MODEL:6 * D_MODEL]
    g2 = g2_ref[...]
    x = x_ref[...]
    h = _rms_modulate(x, g2, shift, scale)
    hh = jnp.where(i > 0, _rms_modulate(xh_ref[...], g2, shift, scale), 0.0)
    ha = jnp.concatenate([hh, h], axis=0).astype(bf16)
    up = jnp.dot(ha, wup_ref[...], preferred_element_type=f32)
    cw = cw_ref[...]
    y = cb_ref[...] + cw[0:1, :] * up[HALO_CONV - 2:HALO_CONV - 2 + tm]
    y = y + cw[1:2, :] * up[HALO_CONV - 1:HALO_CONV - 1 + tm]
    y = y + cw[2:3, :] * up[HALO_CONV:HALO_CONV + tm]
    a = y[:, 0:D_FF]
    b = y[:, D_FF:2 * D_FF]
    gated = (a * _sigmoid(a)) * b
    o = jnp.dot(gated.astype(bf16), wdn_ref[...], preferred_element_type=f32)
    o_ref[...] = x + gate * o


def _ffn_call(x1, mod, g2, wup, cw, cb, wdn, tm):
    s = x1.shape[0]
    const2 = lambda i: (0, 0)
    hb = tm // HALO_CONV
    return pl.pallas_call(
        functools.partial(_ffn_kernel, tm=tm),
        grid=(s // tm,),
        in_specs=[pl.BlockSpec((tm, D_MODEL), lambda i: (i, 0)),
                  pl.BlockSpec((HALO_CONV, D_MODEL), lambda i: (jnp.maximum(i * hb - 1, 0), 0)),
                  pl.BlockSpec((SUBLANE, N_MOD * D_MODEL), const2),
                  pl.BlockSpec((1, D_MODEL), const2),
                  pl.BlockSpec((D_MODEL, 2 * D_FF), const2, pipeline_mode=pl.Buffered(1)),
                  pl.BlockSpec((3, 2 * D_FF), const2),
                  pl.BlockSpec((1, 2 * D_FF), const2),
                  pl.BlockSpec((D_FF, D_MODEL), const2, pipeline_mode=pl.Buffered(1))],
        out_specs=pl.BlockSpec((tm, D_MODEL), lambda i: (i, 0)),
        out_shape=jax.ShapeDtypeStruct((s, D_MODEL), f32),
        compiler_params=pltpu.CompilerParams(dimension_semantics=("parallel",),
                                             vmem_limit_bytes=VMEM_LIMIT),
        name="ffn",
    )(x1, x1, mod, g2, wup, cw, cb, wdn)


def _pack_w_idx(w):
    return jnp.pad(w[:, IN_IDX0:IN_IDX1], ((0, 0), (0, 3 * LANE - (IN_IDX1 - IN_IDX0))))


def kernel(x, c, w_ada, b_ada, norm1_g, w_in, q_norm_g, k_norm_g, w_attn_br, w_pool_grp,
           pool_scale, w_pool_br, w_out, norm2_g, w_up, conv_w, conv_b, w_down):
    bsz, s, d = x.shape
    assert bsz == 1 and d == D_MODEL and s % (N_STAGE * TK) == 0
    depth = w_ada.shape[0]
    topk = min(TOPK_MAX, s // 4)
    x2 = x.reshape(s, d)
    c8 = jnp.pad(c, ((0, SUBLANE - bsz), (0, 0)))
    for l in range(depth):
        mod = _mod_call(c8, w_ada[l], b_ada[l].reshape(1, -1))
        qbd, k, vt, qi, ki, wit, u, ga, gp = _proj_call(
            x2, mod, norm1_g[l].reshape(1, -1),
            w_in[l][:, 0:IN_IDX0].astype(bf16), w_in[l][:, IN_IDX1:IN_END].astype(bf16),
            jnp.tile(q_norm_g[l], N_HEADS).reshape(1, -1),
            jnp.tile(k_norm_g[l], N_HEADS).reshape(1, -1), _pack_w_idx(w_in[l]), tm=512)
        attn = _attn_call(qbd, qi, wit, k, vt, ki, topk)
        x2 = _mix_call(x2, attn, u, ga, gp, mod, w_attn_br[l].astype(bf16),
                       w_pool_grp[l].astype(bf16), pool_scale[l].reshape(1, -1),
                       w_pool_br[l].astype(bf16), w_out[l].astype(bf16), tm=512)
        x2 = _ffn_call(x2, mod, norm2_g[l].reshape(1, -1), w_up[l].astype(bf16), conv_w[l],
                       conv_b[l].reshape(1, -1), w_down[l].astype(bf16), tm=512)
    return x2.reshape(bsz, s, d)
```

```python
import functools

import jax
import jax.numpy as jnp
from jax import lax
from jax.experimental import pallas as pl
from jax.experimental.pallas import tpu as pltpu

f32 = jnp.float32
bf16 = jnp.bfloat16
i32 = jnp.int32

D_MODEL = 1024
N_HEADS = 8
HEAD_DIM = 64
ATTN_WIDTH = N_HEADS * HEAD_DIM
IDX_HEADS = 4
IDX_DIM = 64
TOPK_MAX = 256
POOL_WINDOWS = (2, 4, 8, 16)
POOL_GROUP_DIM = 128
POOL_WIDTH = 512
D_FF = 2816
EPS = 1e-6
N_MOD = 6

LANE = 128
SUBLANE = 8
VMEM_LIMIT = 58 * 1024 * 1024

LOG2E = 1.4426950408889634
INT_MIN = -2147483648
NEG_INF = float("-inf")

IN_IDX0 = 3 * ATTN_WIDTH
IN_IDX1 = IN_IDX0 + IDX_HEADS * IDX_DIM + IDX_DIM + IDX_HEADS
IN_END = IN_IDX1 + POOL_WIDTH + 2 * D_MODEL
C_Q, C_K, C_V, C_QKV = 0, ATTN_WIDTH, 2 * ATTN_WIDTH, 3 * ATTN_WIDTH
C_U, C_GA, C_GP, C_REST = 0, POOL_WIDTH, POOL_WIDTH + D_MODEL, POOL_WIDTH + 2 * D_MODEL

TQ = 128
TK = 256
N_PAIR = N_HEADS // 2
IDX_K = 4 * LANE


def _sigmoid(x):
    return 1.0 / (1.0 + jnp.exp(-x))


def _rms_modulate(x, g, shift, scale):
    y = x * lax.rsqrt(jnp.mean(x * x, axis=-1, keepdims=True) + EPS)
    return (y * g) * (1.0 + scale) + shift


def _mod_kernel(c_ref, w_ref, b_ref, o_ref):
    c = c_ref[...]
    sc = c * _sigmoid(c)
    o_ref[...] = jnp.dot(sc, w_ref[...], precision=lax.Precision.HIGHEST,
                         preferred_element_type=f32) + b_ref[...]


def _mod_call(c8, w_ada, b_ada):
    n = w_ada.shape[1]
    tn = 1024
    return pl.pallas_call(
        _mod_kernel,
        grid=(n // tn,),
        in_specs=[pl.BlockSpec((SUBLANE, D_MODEL), lambda j: (0, 0)),
                  pl.BlockSpec((D_MODEL, tn), lambda j: (0, j)),
                  pl.BlockSpec((1, tn), lambda j: (0, j))],
        out_specs=pl.BlockSpec((SUBLANE, tn), lambda j: (0, j)),
        out_shape=jax.ShapeDtypeStruct((SUBLANE, n), f32),
        name="mod",
    )(c8, w_ada, b_ada)


def _split3(x):
    hi = x.astype(bf16).astype(f32)
    r = x - hi
    mid = r.astype(bf16).astype(f32)
    lo = (r - mid).astype(bf16).astype(f32)
    return hi, mid, lo


def _head_norm(z, g, bd):
    z2 = z * z
    hi = z2.astype(bf16)
    lo = (z2 - hi.astype(f32)).astype(bf16)
    ms = jnp.dot(hi, bd, preferred_element_type=f32) + jnp.dot(lo, bd, preferred_element_type=f32)
    return (z * lax.rsqrt(ms + EPS)) * g


def _proj_kernel(x_ref, mod_ref, g1_ref, wa_ref, wb_ref, qg_ref, kg_ref, widx_ref,
                 qbd_ref, k_ref, vt_ref, qi_ref, ki_ref, wit_ref, u_ref, ga_ref, gp_ref, *, tm):
    x = x_ref[...]
    shift = mod_ref[0:1, 0:D_MODEL]
    scale = mod_ref[0:1, D_MODEL:2 * D_MODEL]
    h = _rms_modulate(x, g1_ref[...], shift, scale)
    hb = h.astype(bf16)
    proj = jnp.dot(hb, wa_ref[...], preferred_element_type=f32)
    rest = jnp.dot(hb, wb_ref[...], preferred_element_type=f32)

    r = lax.broadcasted_iota(i32, (ATTN_WIDTH, ATTN_WIDTH), 0)
    c = lax.broadcasted_iota(i32, (ATTN_WIDTH, ATTN_WIDTH), 1)
    head_shift = HEAD_DIM.bit_length() - 1
    bd = jnp.where((r >> head_shift) == (c >> head_shift), 1.0 / HEAD_DIM, 0.0).astype(bf16)

    q = _head_norm(proj[:, C_Q:C_K], qg_ref[...], bd) * (HEAD_DIM ** -0.5 * LOG2E)
    k = _head_norm(proj[:, C_K:C_V], kg_ref[...], bd)
    k_ref[...] = k.astype(bf16)

    low = lax.broadcasted_iota(i32, (TQ, LANE), 1) < HEAD_DIM
    low_tm = lax.broadcasted_iota(i32, (tm, LANE), 1) < HEAD_DIM
    pidx = jnp.dot(h, widx_ref[...], precision=lax.Precision.HIGHEST, preferred_element_type=f32)
    qi = pidx[:, 0:IDX_HEADS * IDX_DIM] * (IDX_DIM ** -0.5)
    up64 = lambda z: pltpu.roll(z, HEAD_DIM, axis=1)
    for g in range(tm // TQ):
        rows = slice(g * TQ, (g + 1) * TQ)
        for p in range(N_PAIR):
            qp = q[rows, p * LANE:(p + 1) * LANE]
            bd_q = jnp.concatenate([jnp.where(low, qp, 0.0), jnp.where(low, 0.0, qp)], axis=0)
            qbd_ref[g, p] = bd_q.T.astype(bf16)
        for hh in range(IDX_HEADS):
            seg = qi[rows, (hh // 2) * LANE:(hh // 2 + 1) * LANE]
            own = low if hh % 2 == 0 else jnp.logical_not(low)
            qh, qm, ql = _split3(jnp.where(own, seg, 0.0))
            both = lambda z: z + up64(z)
            in_low = lambda z: jnp.where(low, both(z), 0.0)
            hcols = slice(hh * TQ, (hh + 1) * TQ)
            for kt, term in enumerate((both(qh), both(qm), in_low(ql), in_low(qh))):
                qi_ref[g, kt * LANE:(kt + 1) * LANE, hcols] = term.T.astype(bf16)

    v = proj[:, C_V:C_QKV]
    vt = v.T.astype(bf16)
    for cc in range(tm // TK):
        for hh in range(N_HEADS):
            vt_ref[cc, hh] = vt[hh * HEAD_DIM:(hh + 1) * HEAD_DIM, cc * TK:(cc + 1) * TK]

    kw = pidx[:, IDX_HEADS * IDX_DIM:3 * LANE]
    kh, km, kl = _split3(jnp.where(low_tm, kw, 0.0))
    ki_ref[:, 0:LANE] = (kh + up64(km)).astype(bf16)
    ki_ref[:, LANE:2 * LANE] = kl.astype(bf16)
    wt = (kw * (IDX_HEADS ** -0.5)).T
    wit_ref[...] = wt[HEAD_DIM:HEAD_DIM + SUBLANE, :]
    u_ref[...] = rest[:, C_U:C_GA]
    ga_ref[...] = _sigmoid(rest[:, C_GA:C_GP]).astype(bf16)
    gp_ref[...] = _sigmoid(rest[:, C_GP:C_REST]).astype(bf16)


def _proj_call(x2, mod, g1, w_qkv, w_rest, qg, kg, widx, tm):
    s = x2.shape[0]
    nq = s // TQ
    const = lambda i: (0, 0)
    return pl.pallas_call(
        functools.partial(_proj_kernel, tm=tm),
        grid=(s // tm,),
        in_specs=[pl.BlockSpec((tm, D_MODEL), lambda i: (i, 0)),
                  pl.BlockSpec((SUBLANE, N_MOD * D_MODEL), const),
                  pl.BlockSpec((1, D_MODEL), const),
                  pl.BlockSpec((D_MODEL, C_QKV), const),
                  pl.BlockSpec((D_MODEL, C_REST), const),
                  pl.BlockSpec((1, ATTN_WIDTH), const),
                  pl.BlockSpec((1, ATTN_WIDTH), const),
                  pl.BlockSpec((D_MODEL, 3 * LANE), const)],
        out_specs=[pl.BlockSpec((tm // TQ, N_PAIR, LANE, 2 * TQ), lambda i: (i, 0, 0, 0)),
                   pl.BlockSpec((tm, ATTN_WIDTH), lambda i: (i, 0)),
                   pl.BlockSpec((tm // TK, N_HEADS, HEAD_DIM, TK), lambda i: (i, 0, 0, 0)),
                   pl.BlockSpec((tm // TQ, IDX_K, IDX_HEADS * TQ), lambda i: (i, 0, 0)),
                   pl.BlockSpec((tm, 2 * LANE), lambda i: (i, 0)),
                   pl.BlockSpec((SUBLANE, tm), lambda i: (0, i)),
                   pl.BlockSpec((tm, POOL_WIDTH), lambda i: (i, 0)),
                   pl.BlockSpec((tm, D_MODEL), lambda i: (i, 0)),
                   pl.BlockSpec((tm, D_MODEL), lambda i: (i, 0))],
        out_shape=[jax.ShapeDtypeStruct((nq, N_PAIR, LANE, 2 * TQ), bf16),
                   jax.ShapeDtypeStruct((s, ATTN_WIDTH), bf16),
                   jax.ShapeDtypeStruct((s // TK, N_HEADS, HEAD_DIM, TK), bf16),
                   jax.ShapeDtypeStruct((nq, IDX_K, IDX_HEADS * TQ), bf16),
                   jax.ShapeDtypeStruct((s, 2 * LANE), bf16),
                   jax.ShapeDtypeStruct((SUBLANE, s), f32),
                   jax.ShapeDtypeStruct((s, POOL_WIDTH), f32),
                   jax.ShapeDtypeStruct((s, D_MODEL), bf16),
                   jax.ShapeDtypeStruct((s, D_MODEL), bf16)],
        compiler_params=pltpu.CompilerParams(dimension_semantics=("parallel",),
                                             vmem_limit_bytes=VMEM_LIMIT),
        name="proj",
    )(x2, mod, g1, w_qkv, w_rest, qg, kg, widx)


def _slope2(h):
    return (2.0 ** (-8.0 * (h + 1) / N_HEADS)) * LOG2E


def _tree(op, xs):
    xs = list(xs)
    while len(xs) > 1:
        nxt = [op(xs[a], xs[a + 1]) for a in range(0, len(xs) - 1, 2)]
        if len(xs) % 2:
            nxt.append(xs[-1])
        xs = nxt
    return xs[0]


def _row_groups(x):
    return [x[j * SUBLANE:(j + 1) * SUBLANE] for j in range(x.shape[0] // SUBLANE)]


N_CNT_ACC = 8
N_STAGE = 4
ATTN_UNROLL = 8
EARLY_CANDS = (0, 1 << 30, INT_MIN + (1 << 30))
PV_ROWS = HEAD_DIM + 16


def _attn_kernel(qbd_ref, qi_ref, wi_ref, k_ref, vt_ref, ki_ref, o_ref,
                 keys_ref, cnt0_ref, pos_ref, slf_ref, il0_ref, il1_ref,
                 pre0_ref, pre1_ref, pre2_ref, pre3_ref, sm0_ref, sm1_ref, cmax0_ref, cmax1_ref,
                 p0_ref, p1_ref, alpha0_ref, alpha1_ref, m_ref, acc_ref, *, topk):
    il_ref = (il0_ref, il1_ref)
    pre_ref = (pre0_ref, pre1_ref, pre2_ref, pre3_ref)
    sm_ref = (sm0_ref, sm1_ref)
    cmax_ref = (cmax0_ref, cmax1_ref)
    p_ref = (p0_ref, p1_ref)
    alpha_ref = (alpha0_ref, alpha1_ref)
    i = pl.program_id(0)
    nsc = (i + 4) >> 2
    nch = 2 * nsc
    nbody = (i + 8) >> 3
    n_causal = (i + 2) >> 1
    n_full, n_tail = n_causal >> 2, n_causal & 3

    def chunk_start(c):
        return pl.multiple_of(c * TK, TK)

    def pipeline(produce, consume, carry):
        for j in range(N_STAGE):
            produce(j, j)

        def body(b, carry):
            c0 = N_STAGE * b
            for j in range(N_STAGE):
                carry = consume(c0 + j, j, carry)
                produce(c0 + N_STAGE + j, j)
            return carry

        carry = lax.fori_loop(0, nbody - 1, body, carry)
        c0 = N_STAGE * (nbody - 1)
        for j in range(N_STAGE):
            carry = consume(c0 + j, j, carry)
        return carry

    @pl.when(i == 0)
    def _():
        col = lax.broadcasted_iota(i32, (TK, LANE), 1)
        row = lax.broadcasted_iota(i32, (TK, LANE), 0).astype(f32)
        pos_ref[...] = jnp.where(col < 3, row, 0.0).astype(bf16)
        term = lax.broadcasted_iota(i32, (LANE, 2 * TQ), 0)
        lane = lax.broadcasted_iota(i32, (LANE, 2 * TQ), 1)
        for p in range(N_PAIR):
            sl = jnp.where(lane < TQ, _slope2(2 * p), _slope2(2 * p + 1)) + jnp.zeros((LANE, 2 * TQ), f32)
            hi, mid, lo = _split3(sl)
            slf_ref[p] = jnp.where(term == 0, hi, jnp.where(term == 1, mid,
                                   jnp.where(term == 2, lo, 0.0))).astype(bf16)

    qi = qi_ref[0]
    w = wi_ref[...]
    d0 = (lax.broadcasted_iota(i32, (TK, TQ), 1) - lax.broadcasted_iota(i32, (TK, TQ), 0))

    def idx_matmul(u, slot):
        rows = pl.ds(pl.multiple_of(u * 2 * TK, 2 * TK), 2 * TK)
        kx = ki_ref[rows, 0:LANE]
        ky = ki_ref[rows, LANE:2 * LANE]
        il_ref[slot][...] = jnp.dot(jnp.concatenate([kx, kx, kx, ky], axis=1), qi,
                                    preferred_element_type=f32)

    def idx_keys(c, slot, half):
        r0 = chunk_start(c)
        rows = slice(half * TK, (half + 1) * TK)
        sc = jnp.maximum(il_ref[slot][rows, 0:TQ], 0.0) * w[0:1, :]
        for hh in range(1, IDX_HEADS):
            sc = sc + jnp.maximum(il_ref[slot][rows, hh * TQ:(hh + 1) * TQ], 0.0) * w[hh:hh + 1, :]
        b = lax.bitcast_convert_type(sc, i32)
        key = jnp.where(b < 0, -(b & 0x7FFFFFFF), b)
        valid = d0 >= (r0 - i * TQ)
        key = jnp.where(valid, key, INT_MIN)
        keys_ref[pl.ds(r0, TK), :] = key
        for n, cand in enumerate(EARLY_CANDS):
            cnt0_ref[n] = cnt0_ref[n] + _tree(jnp.add, _row_groups(jnp.where(key >= cand, 1, 0)))

    cnt0_ref[...] = jnp.zeros(cnt0_ref.shape, i32)

    idx_matmul(0, 0)
    idx_matmul(1, 1)

    def idx_body(b, carry):
        c0 = N_STAGE * b
        idx_keys(c0, 0, 0)
        idx_keys(c0 + 1, 0, 1)
        idx_matmul(2 * b + 2, 0)
        idx_keys(c0 + 2, 1, 0)
        idx_keys(c0 + 3, 1, 1)
        idx_matmul(2 * b + 3, 1)
        return carry

    lax.fori_loop(0, nbody - 1, idx_body, 0)
    c_last = N_STAGE * (nbody - 1)
    for j in range(N_STAGE):
        idx_keys(c_last + j, j // 2, j % 2)

    def count_ge(cand):
        def add_chunk(c, accs):
            ind = jnp.where(keys_ref[pl.ds(chunk_start(c), TK), :] >= cand, 1, 0)
            for j, g in enumerate(_row_groups(ind)):
                accs[j % N_CNT_ACC] = accs[j % N_CNT_ACC] + g

        def body(b, accs):
            accs = list(accs)
            for part in range(N_STAGE):
                add_chunk(b * N_STAGE + part, accs)
            return tuple(accs)

        def tail(j, accs):
            accs = list(accs)
            add_chunk(n_full * N_STAGE + j, accs)
            return tuple(accs)

        accs = lax.fori_loop(0, n_full, body,
                             tuple(jnp.zeros((SUBLANE, TQ), i32) for _ in range(N_CNT_ACC)))
        accs = lax.fori_loop(0, n_tail, tail, accs)
        return jnp.sum(_tree(jnp.add, accs), axis=0, keepdims=True)

    def bit_step(bi, st):
        t, above = st
        cand = t + lax.shift_left(jnp.int32(1), 31 - bi)
        cnt = count_ge(cand)
        up = cnt >= topk
        return jnp.where(up, cand, t), jnp.where(up, above, cnt)

    c_zero, c_pos, c_neg = (jnp.sum(cnt0_ref[n], axis=0, keepdims=True) for n in range(len(EARLY_CANDS)))
    up0 = c_zero >= topk
    t = jnp.where(up0, 0, INT_MIN)
    above = jnp.where(up0, 0, c_zero)
    c_one = jnp.where(up0, c_pos, c_neg)
    up1 = c_one >= topk
    t = jnp.where(up1, t + (1 << 30), t)
    above = jnp.where(up1, above, c_one)
    t, above = lax.fori_loop(2, 32, bit_step, (t, above))
    t = jnp.maximum(t, INT_MIN + 1)
    r_tie = (topk - above).astype(f32)

    tri = jnp.where(lax.broadcasted_iota(i32, (TK, TK), 0) >= lax.broadcasted_iota(i32, (TK, TK), 1),
                    1.0, 0.0).astype(bf16)

    def tie_rank(c, slot):
        e = jnp.where(keys_ref[pl.ds(chunk_start(c), TK), :] == t, 1.0, 0.0).astype(bf16)
        pre_ref[slot][...] = jnp.dot(tri, e, preferred_element_type=f32)

    def mask_out(c, slot, rank):
        r0 = chunk_start(c)
        kk = keys_ref[pl.ds(r0, TK), :]
        pre = pre_ref[slot][...] + rank
        nm = jnp.where(kk > t, 0.0, jnp.where(kk == t, jnp.where(pre <= r_tie, 0.0, NEG_INF), NEG_INF))
        keys_ref[pl.ds(r0, TK), :] = lax.bitcast_convert_type(nm, i32)
        return pre[TK - 1:TK, :]

    pipeline(tie_rank, mask_out, jnp.zeros((1, TQ), f32))

    m_ref[...] = jnp.full(m_ref.shape, NEG_INF, f32)
    acc_ref[...] = jnp.zeros(acc_ref.shape, f32)
    lane2 = lax.broadcasted_iota(i32, (1, 2 * TQ), 1)
    ones_rows = jnp.ones((PV_ROWS - HEAD_DIM, TK), bf16)

    def logits(c, slot):
        rows = pl.ds(chunk_start(c), TK)
        nm = lax.bitcast_convert_type(keys_ref[rows, :], f32)
        nm2 = jnp.concatenate([nm, nm], axis=1)
        for p in range(N_PAIR):
            lhs = jnp.concatenate([k_ref[rows, p * LANE:(p + 1) * LANE], pos_ref[...]], axis=1)
            rhs = jnp.concatenate([qbd_ref[0, p], slf_ref[p]], axis=0)
            sm = jnp.dot(lhs, rhs, preferred_element_type=f32) + nm2
            sm_ref[slot][p] = sm
            cmax_ref[slot][p] = jnp.max(_tree(jnp.maximum, _row_groups(sm)), axis=0, keepdims=True)

    def probs(c, slot):
        r0f = jnp.asarray(c * TK, dtype=f32)
        for p in range(N_PAIR):
            coff = jnp.where(lane2 < TQ, _slope2(2 * p), _slope2(2 * p + 1)) * r0f
            m_old = m_ref[p]
            m_new = jnp.maximum(m_old, cmax_ref[slot][p] + coff)
            m_safe = jnp.where(m_new == NEG_INF, 0.0, m_new)
            alpha_ref[slot][p] = jnp.where(m_old == NEG_INF, 0.0, jnp.exp2(m_old - m_safe))
            p_ref[slot][p] = jnp.exp2((sm_ref[slot][p] - (m_safe - coff)).astype(bf16))
            m_ref[p] = m_new

    def weighted_sum(c, slot):
        for h in range(N_HEADS):
            lanes = slice((h % 2) * TQ, (h % 2 + 1) * TQ)
            lhs = jnp.concatenate([vt_ref[c, h], ones_rows], axis=0)
            pv = jnp.dot(lhs, p_ref[slot][h // 2, :, lanes], preferred_element_type=f32)
            acc_ref[h] = acc_ref[h] * alpha_ref[slot][h // 2, :, lanes] + pv

    logits(0, 0)
    probs(0, 0)
    logits(1, 1)

    def attn_step(c0):
        probs(c0 + 1, 1)
        logits(c0 + 2, 0)
        logits(c0 + 3, 1)
        weighted_sum(c0, 0)
        probs(c0 + 2, 0)
        weighted_sum(c0 + 1, 1)

    n_steps = nsc - 1

    def run_steps(first_step, width, trips):
        def body(b, carry):
            for j in range(width):
                attn_step(2 * (first_step + b * width + j))
            return carry
        lax.fori_loop(0, trips, body, 0)

    shift = ATTN_UNROLL.bit_length() - 1
    run_steps(0, ATTN_UNROLL, n_steps >> shift)
    done = (n_steps >> shift) << shift
    for sh in range(shift - 1, -1, -1):
        take = (n_steps >> sh) & 1
        run_steps(done, 1 << sh, take)
        done = done + (take << sh)
    weighted_sum(nch - 2, 0)
    probs(nch - 1, 1)
    weighted_sum(nch - 1, 1)

    outs = []
    for h in range(N_HEADS):
        a = acc_ref[h]
        outs.append(a[0:HEAD_DIM] / a[HEAD_DIM:HEAD_DIM + 1])
    o_ref[...] = jnp.concatenate(outs, axis=0).T.astype(bf16)


def _attn_call(qbd, qi, wit, k, vt, ki, topk):
    s = k.shape[0]
    nq = s // TQ
    whole = pl.BlockSpec(memory_space=pltpu.VMEM)
    il = [pltpu.VMEM((2 * TK, IDX_HEADS * TQ), f32)] * 2
    pre = [pltpu.VMEM((TK, TQ), f32)] * N_STAGE
    return pl.pallas_call(
        functools.partial(_attn_kernel, topk=topk),
        grid=(nq,),
        in_specs=[pl.BlockSpec((1, N_PAIR, LANE, 2 * TQ), lambda i: (i, 0, 0, 0)),
                  pl.BlockSpec((1, IDX_K, IDX_HEADS * TQ), lambda i: (i, 0, 0)),
                  pl.BlockSpec((SUBLANE, TQ), lambda i: (0, i)),
                  whole, whole, whole],
        out_specs=pl.BlockSpec((TQ, ATTN_WIDTH), lambda i: (i, 0)),
        out_shape=jax.ShapeDtypeStruct((s, ATTN_WIDTH), bf16),
        scratch_shapes=[pltpu.VMEM((s, TQ), i32),
                        pltpu.VMEM((len(EARLY_CANDS), SUBLANE, TQ), i32),
                        pltpu.VMEM((TK, LANE), bf16),
                        pltpu.VMEM((N_PAIR, LANE, 2 * TQ), bf16),
                        *il, *pre,
                        pltpu.VMEM((N_PAIR, TK, 2 * TQ), f32),
                        pltpu.VMEM((N_PAIR, TK, 2 * TQ), f32),
                        pltpu.VMEM((N_PAIR, 1, 2 * TQ), f32),
                        pltpu.VMEM((N_PAIR, 1, 2 * TQ), f32),
                        pltpu.VMEM((N_PAIR, TK, 2 * TQ), bf16),
                        pltpu.VMEM((N_PAIR, TK, 2 * TQ), bf16),
                        pltpu.VMEM((N_PAIR, 1, 2 * TQ), f32),
                        pltpu.VMEM((N_PAIR, 1, 2 * TQ), f32),
                        pltpu.VMEM((N_PAIR, 1, 2 * TQ), f32),
                        pltpu.VMEM((N_HEADS, PV_ROWS, TQ), f32)],
        compiler_params=pltpu.CompilerParams(dimension_semantics=("arbitrary",),
                                             vmem_limit_bytes=VMEM_LIMIT),
        name="attn",
    )(qbd, qi, wit, k, vt, ki)


HALO_POOL = 16


def _mix_kernel(x_ref, attn_ref, u_ref, uh_ref, ga_ref, gp_ref, mod_ref,
                wab_ref, wg_ref, ps_ref, wpb_ref, wo_ref, o_ref, *, tm):
    i = pl.program_id(0)
    y_attn = jnp.dot(attn_ref[...], wab_ref[...], preferred_element_type=f32)

    u = u_ref[...]
    halo = jnp.where(i > 0, uh_ref[...], 0.0)
    a = jnp.concatenate([halo, u], axis=0)
    tpos = (i * tm + lax.broadcasted_iota(i32, (tm, POOL_GROUP_DIM), 0) + 1).astype(f32)
    mixed = []
    for g, wdw in enumerate(POOL_WINDOWS):
        ag = a[:, g * POOL_GROUP_DIM:(g + 1) * POOL_GROUP_DIM]
        ug = ag[HALO_POOL:HALO_POOL + tm]
        ssum = ug
        for j in range(1, wdw):
            ssum = ssum + ag[HALO_POOL - j:HALO_POOL - j + tm]
        pooled = ssum / jnp.minimum(tpos, float(wdw)) - ug
        mixed.append(jnp.dot(pooled.astype(bf16), wg_ref[g], preferred_element_type=f32))
    mixed = jnp.concatenate(mixed, axis=1) * ps_ref[...]
    y_pool = jnp.dot(mixed.astype(bf16), wpb_ref[...], preferred_element_type=f32)

    merged = ga_ref[...].astype(f32) * y_attn + gp_ref[...].astype(f32) * y_pool
    o = jnp.dot(merged.astype(bf16), wo_ref[...], preferred_element_type=f32)
    gate = mod_ref[0:1, 2 * D_MODEL:3 * D_MODEL]
    o_ref[...] = x_ref[...] + gate * o


def _mix_call(x2, attn, u, ga, gp, mod, wab, wg, ps, wpb, wo, tm):
    s = x2.shape[0]
    const2 = lambda i: (0, 0)
    hb = tm // HALO_POOL
    return pl.pallas_call(
        functools.partial(_mix_kernel, tm=tm),
        grid=(s // tm,),
        in_specs=[pl.BlockSpec((tm, D_MODEL), lambda i: (i, 0)),
                  pl.BlockSpec((tm, ATTN_WIDTH), lambda i: (i, 0)),
                  pl.BlockSpec((tm, POOL_WIDTH), lambda i: (i, 0)),
                  pl.BlockSpec((HALO_POOL, POOL_WIDTH), lambda i: (jnp.maximum(i * hb - 1, 0), 0)),
                  pl.BlockSpec((tm, D_MODEL), lambda i: (i, 0)),
                  pl.BlockSpec((tm, D_MODEL), lambda i: (i, 0)),
                  pl.BlockSpec((SUBLANE, N_MOD * D_MODEL), const2),
                  pl.BlockSpec((ATTN_WIDTH, D_MODEL), const2),
                  pl.BlockSpec((len(POOL_WINDOWS), POOL_GROUP_DIM, POOL_GROUP_DIM), lambda i: (0, 0, 0)),
                  pl.BlockSpec((1, POOL_WIDTH), const2),
                  pl.BlockSpec((POOL_WIDTH, D_MODEL), const2),
                  pl.BlockSpec((D_MODEL, D_MODEL), const2)],
        out_specs=pl.BlockSpec((tm, D_MODEL), lambda i: (i, 0)),
        out_shape=jax.ShapeDtypeStruct((s, D_MODEL), f32),
        compiler_params=pltpu.CompilerParams(dimension_semantics=("parallel",),
                                             vmem_limit_bytes=VMEM_LIMIT),
        name="mix",
    )(x2, attn, u, u, ga, gp, mod, wab, wg, ps, wpb, wo)


HALO_CONV = 8


def _ffn_kernel(x_ref, xh_ref, mod_ref, g2_ref, wup_ref, cw_ref, cb_ref, wdn_ref, o_ref, *, tm):
    i = pl.program_id(0)
    shift = mod_ref[0:1, 3 * D_MODEL:4 * D_MODEL]
    scale = mod_ref[0:1, 4 * D_MODEL:5 * D_MODEL]
    gate = mod_ref[0:1, 5 * D_MODEL:6 * D_MODEL]
    g2 = g2_ref[...]
    x = x_ref[...]
    h = _rms_modulate(x, g2, shift, scale)
    hh = jnp.where(i > 0, _rms_modulate(xh_ref[...], g2, shift, scale), 0.0)
    ha = jnp.concatenate([hh, h], axis=0).astype(bf16)
    up = jnp.dot(ha, wup_ref[...], preferred_element_type=f32)
    cw = cw_ref[...]
    y = cb_ref[...] + cw[0:1, :] * up[HALO_CONV - 2:HALO_CONV - 2 + tm]
    y = y + cw[1:2, :] * up[HALO_CONV - 1:HALO_CONV - 1 + tm]
    y = y + cw[2:3, :] * up[HALO_CONV:HALO_CONV + tm]
    a = y[:, 0:D_FF]
    b = y[:, D_FF:2 * D_FF]
    gated = (a * _sigmoid(a)) * b
    o = jnp.dot(gated.astype(bf16), wdn_ref[...], preferred_element_type=f32)
    o_ref[...] = x + gate * o


def _ffn_call(x1, mod, g2, wup, cw, cb, wdn, tm):
    s = x1.shape[0]
    const2 = lambda i: (0, 0)
    hb = tm // HALO_CONV
    return pl.pallas_call(
        functools.partial(_ffn_kernel, tm=tm),
        grid=(s // tm,),
        in_specs=[pl.BlockSpec((tm, D_MODEL), lambda i: (i, 0)),
                  pl.BlockSpec((HALO_CONV, D_MODEL), lambda i: (jnp.maximum(i * hb - 1, 0), 0)),
                  pl.BlockSpec((SUBLANE, N_MOD * D_MODEL), const2),
                  pl.BlockSpec((1, D_MODEL), const2),
                  pl.BlockSpec((D_MODEL, 2 * D_FF), const2, pipeline_mode=pl.Buffered(1)),
                  pl.BlockSpec((3, 2 * D_FF), const2),
                  pl.BlockSpec((1, 2 * D_FF), const2),
                  pl.BlockSpec((D_FF, D_MODEL), const2, pipeline_mode=pl.Buffered(1))],
        out_specs=pl.BlockSpec((tm, D_MODEL), lambda i: (i, 0)),
        out_shape=jax.ShapeDtypeStruct((s, D_MODEL), f32),
        compiler_params=pltpu.CompilerParams(dimension_semantics=("parallel",),
                                             vmem_limit_bytes=VMEM_LIMIT),
        name="ffn",
    )(x1, x1, mod, g2, wup, cw, cb, wdn)


def _pack_w_idx(w):
    return jnp.pad(w[:, IN_IDX0:IN_IDX1], ((0, 0), (0, 3 * LANE - (IN_IDX1 - IN_IDX0))))


def kernel(x, c, w_ada, b_ada, norm1_g, w_in, q_norm_g, k_norm_g, w_attn_br, w_pool_grp,
           pool_scale, w_pool_br, w_out, norm2_g, w_up, conv_w, conv_b, w_down):
    bsz, s, d = x.shape
    assert bsz == 1 and d == D_MODEL and s % (N_STAGE * TK) == 0
    depth = w_ada.shape[0]
    topk = min(TOPK_MAX, s // 4)
    x2 = x.reshape(s, d)
    c8 = jnp.pad(c, ((0, SUBLANE - bsz), (0, 0)))
    for l in range(depth):
        mod = _mod_call(c8, w_ada[l], b_ada[l].reshape(1, -1))
        qbd, k, vt, qi, ki, wit, u, ga, gp = _proj_call(
            x2, mod, norm1_g[l].reshape(1, -1),
            w_in[l][:, 0:IN_IDX0].astype(bf16), w_in[l][:, IN_IDX1:IN_END].astype(bf16),
            jnp.tile(q_norm_g[l], N_HEADS).reshape(1, -1),
            jnp.tile(k_norm_g[l], N_HEADS).reshape(1, -1), _pack_w_idx(w_in[l]), tm=512)
        attn = _attn_call(qbd, qi, wit, k, vt, ki, topk)
        x2 = _mix_call(x2, attn, u, ga, gp, mod, w_attn_br[l].astype(bf16),
                       w_pool_grp[l].astype(bf16), pool_scale[l].reshape(1, -1),
                       w_pool_br[l].astype(bf16), w_out[l].astype(bf16), tm=512)
        x2 = _ffn_call(x2, mod, norm2_g[l].reshape(1, -1), w_up[l].astype(bf16), conv_w[l],
                       conv_b[l].reshape(1, -1), w_down[l].astype(bf16), tm=512)
    return x2.reshape(bsz, s, d)
```

```python
import functools

import jax
import jax.numpy as jnp
from jax import lax
from jax.experimental import pallas as pl
from jax.experimental.pallas import tpu as pltpu

f32 = jnp.float32
bf16 = jnp.bfloat16
i32 = jnp.int32

D_MODEL = 1024
N_HEADS = 8
HEAD_DIM = 64
ATTN_WIDTH = N_HEADS * HEAD_DIM
IDX_HEADS = 4
IDX_DIM = 64
TOPK_MAX = 256
POOL_WINDOWS = (2, 4, 8, 16)
POOL_GROUP_DIM = 128
POOL_WIDTH = 512
D_FF = 2816
EPS = 1e-6
N_MOD = 6

LANE = 128
SUBLANE = 8
VMEM_LIMIT = 58 * 1024 * 1024

LOG2E = 1.4426950408889634
INT_MIN = -2147483648
NEG_INF = float("-inf")

IN_IDX0 = 3 * ATTN_WIDTH
IN_IDX1 = IN_IDX0 + IDX_HEADS * IDX_DIM + IDX_DIM + IDX_HEADS
IN_END = IN_IDX1 + POOL_WIDTH + 2 * D_MODEL
C_Q, C_K, C_V, C_QKV = 0, ATTN_WIDTH, 2 * ATTN_WIDTH, 3 * ATTN_WIDTH
C_U, C_GA, C_GP, C_REST = 0, POOL_WIDTH, POOL_WIDTH + D_MODEL, POOL_WIDTH + 2 * D_MODEL

TQ = 128
TK = 256
N_PAIR = N_HEADS // 2
IDX_K = 4 * LANE


def _sigmoid(x):
    return 1.0 / (1.0 + jnp.exp(-x))


def _rms_modulate(x, g, shift, scale):
    y = x * lax.rsqrt(jnp.mean(x * x, axis=-1, keepdims=True) + EPS)
    return (y * g) * (1.0 + scale) + shift


def _mod_kernel(c_ref, w_ref, b_ref, o_ref):
    c = c_ref[...]
    sc = c * _sigmoid(c)
    o_ref[...] = jnp.dot(sc, w_ref[...], precision=lax.Precision.HIGHEST,
                         preferred_element_type=f32) + b_ref[...]


def _mod_call(c8, w_ada, b_ada):
    n = w_ada.shape[1]
    tn = 1024
    return pl.pallas_call(
        _mod_kernel,
        grid=(n // tn,),
        in_specs=[pl.BlockSpec((SUBLANE, D_MODEL), lambda j: (0, 0)),
                  pl.BlockSpec((D_MODEL, tn), lambda j: (0, j)),
                  pl.BlockSpec((1, tn), lambda j: (0, j))],
        out_specs=pl.BlockSpec((SUBLANE, tn), lambda j: (0, j)),
        out_shape=jax.ShapeDtypeStruct((SUBLANE, n), f32),
        name="mod",
    )(c8, w_ada, b_ada)


def _split3(x):
    hi = x.astype(bf16).astype(f32)
    r = x - hi
    mid = r.astype(bf16).astype(f32)
    lo = (r - mid).astype(bf16).astype(f32)
    return hi, mid, lo


def _head_norm(z, g, bd):
    z2 = z * z
    hi = z2.astype(bf16)
    lo = (z2 - hi.astype(f32)).astype(bf16)
    ms = jnp.dot(hi, bd, preferred_element_type=f32) + jnp.dot(lo, bd, preferred_element_type=f32)
    return (z * lax.rsqrt(ms + EPS)) * g


def _proj_kernel(x_ref, mod_ref, g1_ref, wa_ref, wb_ref, qg_ref, kg_ref, widx_ref,
                 qbd_ref, k_ref, vt_ref, qi_ref, ki_ref, wit_ref, u_ref, ga_ref, gp_ref, *, tm):
    x = x_ref[...]
    shift = mod_ref[0:1, 0:D_MODEL]
    scale = mod_ref[0:1, D_MODEL:2 * D_MODEL]
    h = _rms_modulate(x, g1_ref[...], shift, scale)
    hb = h.astype(bf16)
    proj = jnp.dot(hb, wa_ref[...], preferred_element_type=f32)
    rest = jnp.dot(hb, wb_ref[...], preferred_element_type=f32)

    r = lax.broadcasted_iota(i32, (ATTN_WIDTH, ATTN_WIDTH), 0)
    c = lax.broadcasted_iota(i32, (ATTN_WIDTH, ATTN_WIDTH), 1)
    head_shift = HEAD_DIM.bit_length() - 1
    bd = jnp.where((r >> head_shift) == (c >> head_shift), 1.0 / HEAD_DIM, 0.0).astype(bf16)

    q = _head_norm(proj[:, C_Q:C_K], qg_ref[...], bd) * (HEAD_DIM ** -0.5 * LOG2E)
    k = _head_norm(proj[:, C_K:C_V], kg_ref[...], bd)
    k_ref[...] = k.astype(bf16)

    low = lax.broadcasted_iota(i32, (TQ, LANE), 1) < HEAD_DIM
    low_tm = lax.broadcasted_iota(i32, (tm, LANE), 1) < HEAD_DIM
    pidx = jnp.dot(h, widx_ref[...], precision=lax.Precision.HIGHEST, preferred_element_type=f32)
    qi = pidx[:, 0:IDX_HEADS * IDX_DIM] * (IDX_DIM ** -0.5)
    up64 = lambda z: pltpu.roll(z, HEAD_DIM, axis=1)
    for g in range(tm // TQ):
        rows = slice(g * TQ, (g + 1) * TQ)
        for p in range(N_PAIR):
            qp = q[rows, p * LANE:(p + 1) * LANE]
            bd_q = jnp.concatenate([jnp.where(low, qp, 0.0), jnp.where(low, 0.0, qp)], axis=0)
            qbd_ref[g, p] = bd_q.T.astype(bf16)
        for hh in range(IDX_HEADS):
            seg = qi[rows, (hh // 2) * LANE:(hh // 2 + 1) * LANE]
            own = low if hh % 2 == 0 else jnp.logical_not(low)
            qh, qm, ql = _split3(jnp.where(own, seg, 0.0))
            both = lambda z: z + up64(z)
            in_low = lambda z: jnp.where(low, both(z), 0.0)
            hcols = slice(hh * TQ, (hh + 1) * TQ)
            for kt, term in enumerate((both(qh), both(qm), in_low(ql), in_low(qh))):
                qi_ref[g, kt * LANE:(kt + 1) * LANE, hcols] = term.T.astype(bf16)

    v = proj[:, C_V:C_QKV]
    vt = v.T.astype(bf16)
    for cc in range(tm // TK):
        for hh in range(N_HEADS):
            vt_ref[cc, hh] = vt[hh * HEAD_DIM:(hh + 1) * HEAD_DIM, cc * TK:(cc + 1) * TK]

    kw = pidx[:, IDX_HEADS * IDX_DIM:3 * LANE]
    kh, km, kl = _split3(jnp.where(low_tm, kw, 0.0))
    ki_ref[:, 0:LANE] = (kh + up64(km)).astype(bf16)
    ki_ref[:, LANE:2 * LANE] = kl.astype(bf16)
    wt = (kw * (IDX_HEADS ** -0.5)).T
    wit_ref[...] = wt[HEAD_DIM:HEAD_DIM + SUBLANE, :]
    u_ref[...] = rest[:, C_U:C_GA]
    ga_ref[...] = _sigmoid(rest[:, C_GA:C_GP]).astype(bf16)
    gp_ref[...] = _sigmoid(rest[:, C_GP:C_REST]).astype(bf16)


def _proj_call(x2, mod, g1, w_qkv, w_rest, qg, kg, widx, tm):
    s = x2.shape[0]
    nq = s // TQ
    const = lambda i: (0, 0)
    return pl.pallas_call(
        functools.partial(_proj_kernel, tm=tm),
        grid=(s // tm,),
        in_specs=[pl.BlockSpec((tm, D_MODEL), lambda i: (i, 0)),
                  pl.BlockSpec((SUBLANE, N_MOD * D_MODEL), const),
                  pl.BlockSpec((1, D_MODEL), const),
                  pl.BlockSpec((D_MODEL, C_QKV), const),
                  pl.BlockSpec((D_MODEL, C_REST), const),
                  pl.BlockSpec((1, ATTN_WIDTH), const),
                  pl.BlockSpec((1, ATTN_WIDTH), const),
                  pl.BlockSpec((D_MODEL, 3 * LANE), const)],
        out_specs=[pl.BlockSpec((tm // TQ, N_PAIR, LANE, 2 * TQ), lambda i: (i, 0, 0, 0)),
                   pl.BlockSpec((tm, ATTN_WIDTH), lambda i: (i, 0)),
                   pl.BlockSpec((tm // TK, N_HEADS, HEAD_DIM, TK), lambda i: (i, 0, 0, 0)),
                   pl.BlockSpec((tm // TQ, IDX_K, IDX_HEADS * TQ), lambda i: (i, 0, 0)),
                   pl.BlockSpec((tm, 2 * LANE), lambda i: (i, 0)),
                   pl.BlockSpec((SUBLANE, tm), lambda i: (0, i)),
                   pl.BlockSpec((tm, POOL_WIDTH), lambda i: (i, 0)),
                   pl.BlockSpec((tm, D_MODEL), lambda i: (i, 0)),
                   pl.BlockSpec((tm, D_MODEL), lambda i: (i, 0))],
        out_shape=[jax.ShapeDtypeStruct((nq, N_PAIR, LANE, 2 * TQ), bf16),
                   jax.ShapeDtypeStruct((s, ATTN_WIDTH), bf16),
                   jax.ShapeDtypeStruct((s // TK, N_HEADS, HEAD_DIM, TK), bf16),
                   jax.ShapeDtypeStruct((nq, IDX_K, IDX_HEADS * TQ), bf16),
                   jax.ShapeDtypeStruct((s, 2 * LANE), bf16),
                   jax.ShapeDtypeStruct((SUBLANE, s), f32),
                   jax.ShapeDtypeStruct((s, POOL_WIDTH), f32),
                   jax.ShapeDtypeStruct((s, D_MODEL), bf16),
                   jax.ShapeDtypeStruct((s, D_MODEL), bf16)],
        compiler_params=pltpu.CompilerParams(dimension_semantics=("parallel",),
                                             vmem_limit_bytes=VMEM_LIMIT),
        name="proj",
    )(x2, mod, g1, w_qkv, w_rest, qg, kg, widx)


def _slope2(h):
    return (2.0 ** (-8.0 * (h + 1) / N_HEADS)) * LOG2E


def _tree(op, xs):
    xs = list(xs)
    while len(xs) > 1:
        nxt = [op(xs[a], xs[a + 1]) for a in range(0, len(xs) - 1, 2)]
        if len(xs) % 2:
            nxt.append(xs[-1])
        xs = nxt
    return xs[0]


def _row_groups(x):
    return [x[j * SUBLANE:(j + 1) * SUBLANE] for j in range(x.shape[0] // SUBLANE)]


N_CNT_ACC = 8
N_STAGE = 4
ATTN_UNROLL = 8
EARLY_CANDS = (0, 1 << 30, INT_MIN + (1 << 30))
PV_ROWS = HEAD_DIM + 16


def _attn_kernel(qbd_ref, qi_ref, wi_ref, k_ref, vt_ref, ki_ref, o_ref,
                 keys_ref, cnt0_ref, pos_ref, slf_ref, il0_ref, il1_ref,
                 pre0_ref, pre1_ref, pre2_ref, pre3_ref, sm0_ref, sm1_ref, cmax0_ref, cmax1_ref,
                 p0_ref, p1_ref, alpha0_ref, alpha1_ref, m_ref, acc_ref, *, topk):
    il_ref = (il0_ref, il1_ref)
    pre_ref = (pre0_ref, pre1_ref, pre2_ref, pre3_ref)
    sm_ref = (sm0_ref, sm1_ref)
    cmax_ref = (cmax0_ref, cmax1_ref)
    p_ref = (p0_ref, p1_ref)
    alpha_ref = (alpha0_ref, alpha1_ref)
    i = pl.program_id(0)
    nsc = (i + 4) >> 2
    nch = 2 * nsc
    nbody = (i + 8) >> 3
    n_causal = (i + 2) >> 1
    n_full, n_tail = n_causal >> 2, n_causal & 3

    def chunk_start(c):
        return pl.multiple_of(c * TK, TK)

    def paired_loop(body, n, carry):
        carry = lax.fori_loop(0, n >> 1, lambda b, cr: body(2 * b + 1, body(2 * b, cr)), carry)
        return lax.fori_loop(0, n & 1, lambda b, cr: body(n - 1, cr), carry)

    def pipeline(produce, consume, carry):
        for j in range(N_STAGE):
            produce(j, j)

        def body(b, carry):
            c0 = N_STAGE * b
            for j in range(N_STAGE):
                carry = consume(c0 + j, j, carry)
                produce(c0 + N_STAGE + j, j)
            return carry

        carry = paired_loop(body, nbody - 1, carry)
        c0 = N_STAGE * (nbody - 1)
        for j in range(N_STAGE):
            carry = consume(c0 + j, j, carry)
        return carry

    @pl.when(i == 0)
    def _():
        col = lax.broadcasted_iota(i32, (TK, LANE), 1)
        row = lax.broadcasted_iota(i32, (TK, LANE), 0).astype(f32)
        pos_ref[...] = jnp.where(col < 3, row, 0.0).astype(bf16)
        term = lax.broadcasted_iota(i32, (LANE, 2 * TQ), 0)
        lane = lax.broadcasted_iota(i32, (LANE, 2 * TQ), 1)
        for p in range(N_PAIR):
            sl = jnp.where(lane < TQ, _slope2(2 * p), _slope2(2 * p + 1)) + jnp.zeros((LANE, 2 * TQ), f32)
            hi, mid, lo = _split3(sl)
            slf_ref[p] = jnp.where(term == 0, hi, jnp.where(term == 1, mid,
                                   jnp.where(term == 2, lo, 0.0))).astype(bf16)

    qi = qi_ref[0]
    w = wi_ref[...]
    d0 = (lax.broadcasted_iota(i32, (TK, TQ), 1) - lax.broadcasted_iota(i32, (TK, TQ), 0))

    def idx_matmul(u, slot):
        rows = pl.ds(pl.multiple_of(u * 2 * TK, 2 * TK), 2 * TK)
        kx = ki_ref[rows, 0:LANE]
        ky = ki_ref[rows, LANE:2 * LANE]
        il_ref[slot][...] = jnp.dot(jnp.concatenate([kx, kx, kx, ky], axis=1), qi,
                                    preferred_element_type=f32)

    def idx_keys(c, slot, half):
        r0 = chunk_start(c)
        rows = slice(half * TK, (half + 1) * TK)
        sc = jnp.maximum(il_ref[slot][rows, 0:TQ], 0.0) * w[0:1, :]
        for hh in range(1, IDX_HEADS):
            sc = sc + jnp.maximum(il_ref[slot][rows, hh * TQ:(hh + 1) * TQ], 0.0) * w[hh:hh + 1, :]
        b = lax.bitcast_convert_type(sc, i32)
        key = jnp.where(b < 0, -(b & 0x7FFFFFFF), b)
        valid = d0 >= (r0 - i * TQ)
        key = jnp.where(valid, key, INT_MIN)
        keys_ref[pl.ds(r0, TK), :] = key
        for n, cand in enumerate(EARLY_CANDS):
            cnt0_ref[n] = cnt0_ref[n] + _tree(jnp.add, _row_groups(jnp.where(key >= cand, 1, 0)))

    cnt0_ref[...] = jnp.zeros(cnt0_ref.shape, i32)

    idx_matmul(0, 0)
    idx_matmul(1, 1)

    def idx_body(b, carry):
        c0 = N_STAGE * b
        idx_keys(c0, 0, 0)
        idx_keys(c0 + 1, 0, 1)
        idx_matmul(2 * b + 2, 0)
        idx_keys(c0 + 2, 1, 0)
        idx_keys(c0 + 3, 1, 1)
        idx_matmul(2 * b + 3, 1)
        return carry

    paired_loop(idx_body, nbody - 1, 0)
    c_last = N_STAGE * (nbody - 1)
    for j in range(N_STAGE):
        idx_keys(c_last + j, j // 2, j % 2)

    def count_ge(cand):
        def add_chunk(c, accs):
            ind = jnp.where(keys_ref[pl.ds(chunk_start(c), TK), :] >= cand, 1, 0)
            for j, g in enumerate(_row_groups(ind)):
                accs[j % N_CNT_ACC] = accs[j % N_CNT_ACC] + g

        def body(b, accs):
            accs = list(accs)
            for part in range(N_STAGE):
                add_chunk(b * N_STAGE + part, accs)
            return tuple(accs)

        def tail(j, accs):
            accs = list(accs)
            add_chunk(n_full * N_STAGE + j, accs)
            return tuple(accs)

        accs = lax.fori_loop(0, n_full, body,
                             tuple(jnp.zeros((SUBLANE, TQ), i32) for _ in range(N_CNT_ACC)))
        accs = lax.fori_loop(0, n_tail, tail, accs)
        return jnp.sum(_tree(jnp.add, accs), axis=0, keepdims=True)

    def bit_step(bi, st):
        t, above = st
        cand = t + lax.shift_left(jnp.int32(1), 31 - bi)
        cnt = count_ge(cand)
        up = cnt >= topk
        return jnp.where(up, cand, t), jnp.where(up, above, cnt)

    c_zero, c_pos, c_neg = (jnp.sum(cnt0_ref[n], axis=0, keepdims=True) for n in range(len(EARLY_CANDS)))
    up0 = c_zero >= topk
    t = jnp.where(up0, 0, INT_MIN)
    above = jnp.where(up0, 0, c_zero)
    c_one = jnp.where(up0, c_pos, c_neg)
    up1 = c_one >= topk
    t = jnp.where(up1, t + (1 << 30), t)
    above = jnp.where(up1, above, c_one)
    t, above = lax.fori_loop(2, 32, bit_step, (t, above))
    t = jnp.maximum(t, INT_MIN + 1)
    r_tie = (topk - above).astype(f32)

    tri = jnp.where(lax.broadcasted_iota(i32, (TK, TK), 0) >= lax.broadcasted_iota(i32, (TK, TK), 1),
                    1.0, 0.0).astype(bf16)

    def tie_rank(c, slot):
        e = jnp.where(keys_ref[pl.ds(chunk_start(c), TK), :] == t, 1.0, 0.0).astype(bf16)
        pre_ref[slot][...] = jnp.dot(tri, e, preferred_element_type=f32)

    def mask_out(c, slot, rank):
        r0 = chunk_start(c)
        kk = keys_ref[pl.ds(r0, TK), :]
        pre = pre_ref[slot][...] + rank
        nm = jnp.where(kk > t, 0.0, jnp.where(kk == t, jnp.where(pre <= r_tie, 0.0, NEG_INF), NEG_INF))
        keys_ref[pl.ds(r0, TK), :] = lax.bitcast_convert_type(nm, i32)
        return pre[TK - 1:TK, :]

    pipeline(tie_rank, mask_out, jnp.zeros((1, TQ), f32))

    m_ref[...] = jnp.full(m_ref.shape, NEG_INF, f32)
    acc_ref[...] = jnp.zeros(acc_ref.shape, f32)
    lane2 = lax.broadcasted_iota(i32, (1, 2 * TQ), 1)
    ones_rows = jnp.ones((PV_ROWS - HEAD_DIM, TK), bf16)

    def logits(c, slot):
        rows = pl.ds(chunk_start(c), TK)
        nm = lax.bitcast_convert_type(keys_ref[rows, :], f32)
        nm2 = jnp.concatenate([nm, nm], axis=1)
        for p in range(N_PAIR):
            lhs = jnp.concatenate([k_ref[rows, p * LANE:(p + 1) * LANE], pos_ref[...]], axis=1)
            rhs = jnp.concatenate([qbd_ref[0, p], slf_ref[p]], axis=0)
            sm = jnp.dot(lhs, rhs, preferred_element_type=f32) + nm2
            sm_ref[slot][p] = sm
            cmax_ref[slot][p] = jnp.max(_tree(jnp.maximum, _row_groups(sm)), axis=0, keepdims=True)

    def probs(c, slot):
        r0f = jnp.asarray(c * TK, dtype=f32)
        for p in range(N_PAIR):
            coff = jnp.where(lane2 < TQ, _slope2(2 * p), _slope2(2 * p + 1)) * r0f
            m_old = m_ref[p]
            m_new = jnp.maximum(m_old, cmax_ref[slot][p] + coff)
            m_safe = jnp.where(m_new == NEG_INF, 0.0, m_new)
            alpha_ref[slot][p] = jnp.where(m_old == NEG_INF, 0.0, jnp.exp2(m_old - m_safe))
            p_ref[slot][p] = jnp.exp2((sm_ref[slot][p] - (m_safe - coff)).astype(bf16))
            m_ref[p] = m_new

    def weighted_sum(c, slot):
        for h in range(N_HEADS):
            lanes = slice((h % 2) * TQ, (h % 2 + 1) * TQ)
            lhs = jnp.concatenate([vt_ref[c, h], ones_rows], axis=0)
            pv = jnp.dot(lhs, p_ref[slot][h // 2, :, lanes], preferred_element_type=f32)
            acc_ref[h] = acc_ref[h] * alpha_ref[slot][h // 2, :, lanes] + pv

    logits(0, 0)
    probs(0, 0)
    logits(1, 1)

    def attn_step(c0):
        probs(c0 + 1, 1)
        logits(c0 + 2, 0)
        logits(c0 + 3, 1)
        weighted_sum(c0, 0)
        probs(c0 + 2, 0)
        weighted_sum(c0 + 1, 1)

    n_steps = nsc - 1

    def run_steps(first_step, width, trips):
        def body(b, carry):
            for j in range(width):
                attn_step(2 * (first_step + b * width + j))
            return carry
        lax.fori_loop(0, trips, body, 0)

    shift = ATTN_UNROLL.bit_length() - 1
    run_steps(0, ATTN_UNROLL, n_steps >> shift)
    done = (n_steps >> shift) << shift
    for sh in range(shift - 1, -1, -1):
        take = (n_steps >> sh) & 1
        run_steps(done, 1 << sh, take)
        done = done + (take << sh)
    weighted_sum(nch - 2, 0)
    probs(nch - 1, 1)
    weighted_sum(nch - 1, 1)

    outs = []
    for h in range(N_HEADS):
        a = acc_ref[h]
        outs.append(a[0:HEAD_DIM] / a[HEAD_DIM:HEAD_DIM + 1])
    o_ref[...] = jnp.concatenate(outs, axis=0).T.astype(bf16)


def _attn_call(qbd, qi, wit, k, vt, ki, topk):
    s = k.shape[0]
    nq = s // TQ
    whole = pl.BlockSpec(memory_space=pltpu.VMEM)
    il = [pltpu.VMEM((2 * TK, IDX_HEADS * TQ), f32)] * 2
    pre = [pltpu.VMEM((TK, TQ), f32)] * N_STAGE
    return pl.pallas_call(
        functools.partial(_attn_kernel, topk=topk),
        grid=(nq,),
        in_specs=[pl.BlockSpec((1, N_PAIR, LANE, 2 * TQ), lambda i: (i, 0, 0, 0)),
                  pl.BlockSpec((1, IDX_K, IDX_HEADS * TQ), lambda i: (i, 0, 0)),
                  pl.BlockSpec((SUBLANE, TQ), lambda i: (0, i)),
                  whole, whole, whole],
        out_specs=pl.BlockSpec((TQ, ATTN_WIDTH), lambda i: (i, 0)),
        out_shape=jax.ShapeDtypeStruct((s, ATTN_WIDTH), bf16),
        scratch_shapes=[pltpu.VMEM((s, TQ), i32),
                        pltpu.VMEM((len(EARLY_CANDS), SUBLANE, TQ), i32),
                        pltpu.VMEM((TK, LANE), bf16),
                        pltpu.VMEM((N_PAIR, LANE, 2 * TQ), bf16),
                        *il, *pre,
                        pltpu.VMEM((N_PAIR, TK, 2 * TQ), f32),
                        pltpu.VMEM((N_PAIR, TK, 2 * TQ), f32),
                        pltpu.VMEM((N_PAIR, 1, 2 * TQ), f32),
                        pltpu.VMEM((N_PAIR, 1, 2 * TQ), f32),
                        pltpu.VMEM((N_PAIR, TK, 2 * TQ), bf16),
                        pltpu.VMEM((N_PAIR, TK, 2 * TQ), bf16),
                        pltpu.VMEM((N_PAIR, 1, 2 * TQ), f32),
                        pltpu.VMEM((N_PAIR, 1, 2 * TQ), f32),
                        pltpu.VMEM((N_PAIR, 1, 2 * TQ), f32),
                        pltpu.VMEM((N_HEADS, PV_ROWS, TQ), f32)],
        compiler_params=pltpu.CompilerParams(dimension_semantics=("arbitrary",),
                                             vmem_limit_bytes=VMEM_LIMIT),
        name="attn",
    )(qbd, qi, wit, k, vt, ki)


HALO_POOL = 16


def _mix_kernel(x_ref, attn_ref, u_ref, uh_ref, ga_ref, gp_ref, mod_ref,
                wab_ref, wg_ref, ps_ref, wpb_ref, wo_ref, o_ref, *, tm):
    i = pl.program_id(0)
    y_attn = jnp.dot(attn_ref[...], wab_ref[...], preferred_element_type=f32)

    u = u_ref[...]
    halo = jnp.where(i > 0, uh_ref[...], 0.0)
    a = jnp.concatenate([halo, u], axis=0)
    tpos = (i * tm + lax.broadcasted_iota(i32, (tm, POOL_GROUP_DIM), 0) + 1).astype(f32)
    mixed = []
    for g, wdw in enumerate(POOL_WINDOWS):
        ag = a[:, g * POOL_GROUP_DIM:(g + 1) * POOL_GROUP_DIM]
        ug = ag[HALO_POOL:HALO_POOL + tm]
        ssum = ug
        for j in range(1, wdw):
            ssum = ssum + ag[HALO_POOL - j:HALO_POOL - j + tm]
        pooled = ssum / jnp.minimum(tpos, float(wdw)) - ug
        mixed.append(jnp.dot(pooled.astype(bf16), wg_ref[g], preferred_element_type=f32))
    mixed = jnp.concatenate(mixed, axis=1) * ps_ref[...]
    y_pool = jnp.dot(mixed.astype(bf16), wpb_ref[...], preferred_element_type=f32)

    merged = ga_ref[...].astype(f32) * y_attn + gp_ref[...].astype(f32) * y_pool
    o = jnp.dot(merged.astype(bf16), wo_ref[...], preferred_element_type=f32)
    gate = mod_ref[0:1, 2 * D_MODEL:3 * D_MODEL]
    o_ref[...] = x_ref[...] + gate * o


def _mix_call(x2, attn, u, ga, gp, mod, wab, wg, ps, wpb, wo, tm):
    s = x2.shape[0]
    const2 = lambda i: (0, 0)
    hb = tm // HALO_POOL
    return pl.pallas_call(
        functools.partial(_mix_kernel, tm=tm),
        grid=(s // tm,),
        in_specs=[pl.BlockSpec((tm, D_MODEL), lambda i: (i, 0)),
                  pl.BlockSpec((tm, ATTN_WIDTH), lambda i: (i, 0)),
                  pl.BlockSpec((tm, POOL_WIDTH), lambda i: (i, 0)),
                  pl.BlockSpec((HALO_POOL, POOL_WIDTH), lambda i: (jnp.maximum(i * hb - 1, 0), 0)),
                  pl.BlockSpec((tm, D_MODEL), lambda i: (i, 0)),
                  pl.BlockSpec((tm, D_MODEL), lambda i: (i, 0)),
                  pl.BlockSpec((SUBLANE, N_MOD * D_MODEL), const2),
                  pl.BlockSpec((ATTN_WIDTH, D_MODEL), const2),
                  pl.BlockSpec((len(POOL_WINDOWS), POOL_GROUP_DIM, POOL_GROUP_DIM), lambda i: (0, 0, 0)),
                  pl.BlockSpec((1, POOL_WIDTH), const2),
                  pl.BlockSpec((POOL_WIDTH, D_MODEL), const2),
                  pl.BlockSpec((D_MODEL, D_MODEL), const2)],
        out_specs=pl.BlockSpec((tm, D_MODEL), lambda i: (i, 0)),
        out_shape=jax.ShapeDtypeStruct((s, D_MODEL), f32),
        compiler_params=pltpu.CompilerParams(dimension_semantics=("parallel",),
                                             vmem_limit_bytes=VMEM_LIMIT),
        name="mix",
    )(x2, attn, u, u, ga, gp, mod, wab, wg, ps, wpb, wo)


HALO_CONV = 8


def _ffn_kernel(x_ref, xh_ref, mod_ref, g2_ref, wup_ref, cw_ref, cb_ref, wdn_ref, o_ref, *, tm):
    i = pl.program_id(0)
    shift = mod_ref[0:1, 3 * D_MODEL:4 * D_MODEL]
    scale = mod_ref[0:1, 4 * D_MODEL:5 * D_MODEL]
    gate = mod_ref[0:1, 5 * D_MODEL:6 * D_MODEL]
    g2 = g2_ref[...]
    x = x_ref[...]
    h = _rms_modulate(x, g2, shift, scale)
    hh = jnp.where(i > 0, _rms_modulate(xh_ref[...], g2, shift, scale), 0.0)
    ha = jnp.concatenate([hh, h], axis=0).astype(bf16)
    up = jnp.dot(ha, wup_ref[...], preferred_element_type=f32)
    cw = cw_ref[...]
    y = cb_ref[...] + cw[0:1, :] * up[HALO_CONV - 2:HALO_CONV - 2 + tm]
    y = y + cw[1:2, :] * up[HALO_CONV - 1:HALO_CONV - 1 + tm]
    y = y + cw[2:3, :] * up[HALO_CONV:HALO_CONV + tm]
    a = y[:, 0:D_FF]
    b = y[:, D_FF:2 * D_FF]
    gated = (a * _sigmoid(a)) * b
    o = jnp.dot(gated.astype(bf16), wdn_ref[...], preferred_element_type=f32)
    o_ref[...] = x + gate * o


def _ffn_call(x1, mod, g2, wup, cw, cb, wdn, tm):
    s = x1.shape[0]
    const2 = lambda i: (0, 0)
    hb = tm // HALO_CONV
    return pl.pallas_call(
        functools.partial(_ffn_kernel, tm=tm),
        grid=(s // tm,),
        in_specs=[pl.BlockSpec((tm, D_MODEL), lambda i: (i, 0)),
                  pl.BlockSpec((HALO_CONV, D_MODEL), lambda i: (jnp.maximum(i * hb - 1, 0), 0)),
                  pl.BlockSpec((SUBLANE, N_MOD * D_MODEL), const2),
                  pl.BlockSpec((1, D_MODEL), const2),
                  pl.BlockSpec((D_MODEL, 2 * D_FF), const2, pipeline_mode=pl.Buffered(1)),
                  pl.BlockSpec((3, 2 * D_FF), const2),
                  pl.BlockSpec((1, 2 * D_FF), const2),
                  pl.BlockSpec((D_FF, D_MODEL), const2, pipeline_mode=pl.Buffered(1))],
        out_specs=pl.BlockSpec((tm, D_MODEL), lambda i: (i, 0)),
        out_shape=jax.ShapeDtypeStruct((s, D_MODEL), f32),
        compiler_params=pltpu.CompilerParams(dimension_semantics=("parallel",),
                                             vmem_limit_bytes=VMEM_LIMIT),
        name="ffn",
    )(x1, x1, mod, g2, wup, cw, cb, wdn)


def _pack_w_idx(w):
    return jnp.pad(w[:, IN_IDX0:IN_IDX1], ((0, 0), (0, 3 * LANE - (IN_IDX1 - IN_IDX0))))


def kernel(x, c, w_ada, b_ada, norm1_g, w_in, q_norm_g, k_norm_g, w_attn_br, w_pool_grp,
           pool_scale, w_pool_br, w_out, norm2_g, w_up, conv_w, conv_b, w_down):
    bsz, s, d = x.shape
    assert bsz == 1 and d == D_MODEL and s % (N_STAGE * TK) == 0
    depth = w_ada.shape[0]
    topk = min(TOPK_MAX, s // 4)
    x2 = x.reshape(s, d)
    c8 = jnp.pad(c, ((0, SUBLANE - bsz), (0, 0)))
    for l in range(depth):
        mod = _mod_call(c8, w_ada[l], b_ada[l].reshape(1, -1))
        qbd, k, vt, qi, ki, wit, u, ga, gp = _proj_call(
            x2, mod, norm1_g[l].reshape(1, -1),
            w_in[l][:, 0:IN_IDX0].astype(bf16), w_in[l][:, IN_IDX1:IN_END].astype(bf16),
            jnp.tile(q_norm_g[l], N_HEADS).reshape(1, -1),
            jnp.tile(k_norm_g[l], N_HEADS).reshape(1, -1), _pack_w_idx(w_in[l]), tm=512)
        attn = _attn_call(qbd, qi, wit, k, vt, ki, topk)
        x2 = _mix_call(x2, attn, u, ga, gp, mod, w_attn_br[l].astype(bf16),
                       w_pool_grp[l].astype(bf16), pool_scale[l].reshape(1, -1),
                       w_pool_br[l].astype(bf16), w_out[l].astype(bf16), tm=512)
        x2 = _ffn_call(x2, mod, norm2_g[l].reshape(1, -1), w_up[l].astype(bf16), conv_w[l],
                       conv_b[l].reshape(1, -1), w_down[l].astype(bf16), tm=512)
    return x2.reshape(bsz, s, d)
```

```python
import functools

import jax
import jax.numpy as jnp
from jax import lax
from jax.experimental import pallas as pl
from jax.experimental.pallas import tpu as pltpu

f32 = jnp.float32
bf16 = jnp.bfloat16
i32 = jnp.int32

D_MODEL = 1024
N_HEADS = 8
HEAD_DIM = 64
ATTN_WIDTH = N_HEADS * HEAD_DIM
IDX_HEADS = 4
IDX_DIM = 64
TOPK_MAX = 256
POOL_WINDOWS = (2, 4, 8, 16)
POOL_GROUP_DIM = 128
POOL_WIDTH = 512
D_FF = 2816
EPS = 1e-6
N_MOD = 6

LANE = 128
SUBLANE = 8
VMEM_LIMIT = 58 * 1024 * 1024

LOG2E = 1.4426950408889634
INT_MIN = -2147483648
NEG_INF = float("-inf")

IN_IDX0 = 3 * ATTN_WIDTH
IN_IDX1 = IN_IDX0 + IDX_HEADS * IDX_DIM + IDX_DIM + IDX_HEADS
IN_END = IN_IDX1 + POOL_WIDTH + 2 * D_MODEL
C_Q, C_K, C_V, C_QKV = 0, ATTN_WIDTH, 2 * ATTN_WIDTH, 3 * ATTN_WIDTH
C_U, C_GA, C_GP, C_REST = 0, POOL_WIDTH, POOL_WIDTH + D_MODEL, POOL_WIDTH + 2 * D_MODEL

TQ = 128
TK = 256
N_PAIR = N_HEADS // 2
IDX_K = 4 * LANE


def _sigmoid(x):
    return 1.0 / (1.0 + jnp.exp(-x))


def _rms_modulate(x, g, shift, scale):
    y = x * lax.rsqrt(jnp.mean(x * x, axis=-1, keepdims=True) + EPS)
    return (y * g) * (1.0 + scale) + shift


def _mod_kernel(c_ref, w_ref, b_ref, o_ref):
    c = c_ref[...]
    sc = c * _sigmoid(c)
    o_ref[...] = jnp.dot(sc, w_ref[...], precision=lax.Precision.HIGHEST,
                         preferred_element_type=f32) + b_ref[...]


def _mod_call(c8, w_ada, b_ada):
    n = w_ada.shape[1]
    tn = 1024
    return pl.pallas_call(
        _mod_kernel,
        grid=(n // tn,),
        in_specs=[pl.BlockSpec((SUBLANE, D_MODEL), lambda j: (0, 0)),
                  pl.BlockSpec((D_MODEL, tn), lambda j: (0, j)),
                  pl.BlockSpec((1, tn), lambda j: (0, j))],
        out_specs=pl.BlockSpec((SUBLANE, tn), lambda j: (0, j)),
        out_shape=jax.ShapeDtypeStruct((SUBLANE, n), f32),
        name="mod",
    )(c8, w_ada, b_ada)


def _split3(x):
    hi = x.astype(bf16).astype(f32)
    r = x - hi
    mid = r.astype(bf16).astype(f32)
    lo = (r - mid).astype(bf16).astype(f32)
    return hi, mid, lo


def _head_norm(z, g, bd):
    z2 = z * z
    hi = z2.astype(bf16)
    lo = (z2 - hi.astype(f32)).astype(bf16)
    ms = jnp.dot(hi, bd, preferred_element_type=f32) + jnp.dot(lo, bd, preferred_element_type=f32)
    return (z * lax.rsqrt(ms + EPS)) * g


def _proj_kernel(x_ref, mod_ref, g1_ref, wa_ref, wb_ref, qg_ref, kg_ref, widx_ref,
                 qbd_ref, k_ref, vt_ref, qi_ref, ki_ref, wit_ref, u_ref, ga_ref, gp_ref, *, tm):
    x = x_ref[...]
    shift = mod_ref[0:1, 0:D_MODEL]
    scale = mod_ref[0:1, D_MODEL:2 * D_MODEL]
    h = _rms_modulate(x, g1_ref[...], shift, scale)
    hb = h.astype(bf16)
    proj = jnp.dot(hb, wa_ref[...], preferred_element_type=f32)
    rest = jnp.dot(hb, wb_ref[...], preferred_element_type=f32)

    r = lax.broadcasted_iota(i32, (ATTN_WIDTH, ATTN_WIDTH), 0)
    c = lax.broadcasted_iota(i32, (ATTN_WIDTH, ATTN_WIDTH), 1)
    head_shift = HEAD_DIM.bit_length() - 1
    bd = jnp.where((r >> head_shift) == (c >> head_shift), 1.0 / HEAD_DIM, 0.0).astype(bf16)

    q = _head_norm(proj[:, C_Q:C_K], qg_ref[...], bd) * (HEAD_DIM ** -0.5 * LOG2E)
    k = _head_norm(proj[:, C_K:C_V], kg_ref[...], bd)
    k_ref[...] = k.astype(bf16)

    low = lax.broadcasted_iota(i32, (TQ, LANE), 1) < HEAD_DIM
    low_tm = lax.broadcasted_iota(i32, (tm, LANE), 1) < HEAD_DIM
    pidx = jnp.dot(h, widx_ref[...], precision=lax.Precision.HIGHEST, preferred_element_type=f32)
    qi = pidx[:, 0:IDX_HEADS * IDX_DIM] * (IDX_DIM ** -0.5)
    up64 = lambda z: pltpu.roll(z, HEAD_DIM, axis=1)
    for g in range(tm // TQ):
        rows = slice(g * TQ, (g + 1) * TQ)
        for p in range(N_PAIR):
            qp = q[rows, p * LANE:(p + 1) * LANE]
            bd_q = jnp.concatenate([jnp.where(low, qp, 0.0), jnp.where(low, 0.0, qp)], axis=0)
            qbd_ref[g, p] = bd_q.T.astype(bf16)
        for hh in range(IDX_HEADS):
            seg = qi[rows, (hh // 2) * LANE:(hh // 2 + 1) * LANE]
            own = low if hh % 2 == 0 else jnp.logical_not(low)
            qh, qm, ql = _split3(jnp.where(own, seg, 0.0))
            both = lambda z: z + up64(z)
            in_low = lambda z: jnp.where(low, both(z), 0.0)
            hcols = slice(hh * TQ, (hh + 1) * TQ)
            for kt, term in enumerate((both(qh), both(qm), in_low(ql), in_low(qh))):
                qi_ref[g, kt * LANE:(kt + 1) * LANE, hcols] = term.T.astype(bf16)

    v = proj[:, C_V:C_QKV]
    vt = v.T.astype(bf16)
    for cc in range(tm // TK):
        for hh in range(N_HEADS):
            vt_ref[cc, hh] = vt[hh * HEAD_DIM:(hh + 1) * HEAD_DIM, cc * TK:(cc + 1) * TK]

    kw = pidx[:, IDX_HEADS * IDX_DIM:3 * LANE]
    kh, km, kl = _split3(jnp.where(low_tm, kw, 0.0))
    ki_ref[:, 0:LANE] = (kh + up64(km)).astype(bf16)
    ki_ref[:, LANE:2 * LANE] = kl.astype(bf16)
    wt = (kw * (IDX_HEADS ** -0.5)).T
    wit_ref[...] = wt[HEAD_DIM:HEAD_DIM + SUBLANE, :]
    u_ref[...] = rest[:, C_U:C_GA]
    ga_ref[...] = _sigmoid(rest[:, C_GA:C_GP]).astype(bf16)
    gp_ref[...] = _sigmoid(rest[:, C_GP:C_REST]).astype(bf16)


def _proj_call(x2, mod, g1, w_qkv, w_rest, qg, kg, widx, tm):
    s = x2.shape[0]
    nq = s // TQ
    const = lambda i: (0, 0)
    return pl.pallas_call(
        functools.partial(_proj_kernel, tm=tm),
        grid=(s // tm,),
        in_specs=[pl.BlockSpec((tm, D_MODEL), lambda i: (i, 0)),
                  pl.BlockSpec((SUBLANE, N_MOD * D_MODEL), const),
                  pl.BlockSpec((1, D_MODEL), const),
                  pl.BlockSpec((D_MODEL, C_QKV), const),
                  pl.BlockSpec((D_MODEL, C_REST), const),
                  pl.BlockSpec((1, ATTN_WIDTH), const),
                  pl.BlockSpec((1, ATTN_WIDTH), const),
                  pl.BlockSpec((D_MODEL, 3 * LANE), const)],
        out_specs=[pl.BlockSpec((tm // TQ, N_PAIR, LANE, 2 * TQ), lambda i: (i, 0, 0, 0)),
                   pl.BlockSpec((tm, ATTN_WIDTH), lambda i: (i, 0)),
                   pl.BlockSpec((tm // TK, N_HEADS, HEAD_DIM, TK), lambda i: (i, 0, 0, 0)),
                   pl.BlockSpec((tm // TQ, IDX_K, IDX_HEADS * TQ), lambda i: (i, 0, 0)),
                   pl.BlockSpec((tm, 2 * LANE), lambda i: (i, 0)),
                   pl.BlockSpec((SUBLANE, tm), lambda i: (0, i)),
                   pl.BlockSpec((tm, POOL_WIDTH), lambda i: (i, 0)),
                   pl.BlockSpec((tm, D_MODEL), lambda i: (i, 0)),
                   pl.BlockSpec((tm, D_MODEL), lambda i: (i, 0))],
        out_shape=[jax.ShapeDtypeStruct((nq, N_PAIR, LANE, 2 * TQ), bf16),
                   jax.ShapeDtypeStruct((s, ATTN_WIDTH), bf16),
                   jax.ShapeDtypeStruct((s // TK, N_HEADS, HEAD_DIM, TK), bf16),
                   jax.ShapeDtypeStruct((nq, IDX_K, IDX_HEADS * TQ), bf16),
                   jax.ShapeDtypeStruct((s, 2 * LANE), bf16),
                   jax.ShapeDtypeStruct((SUBLANE, s), f32),
                   jax.ShapeDtypeStruct((s, POOL_WIDTH), f32),
                   jax.ShapeDtypeStruct((s, D_MODEL), bf16),
                   jax.ShapeDtypeStruct((s, D_MODEL), bf16)],
        compiler_params=pltpu.CompilerParams(dimension_semantics=("parallel",),
                                             vmem_limit_bytes=VMEM_LIMIT),
        name="proj",
    )(x2, mod, g1, w_qkv, w_rest, qg, kg, widx)


def _slope2(h):
    return (2.0 ** (-8.0 * (h + 1) / N_HEADS)) * LOG2E


def _tree(op, xs):
    xs = list(xs)
    while len(xs) > 1:
        nxt = [op(xs[a], xs[a + 1]) for a in range(0, len(xs) - 1, 2)]
        if len(xs) % 2:
            nxt.append(xs[-1])
        xs = nxt
    return xs[0]


def _row_groups(x):
    return [x[j * SUBLANE:(j + 1) * SUBLANE] for j in range(x.shape[0] // SUBLANE)]


N_CNT_ACC = 8
N_STAGE = 4
ATTN_UNROLL = 8
KI_BUFS = 4
PIPE_UNROLL = 4
EARLY_CANDS = (0, 1 << 30, INT_MIN + (1 << 30))
PV_ROWS = HEAD_DIM + 16


def _attn_kernel(qbd_ref, qi_ref, wi_ref, k_ref, vt_ref, ki_ref, o_ref,
                 keys_ref, cnt0_ref, pos_ref, slf_ref, il0_ref, il1_ref,
                 pre0_ref, pre1_ref, pre2_ref, pre3_ref, sm0_ref, sm1_ref, cmax0_ref, cmax1_ref,
                 p0_ref, p1_ref, alpha0_ref, alpha1_ref, m_ref, acc_ref, kibuf_ref, kisem_ref, *, topk):
    il_ref = (il0_ref, il1_ref)
    pre_ref = (pre0_ref, pre1_ref, pre2_ref, pre3_ref)
    sm_ref = (sm0_ref, sm1_ref)
    cmax_ref = (cmax0_ref, cmax1_ref)
    p_ref = (p0_ref, p1_ref)
    alpha_ref = (alpha0_ref, alpha1_ref)
    i = pl.program_id(0)
    nsc = (i + 4) >> 2
    nch = 2 * nsc
    nbody = (i + 8) >> 3
    n_causal = (i + 2) >> 1
    n_full, n_tail = n_causal >> 2, n_causal & 3

    def chunk_start(c):
        return pl.multiple_of(c * TK, TK)

    def paired_loop(body, n, carry):
        def run(first, width, trips, carry):
            def it(b, cr):
                for j in range(width):
                    cr = body(first + b * width + j, cr)
                return cr
            return lax.fori_loop(0, trips, it, carry)

        shift = PIPE_UNROLL.bit_length() - 1
        carry = run(0, PIPE_UNROLL, n >> shift, carry)
        done = (n >> shift) << shift
        for sh in range(shift - 1, -1, -1):
            take = (n >> sh) & 1
            carry = run(done, 1 << sh, take, carry)
            done = done + (take << sh)
        return carry

    def pipeline(produce, consume, carry):
        for j in range(N_STAGE):
            produce(j, j)

        def body(b, carry):
            c0 = N_STAGE * b
            for j in range(N_STAGE):
                carry = consume(c0 + j, j, carry)
                produce(c0 + N_STAGE + j, j)
            return carry

        carry = paired_loop(body, nbody - 1, carry)
        c0 = N_STAGE * (nbody - 1)
        for j in range(N_STAGE):
            carry = consume(c0 + j, j, carry)
        return carry

    @pl.when(i == 0)
    def _():
        col = lax.broadcasted_iota(i32, (TK, LANE), 1)
        row = lax.broadcasted_iota(i32, (TK, LANE), 0).astype(f32)
        pos_ref[...] = jnp.where(col < 3, row, 0.0).astype(bf16)
        term = lax.broadcasted_iota(i32, (LANE, 2 * TQ), 0)
        lane = lax.broadcasted_iota(i32, (LANE, 2 * TQ), 1)
        for p in range(N_PAIR):
            sl = jnp.where(lane < TQ, _slope2(2 * p), _slope2(2 * p + 1)) + jnp.zeros((LANE, 2 * TQ), f32)
            hi, mid, lo = _split3(sl)
            slf_ref[p] = jnp.where(term == 0, hi, jnp.where(term == 1, mid,
                                   jnp.where(term == 2, lo, 0.0))).astype(bf16)

    qi = qi_ref[0]
    w = wi_ref[...]
    d0 = (lax.broadcasted_iota(i32, (TK, TQ), 1) - lax.broadcasted_iota(i32, (TK, TQ), 0))

    n_units = 2 * nbody

    def ki_copy(u):
        ring = u & (KI_BUFS - 1)
        return pltpu.make_async_copy(ki_ref.at[pl.ds(u * (2 * TK), 2 * TK), :],
                                     kibuf_ref.at[ring], kisem_ref.at[ring])

    ki_copy(0).start()
    ki_copy(1).start()
    for u in range(2, KI_BUFS):
        @pl.when(u < n_units)
        def _():
            ki_copy(u).start()

    def idx_matmul(u, slot):
        ring = u & (KI_BUFS - 1)
        ki_copy(u).wait()
        kx = kibuf_ref[ring, :, 0:LANE]
        ky = kibuf_ref[ring, :, LANE:2 * LANE]
        il_ref[slot][...] = jnp.dot(jnp.concatenate([kx, kx, kx, ky], axis=1), qi,
                                    preferred_element_type=f32)

        @pl.when(u + KI_BUFS < n_units)
        def _():
            ki_copy(u + KI_BUFS).start()

    def idx_keys(c, slot, half):
        r0 = chunk_start(c)
        rows = slice(half * TK, (half + 1) * TK)
        sc = jnp.maximum(il_ref[slot][rows, 0:TQ], 0.0) * w[0:1, :]
        for hh in range(1, IDX_HEADS):
            sc = sc + jnp.maximum(il_ref[slot][rows, hh * TQ:(hh + 1) * TQ], 0.0) * w[hh:hh + 1, :]
        b = lax.bitcast_convert_type(sc, i32)
        key = jnp.where(b < 0, -(b & 0x7FFFFFFF), b)
        valid = d0 >= (r0 - i * TQ)
        key = jnp.where(valid, key, INT_MIN)
        keys_ref[pl.ds(r0, TK), :] = key
        for n, cand in enumerate(EARLY_CANDS):
            cnt0_ref[n] = cnt0_ref[n] + _tree(jnp.add, _row_groups(jnp.where(key >= cand, 1, 0)))

    cnt0_ref[...] = jnp.zeros(cnt0_ref.shape, i32)

    idx_matmul(0, 0)
    idx_matmul(1, 1)

    def idx_body(b, carry):
        c0 = N_STAGE * b
        idx_keys(c0, 0, 0)
        idx_keys(c0 + 1, 0, 1)
        idx_matmul(2 * b + 2, 0)
        idx_keys(c0 + 2, 1, 0)
        idx_keys(c0 + 3, 1, 1)
        idx_matmul(2 * b + 3, 1)
        return carry

    paired_loop(idx_body, nbody - 1, 0)
    c_last = N_STAGE * (nbody - 1)
    for j in range(N_STAGE):
        idx_keys(c_last + j, j // 2, j % 2)

    def count_ge(cand):
        def add_chunk(c, accs):
            ind = jnp.where(keys_ref[pl.ds(chunk_start(c), TK), :] >= cand, 1, 0)
            for j, g in enumerate(_row_groups(ind)):
                accs[j % N_CNT_ACC] = accs[j % N_CNT_ACC] + g

        def body(b, accs):
            accs = list(accs)
            for part in range(N_STAGE):
                add_chunk(b * N_STAGE + part, accs)
            return tuple(accs)

        def tail(j, accs):
            accs = list(accs)
            add_chunk(n_full * N_STAGE + j, accs)
            return tuple(accs)

        accs = lax.fori_loop(0, n_full, body,
                             tuple(jnp.zeros((SUBLANE, TQ), i32) for _ in range(N_CNT_ACC)))
        accs = lax.fori_loop(0, n_tail, tail, accs)
        return jnp.sum(_tree(jnp.add, accs), axis=0, keepdims=True)

    def bit_step(bi, st):
        t, above = st
        cand = t + lax.shift_left(jnp.int32(1), 31 - bi)
        cnt = count_ge(cand)
        up = cnt >= topk
        return jnp.where(up, cand, t), jnp.where(up, above, cnt)

    c_zero, c_pos, c_neg = (jnp.sum(cnt0_ref[n], axis=0, keepdims=True) for n in range(len(EARLY_CANDS)))
    up0 = c_zero >= topk
    t = jnp.where(up0, 0, INT_MIN)
    above = jnp.where(up0, 0, c_zero)
    c_one = jnp.where(up0, c_pos, c_neg)
    up1 = c_one >= topk
    t = jnp.where(up1, t + (1 << 30), t)
    above = jnp.where(up1, above, c_one)
    t, above = lax.fori_loop(2, 32, bit_step, (t, above))
    t = jnp.maximum(t, INT_MIN + 1)
    r_tie = (topk - above).astype(f32)

    tri = jnp.where(lax.broadcasted_iota(i32, (TK, TK), 0) >= lax.broadcasted_iota(i32, (TK, TK), 1),
                    1.0, 0.0).astype(bf16)

    def tie_rank(c, slot):
        e = jnp.where(keys_ref[pl.ds(chunk_start(c), TK), :] == t, 1.0, 0.0).astype(bf16)
        pre_ref[slot][...] = jnp.dot(tri, e, preferred_element_type=f32)

    def mask_out(c, slot, rank):
        r0 = chunk_start(c)
        kk = keys_ref[pl.ds(r0, TK), :]
        pre = pre_ref[slot][...] + rank
        nm = jnp.where(kk > t, 0.0, jnp.where(kk == t, jnp.where(pre <= r_tie, 0.0, NEG_INF), NEG_INF))
        keys_ref[pl.ds(r0, TK), :] = lax.bitcast_convert_type(nm, i32)
        return pre[TK - 1:TK, :]

    pipeline(tie_rank, mask_out, jnp.zeros((1, TQ), f32))

    m_ref[...] = jnp.full(m_ref.shape, NEG_INF, f32)
    acc_ref[...] = jnp.zeros(acc_ref.shape, f32)
    lane2 = lax.broadcasted_iota(i32, (1, 2 * TQ), 1)
    ones_rows = jnp.ones((PV_ROWS - HEAD_DIM, TK), bf16)

    def logits(c, slot):
        rows = pl.ds(chunk_start(c), TK)
        nm = lax.bitcast_convert_type(keys_ref[rows, :], f32)
        nm2 = jnp.concatenate([nm, nm], axis=1)
        for p in range(N_PAIR):
            lhs = jnp.concatenate([k_ref[rows, p * LANE:(p + 1) * LANE], pos_ref[...]], axis=1)
            rhs = jnp.concatenate([qbd_ref[0, p], slf_ref[p]], axis=0)
            sm = jnp.dot(lhs, rhs, preferred_element_type=f32) + nm2
            sm_ref[slot][p] = sm
            cmax_ref[slot][p] = jnp.max(_tree(jnp.maximum, _row_groups(sm)), axis=0, keepdims=True)

    def probs(c, slot):
        r0f = jnp.asarray(c * TK, dtype=f32)
        for p in range(N_PAIR):
            coff = jnp.where(lane2 < TQ, _slope2(2 * p), _slope2(2 * p + 1)) * r0f
            m_old = m_ref[p]
            m_new = jnp.maximum(m_old, cmax_ref[slot][p] + coff)
            m_safe = jnp.where(m_new == NEG_INF, 0.0, m_new)
            alpha_ref[slot][p] = jnp.where(m_old == NEG_INF, 0.0, jnp.exp2(m_old - m_safe))
            p_ref[slot][p] = jnp.exp2((sm_ref[slot][p] - (m_safe - coff)).astype(bf16))
            m_ref[p] = m_new

    def weighted_sum(c, slot):
        for h in range(N_HEADS):
            lanes = slice((h % 2) * TQ, (h % 2 + 1) * TQ)
            lhs = jnp.concatenate([vt_ref[c, h], ones_rows], axis=0)
            pv = jnp.dot(lhs, p_ref[slot][h // 2, :, lanes], preferred_element_type=f32)
            acc_ref[h] = acc_ref[h] * alpha_ref[slot][h // 2, :, lanes] + pv

    logits(0, 0)
    probs(0, 0)
    logits(1, 1)

    def attn_step(c0):
        probs(c0 + 1, 1)
        logits(c0 + 2, 0)
        logits(c0 + 3, 1)
        weighted_sum(c0, 0)
        probs(c0 + 2, 0)
        weighted_sum(c0 + 1, 1)

    n_steps = nsc - 1

    def run_steps(first_step, width, trips):
        def body(b, carry):
            for j in range(width):
                attn_step(2 * (first_step + b * width + j))
            return carry
        lax.fori_loop(0, trips, body, 0)

    shift = ATTN_UNROLL.bit_length() - 1
    run_steps(0, ATTN_UNROLL, n_steps >> shift)
    done = (n_steps >> shift) << shift
    for sh in range(shift - 1, -1, -1):
        take = (n_steps >> sh) & 1
        run_steps(done, 1 << sh, take)
        done = done + (take << sh)
    weighted_sum(nch - 2, 0)
    probs(nch - 1, 1)
    weighted_sum(nch - 1, 1)

    outs = []
    for h in range(N_HEADS):
        a = acc_ref[h]
        outs.append(a[0:HEAD_DIM] / a[HEAD_DIM:HEAD_DIM + 1])
    o_ref[...] = jnp.concatenate(outs, axis=0).T.astype(bf16)


def _attn_call(qbd, qi, wit, k, vt, ki, topk):
    s = k.shape[0]
    nq = s // TQ
    whole = pl.BlockSpec(memory_space=pltpu.VMEM)
    il = [pltpu.VMEM((2 * TK, IDX_HEADS * TQ), f32)] * 2
    pre = [pltpu.VMEM((TK, TQ), f32)] * N_STAGE
    return pl.pallas_call(
        functools.partial(_attn_kernel, topk=topk),
        grid=(nq,),
        in_specs=[pl.BlockSpec((1, N_PAIR, LANE, 2 * TQ), lambda i: (i, 0, 0, 0)),
                  pl.BlockSpec((1, IDX_K, IDX_HEADS * TQ), lambda i: (i, 0, 0)),
                  pl.BlockSpec((SUBLANE, TQ), lambda i: (0, i)),
                  whole, whole, pl.BlockSpec(memory_space=pl.ANY)],
        out_specs=pl.BlockSpec((TQ, ATTN_WIDTH), lambda i: (i, 0)),
        out_shape=jax.ShapeDtypeStruct((s, ATTN_WIDTH), bf16),
        scratch_shapes=[pltpu.VMEM((s, TQ), i32),
                        pltpu.VMEM((len(EARLY_CANDS), SUBLANE, TQ), i32),
                        pltpu.VMEM((TK, LANE), bf16),
                        pltpu.VMEM((N_PAIR, LANE, 2 * TQ), bf16),
                        *il, *pre,
                        pltpu.VMEM((N_PAIR, TK, 2 * TQ), f32),
                        pltpu.VMEM((N_PAIR, TK, 2 * TQ), f32),
                        pltpu.VMEM((N_PAIR, 1, 2 * TQ), f32),
                        pltpu.VMEM((N_PAIR, 1, 2 * TQ), f32),
                        pltpu.VMEM((N_PAIR, TK, 2 * TQ), bf16),
                        pltpu.VMEM((N_PAIR, TK, 2 * TQ), bf16),
                        pltpu.VMEM((N_PAIR, 1, 2 * TQ), f32),
                        pltpu.VMEM((N_PAIR, 1, 2 * TQ), f32),
                        pltpu.VMEM((N_PAIR, 1, 2 * TQ), f32),
                        pltpu.VMEM((N_HEADS, PV_ROWS, TQ), f32),
                        pltpu.VMEM((KI_BUFS, 2 * TK, 2 * LANE), bf16),
                        pltpu.SemaphoreType.DMA((KI_BUFS,))],
        compiler_params=pltpu.CompilerParams(dimension_semantics=("arbitrary",),
                                             vmem_limit_bytes=VMEM_LIMIT),
        name="attn",
    )(qbd, qi, wit, k, vt, ki)


HALO_POOL = 16


def _mix_kernel(x_ref, attn_ref, u_ref, uh_ref, ga_ref, gp_ref, mod_ref,
                wab_ref, wg_ref, ps_ref, wpb_ref, wo_ref, o_ref, *, tm):
    i = pl.program_id(0)
    y_attn = jnp.dot(attn_ref[...], wab_ref[...], preferred_element_type=f32)

    u = u_ref[...]
    halo = jnp.where(i > 0, uh_ref[...], 0.0)
    a = jnp.concatenate([halo, u], axis=0)
    tpos = (i * tm + lax.broadcasted_iota(i32, (tm, POOL_GROUP_DIM), 0) + 1).astype(f32)
    mixed = []
    for g, wdw in enumerate(POOL_WINDOWS):
        ag = a[:, g * POOL_GROUP_DIM:(g + 1) * POOL_GROUP_DIM]
        ug = ag[HALO_POOL:HALO_POOL + tm]
        ssum = ug
        for j in range(1, wdw):
            ssum = ssum + ag[HALO_POOL - j:HALO_POOL - j + tm]
        pooled = ssum / jnp.minimum(tpos, float(wdw)) - ug
        mixed.append(jnp.dot(pooled.astype(bf16), wg_ref[g], preferred_element_type=f32))
    mixed = jnp.concatenate(mixed, axis=1) * ps_ref[...]
    y_pool = jnp.dot(mixed.astype(bf16), wpb_ref[...], preferred_element_type=f32)

    merged = ga_ref[...].astype(f32) * y_attn + gp_ref[...].astype(f32) * y_pool
    o = jnp.dot(merged.astype(bf16), wo_ref[...], preferred_element_type=f32)
    gate = mod_ref[0:1, 2 * D_MODEL:3 * D_MODEL]
    o_ref[...] = x_ref[...] + gate * o


def _mix_call(x2, attn, u, ga, gp, mod, wab, wg, ps, wpb, wo, tm):
    s = x2.shape[0]
    const2 = lambda i: (0, 0)
    hb = tm // HALO_POOL
    return pl.pallas_call(
        functools.partial(_mix_kernel, tm=tm),
        grid=(s // tm,),
        in_specs=[pl.BlockSpec((tm, D_MODEL), lambda i: (i, 0)),
                  pl.BlockSpec((tm, ATTN_WIDTH), lambda i: (i, 0)),
                  pl.BlockSpec((tm, POOL_WIDTH), lambda i: (i, 0)),
                  pl.BlockSpec((HALO_POOL, POOL_WIDTH), lambda i: (jnp.maximum(i * hb - 1, 0), 0)),
                  pl.BlockSpec((tm, D_MODEL), lambda i: (i, 0)),
                  pl.BlockSpec((tm, D_MODEL), lambda i: (i, 0)),
                  pl.BlockSpec((SUBLANE, N_MOD * D_MODEL), const2),
                  pl.BlockSpec((ATTN_WIDTH, D_MODEL), const2),
                  pl.BlockSpec((len(POOL_WINDOWS), POOL_GROUP_DIM, POOL_GROUP_DIM), lambda i: (0, 0, 0)),
                  pl.BlockSpec((1, POOL_WIDTH), const2),
                  pl.BlockSpec((POOL_WIDTH, D_MODEL), const2),
                  pl.BlockSpec((D_MODEL, D_MODEL), const2)],
        out_specs=pl.BlockSpec((tm, D_MODEL), lambda i: (i, 0)),
        out_shape=jax.ShapeDtypeStruct((s, D_MODEL), f32),
        compiler_params=pltpu.CompilerParams(dimension_semantics=("parallel",),
                                             vmem_limit_bytes=VMEM_LIMIT),
        name="mix",
    )(x2, attn, u, u, ga, gp, mod, wab, wg, ps, wpb, wo)


HALO_CONV = 8


def _ffn_kernel(x_ref, xh_ref, mod_ref, g2_ref, wup_ref, cw_ref, cb_ref, wdn_ref, o_ref, *, tm):
    i = pl.program_id(0)
    shift = mod_ref[0:1, 3 * D_MODEL:4 * D_MODEL]
    scale = mod_ref[0:1, 4 * D_MODEL:5 * D_MODEL]
    gate = mod_ref[0:1, 5 * D_MODEL:6 * D_MODEL]
    g2 = g2_ref[...]
    x = x_ref[...]
    h = _rms_modulate(x, g2, shift, scale)
    hh = jnp.where(i > 0, _rms_modulate(xh_ref[...], g2, shift, scale), 0.0)
    ha = jnp.concatenate([hh, h], axis=0).astype(bf16)
    up = jnp.dot(ha, wup_ref[...], preferred_element_type=f32)
    cw = cw_ref[...]
    y = cb_ref[...] + cw[0:1, :] * up[HALO_CONV - 2:HALO_CONV - 2 + tm]
    y = y + cw[1:2, :] * up[HALO_CONV - 1:HALO_CONV - 1 + tm]
    y = y + cw[2:3, :] * up[HALO_CONV:HALO_CONV + tm]
    a = y[:, 0:D_FF]
    b = y[:, D_FF:2 * D_FF]
    gated = (a * _sigmoid(a)) * b
    o = jnp.dot(gated.astype(bf16), wdn_ref[...], preferred_element_type=f32)
    o_ref[...] = x + gate * o


def _ffn_call(x1, mod, g2, wup, cw, cb, wdn, tm):
    s = x1.shape[0]
    const2 = lambda i: (0, 0)
    hb = tm // HALO_CONV
    return pl.pallas_call(
        functools.partial(_ffn_kernel, tm=tm),
        grid=(s // tm,),
        in_specs=[pl.BlockSpec((tm, D_MODEL), lambda i: (i, 0)),
                  pl.BlockSpec((HALO_CONV, D_MODEL), lambda i: (jnp.maximum(i * hb - 1, 0), 0)),
                  pl.BlockSpec((SUBLANE, N_MOD * D_MODEL), const2),
                  pl.BlockSpec((1, D_MODEL), const2),
                  pl.BlockSpec((D_MODEL, 2 * D_FF), const2, pipeline_mode=pl.Buffered(1)),
                  pl.BlockSpec((3, 2 * D_FF), const2),
                  pl.BlockSpec((1, 2 * D_FF), const2),
                  pl.BlockSpec((D_FF, D_MODEL), const2, pipeline_mode=pl.Buffered(1))],
        out_specs=pl.BlockSpec((tm, D_MODEL), lambda i: (i, 0)),
        out_shape=jax.ShapeDtypeStruct((s, D_MODEL), f32),
        compiler_params=pltpu.CompilerParams(dimension_semantics=("parallel",),
                                             vmem_limit_bytes=VMEM_LIMIT),
        name="ffn",
    )(x1, x1, mod, g2, wup, cw, cb, wdn)


def _pack_w_idx(w):
    return jnp.pad(w[:, IN_IDX0:IN_IDX1], ((0, 0), (0, 3 * LANE - (IN_IDX1 - IN_IDX0))))


def kernel(x, c, w_ada, b_ada, norm1_g, w_in, q_norm_g, k_norm_g, w_attn_br, w_pool_grp,
           pool_scale, w_pool_br, w_out, norm2_g, w_up, conv_w, conv_b, w_down):
    bsz, s, d = x.shape
    assert bsz == 1 and d == D_MODEL and s % (N_STAGE * TK) == 0
    depth = w_ada.shape[0]
    topk = min(TOPK_MAX, s // 4)
    x2 = x.reshape(s, d)
    c8 = jnp.pad(c, ((0, SUBLANE - bsz), (0, 0)))
    for l in range(depth):
        mod = _mod_call(c8, w_ada[l], b_ada[l].reshape(1, -1))
        qbd, k, vt, qi, ki, wit, u, ga, gp = _proj_call(
            x2, mod, norm1_g[l].reshape(1, -1),
            w_in[l][:, 0:IN_IDX0].astype(bf16), w_in[l][:, IN_IDX1:IN_END].astype(bf16),
            jnp.tile(q_norm_g[l], N_HEADS).reshape(1, -1),
            jnp.tile(k_norm_g[l], N_HEADS).reshape(1, -1), _pack_w_idx(w_in[l]), tm=512)
        attn = _attn_call(qbd, qi, wit, k, vt, ki, topk)
        x2 = _mix_call(x2, attn, u, ga, gp, mod, w_attn_br[l].astype(bf16),
                       w_pool_grp[l].astype(bf16), pool_scale[l].reshape(1, -1),
                       w_pool_br[l].astype(bf16), w_out[l].astype(bf16), tm=512)
        x2 = _ffn_call(x2, mod, norm2_g[l].reshape(1, -1), w_up[l].astype(bf16), conv_w[l],
                       conv_b[l].reshape(1, -1), w_down[l].astype(bf16), tm=512)
    return x2.reshape(bsz, s, d)
```

```python
import functools

import jax
import jax.numpy as jnp
from jax import lax
from jax.experimental import pallas as pl
from jax.experimental.pallas import tpu as pltpu

f32 = jnp.float32
bf16 = jnp.bfloat16
i32 = jnp.int32

D_MODEL = 1024
N_HEADS = 8
HEAD_DIM = 64
ATTN_WIDTH = N_HEADS * HEAD_DIM
IDX_HEADS = 4
IDX_DIM = 64
TOPK_MAX = 256
POOL_WINDOWS = (2, 4, 8, 16)
POOL_GROUP_DIM = 128
POOL_WIDTH = 512
D_FF = 2816
EPS = 1e-6
N_MOD = 6

LANE = 128
SUBLANE = 8
VMEM_LIMIT = 58 * 1024 * 1024

LOG2E = 1.4426950408889634
INT_MIN = -2147483648
NEG_INF = float("-inf")

IN_IDX0 = 3 * ATTN_WIDTH
IN_IDX1 = IN_IDX0 + IDX_HEADS * IDX_DIM + IDX_DIM + IDX_HEADS
IN_END = IN_IDX1 + POOL_WIDTH + 2 * D_MODEL
C_Q, C_K, C_V, C_QKV = 0, ATTN_WIDTH, 2 * ATTN_WIDTH, 3 * ATTN_WIDTH
C_U, C_GA, C_GP, C_REST = 0, POOL_WIDTH, POOL_WIDTH + D_MODEL, POOL_WIDTH + 2 * D_MODEL

TQ = 128
TK = 256
N_PAIR = N_HEADS // 2
IDX_K = 4 * LANE


def _sigmoid(x):
    return 1.0 / (1.0 + jnp.exp(-x))


def _rms_modulate(x, g, shift, scale):
    y = x * lax.rsqrt(jnp.mean(x * x, axis=-1, keepdims=True) + EPS)
    return (y * g) * (1.0 + scale) + shift


def _mod_kernel(c_ref, w_ref, b_ref, o_ref):
    c = c_ref[...]
    sc = c * _sigmoid(c)
    o_ref[...] = jnp.dot(sc, w_ref[...], precision=lax.Precision.HIGHEST,
                         preferred_element_type=f32) + b_ref[...]


def _mod_call(c8, w_ada, b_ada):
    n = w_ada.shape[1]
    tn = 1024
    return pl.pallas_call(
        _mod_kernel,
        grid=(n // tn,),
        in_specs=[pl.BlockSpec((SUBLANE, D_MODEL), lambda j: (0, 0)),
                  pl.BlockSpec((D_MODEL, tn), lambda j: (0, j)),
                  pl.BlockSpec((1, tn), lambda j: (0, j))],
        out_specs=pl.BlockSpec((SUBLANE, tn), lambda j: (0, j)),
        out_shape=jax.ShapeDtypeStruct((SUBLANE, n), f32),
        name="mod",
    )(c8, w_ada, b_ada)


def _split3(x):
    hi = x.astype(bf16).astype(f32)
    r = x - hi
    mid = r.astype(bf16).astype(f32)
    lo = (r - mid).astype(bf16).astype(f32)
    return hi, mid, lo


def _head_norm(z, g, bd):
    z2 = z * z
    hi = z2.astype(bf16)
    lo = (z2 - hi.astype(f32)).astype(bf16)
    ms = jnp.dot(hi, bd, preferred_element_type=f32) + jnp.dot(lo, bd, preferred_element_type=f32)
    return (z * lax.rsqrt(ms + EPS)) * g


def _proj_kernel(x_ref, mod_ref, g1_ref, wa_ref, wb_ref, qg_ref, kg_ref, widx_ref,
                 qbd_ref, k_ref, vt_ref, qi_ref, ki_ref, wit_ref, u_ref, ga_ref, gp_ref, *, tm):
    x = x_ref[...]
    shift = mod_ref[0:1, 0:D_MODEL]
    scale = mod_ref[0:1, D_MODEL:2 * D_MODEL]
    h = _rms_modulate(x, g1_ref[...], shift, scale)
    hb = h.astype(bf16)
    proj = jnp.dot(hb, wa_ref[...], preferred_element_type=f32)
    rest = jnp.dot(hb, wb_ref[...], preferred_element_type=f32)

    r = lax.broadcasted_iota(i32, (ATTN_WIDTH, ATTN_WIDTH), 0)
    c = lax.broadcasted_iota(i32, (ATTN_WIDTH, ATTN_WIDTH), 1)
    head_shift = HEAD_DIM.bit_length() - 1
    bd = jnp.where((r >> head_shift) == (c >> head_shift), 1.0 / HEAD_DIM, 0.0).astype(bf16)

    q = _head_norm(proj[:, C_Q:C_K], qg_ref[...], bd) * (HEAD_DIM ** -0.5 * LOG2E)
    k = _head_norm(proj[:, C_K:C_V], kg_ref[...], bd)
    k_ref[...] = k.astype(bf16)

    low = lax.broadcasted_iota(i32, (TQ, LANE), 1) < HEAD_DIM
    low_tm = lax.broadcasted_iota(i32, (tm, LANE), 1) < HEAD_DIM
    pidx = jnp.dot(h, widx_ref[...], precision=lax.Precision.HIGHEST, preferred_element_type=f32)
    qi = pidx[:, 0:IDX_HEADS * IDX_DIM] * (IDX_DIM ** -0.5)
    up64 = lambda z: pltpu.roll(z, HEAD_DIM, axis=1)
    for g in range(tm // TQ):
        rows = slice(g * TQ, (g + 1) * TQ)
        for p in range(N_PAIR):
            qp = q[rows, p * LANE:(p + 1) * LANE]
            bd_q = jnp.concatenate([jnp.where(low, qp, 0.0), jnp.where(low, 0.0, qp)], axis=0)
            qbd_ref[g, p] = bd_q.T.astype(bf16)
        for hh in range(IDX_HEADS):
            seg = qi[rows, (hh // 2) * LANE:(hh // 2 + 1) * LANE]
            own = low if hh % 2 == 0 else jnp.logical_not(low)
            qh, qm, ql = _split3(jnp.where(own, seg, 0.0))
            both = lambda z: z + up64(z)
            in_low = lambda z: jnp.where(low, both(z), 0.0)
            hcols = slice(hh * TQ, (hh + 1) * TQ)
            for kt, term in enumerate((both(qh), both(qm), in_low(ql), in_low(qh))):
                qi_ref[g, kt * LANE:(kt + 1) * LANE, hcols] = term.T.astype(bf16)

    v = proj[:, C_V:C_QKV]
    vt = v.T.astype(bf16)
    for cc in range(tm // TK):
        for hh in range(N_HEADS):
            vt_ref[cc, hh] = vt[hh * HEAD_DIM:(hh + 1) * HEAD_DIM, cc * TK:(cc + 1) * TK]

    kw = pidx[:, IDX_HEADS * IDX_DIM:3 * LANE]
    kh, km, kl = _split3(jnp.where(low_tm, kw, 0.0))
    ki_ref[:, 0:LANE] = (kh + up64(km)).astype(bf16)
    ki_ref[:, LANE:2 * LANE] = kl.astype(bf16)
    wt = (kw * (IDX_HEADS ** -0.5)).T
    wit_ref[...] = wt[HEAD_DIM:HEAD_DIM + SUBLANE, :]
    u_ref[...] = rest[:, C_U:C_GA]
    ga_ref[...] = _sigmoid(rest[:, C_GA:C_GP]).astype(bf16)
    gp_ref[...] = _sigmoid(rest[:, C_GP:C_REST]).astype(bf16)


def _proj_call(x2, mod, g1, w_qkv, w_rest, qg, kg, widx, tm):
    s = x2.shape[0]
    nq = s // TQ
    const = lambda i: (0, 0)
    return pl.pallas_call(
        functools.partial(_proj_kernel, tm=tm),
        grid=(s // tm,),
        in_specs=[pl.BlockSpec((tm, D_MODEL), lambda i: (i, 0)),
                  pl.BlockSpec((SUBLANE, N_MOD * D_MODEL), const),
                  pl.BlockSpec((1, D_MODEL), const),
                  pl.BlockSpec((D_MODEL, C_QKV), const),
                  pl.BlockSpec((D_MODEL, C_REST), const),
                  pl.BlockSpec((1, ATTN_WIDTH), const),
                  pl.BlockSpec((1, ATTN_WIDTH), const),
                  pl.BlockSpec((D_MODEL, 3 * LANE), const)],
        out_specs=[pl.BlockSpec((tm // TQ, N_PAIR, LANE, 2 * TQ), lambda i: (i, 0, 0, 0)),
                   pl.BlockSpec((tm, ATTN_WIDTH), lambda i: (i, 0)),
                   pl.BlockSpec((tm // TK, N_HEADS, HEAD_DIM, TK), lambda i: (i, 0, 0, 0)),
                   pl.BlockSpec((tm // TQ, IDX_K, IDX_HEADS * TQ), lambda i: (i, 0, 0)),
                   pl.BlockSpec((tm, 2 * LANE), lambda i: (i, 0)),
                   pl.BlockSpec((SUBLANE, tm), lambda i: (0, i)),
                   pl.BlockSpec((tm, POOL_WIDTH), lambda i: (i, 0)),
                   pl.BlockSpec((tm, D_MODEL), lambda i: (i, 0)),
                   pl.BlockSpec((tm, D_MODEL), lambda i: (i, 0))],
        out_shape=[jax.ShapeDtypeStruct((nq, N_PAIR, LANE, 2 * TQ), bf16),
                   jax.ShapeDtypeStruct((s, ATTN_WIDTH), bf16),
                   jax.ShapeDtypeStruct((s // TK, N_HEADS, HEAD_DIM, TK), bf16),
                   jax.ShapeDtypeStruct((nq, IDX_K, IDX_HEADS * TQ), bf16),
                   jax.ShapeDtypeStruct((s, 2 * LANE), bf16),
                   jax.ShapeDtypeStruct((SUBLANE, s), f32),
                   jax.ShapeDtypeStruct((s, POOL_WIDTH), f32),
                   jax.ShapeDtypeStruct((s, D_MODEL), bf16),
                   jax.ShapeDtypeStruct((s, D_MODEL), bf16)],
        compiler_params=pltpu.CompilerParams(dimension_semantics=("parallel",),
                                             vmem_limit_bytes=VMEM_LIMIT),
        name="proj",
    )(x2, mod, g1, w_qkv, w_rest, qg, kg, widx)


def _slope2(h):
    return (2.0 ** (-8.0 * (h + 1) / N_HEADS)) * LOG2E


def _tree(op, xs):
    xs = list(xs)
    while len(xs) > 1:
        nxt = [op(xs[a], xs[a + 1]) for a in range(0, len(xs) - 1, 2)]
        if len(xs) % 2:
            nxt.append(xs[-1])
        xs = nxt
    return xs[0]


def _row_groups(x):
    return [x[j * SUBLANE:(j + 1) * SUBLANE] for j in range(x.shape[0] // SUBLANE)]


N_CNT_ACC = 8
N_STAGE = 4
ATTN_UNROLL = 8
EARLY_CANDS = (0, 1 << 30, INT_MIN + (1 << 30))
PV_ROWS = HEAD_DIM + 16


def _attn_kernel(qbd_ref, qi_ref, wi_ref, k_ref, vt_ref, ki_ref, o_ref,
                 keys_ref, cnt0_ref, pos_ref, slf_ref, il0_ref, il1_ref,
                 pre0_ref, pre1_ref, pre2_ref, pre3_ref, sm0_ref, sm1_ref, cmax0_ref, cmax1_ref,
                 p0_ref, p1_ref, alpha0_ref, alpha1_ref, m_ref, acc_ref, *, topk):
    il_ref = (il0_ref, il1_ref)
    pre_ref = (pre0_ref, pre1_ref, pre2_ref, pre3_ref)
    sm_ref = (sm0_ref, sm1_ref)
    cmax_ref = (cmax0_ref, cmax1_ref)
    p_ref = (p0_ref, p1_ref)
    alpha_ref = (alpha0_ref, alpha1_ref)
    i = pl.program_id(0)
    nsc = (i + 4) >> 2
    nch = 2 * nsc
    nbody = (i + 8) >> 3
    n_causal = (i + 2) >> 1
    n_full, n_tail = n_causal >> 2, n_causal & 3

    def chunk_start(c):
        return pl.multiple_of(c * TK, TK)

    def paired_loop(body, n, carry):
        carry = lax.fori_loop(0, n >> 1, lambda b, cr: body(2 * b + 1, body(2 * b, cr)), carry)
        return lax.fori_loop(0, n & 1, lambda b, cr: body(n - 1, cr), carry)

    def pipeline(produce, consume, carry):
        for j in range(N_STAGE):
            produce(j, j)

        def body(b, carry):
            c0 = N_STAGE * b
            for j in range(N_STAGE):
                carry = consume(c0 + j, j, carry)
                produce(c0 + N_STAGE + j, j)
            return carry

        carry = paired_loop(body, nbody - 1, carry)
        c0 = N_STAGE * (nbody - 1)
        for j in range(N_STAGE):
            carry = consume(c0 + j, j, carry)
        return carry

    @pl.when(i == 0)
    def _():
        col = lax.broadcasted_iota(i32, (TK, LANE), 1)
        row = lax.broadcasted_iota(i32, (TK, LANE), 0).astype(f32)
        pos_ref[...] = jnp.where(col < 3, row, 0.0).astype(bf16)
        term = lax.broadcasted_iota(i32, (LANE, 2 * TQ), 0)
        lane = lax.broadcasted_iota(i32, (LANE, 2 * TQ), 1)
        for p in range(N_PAIR):
            sl = jnp.where(lane < TQ, _slope2(2 * p), _slope2(2 * p + 1)) + jnp.zeros((LANE, 2 * TQ), f32)
            hi, mid, lo = _split3(sl)
            slf_ref[p] = jnp.where(term == 0, hi, jnp.where(term == 1, mid,
                                   jnp.where(term == 2, lo, 0.0))).astype(bf16)

    qi = qi_ref[0]
    w = wi_ref[...]
    d0 = (lax.broadcasted_iota(i32, (TK, TQ), 1) - lax.broadcasted_iota(i32, (TK, TQ), 0))

    def idx_matmul(u, slot):
        rows = pl.ds(pl.multiple_of(u * 2 * TK, 2 * TK), 2 * TK)
        kx = ki_ref[rows, 0:LANE]
        ky = ki_ref[rows, LANE:2 * LANE]
        il_ref[slot][...] = jnp.dot(jnp.concatenate([kx, kx, kx, ky], axis=1), qi,
                                    preferred_element_type=f32)

    def idx_keys(c, slot, half):
        r0 = chunk_start(c)
        rows = slice(half * TK, (half + 1) * TK)
        sc = jnp.maximum(il_ref[slot][rows, 0:TQ], 0.0) * w[0:1, :]
        for hh in range(1, IDX_HEADS):
            sc = sc + jnp.maximum(il_ref[slot][rows, hh * TQ:(hh + 1) * TQ], 0.0) * w[hh:hh + 1, :]
        b = lax.bitcast_convert_type(sc, i32)
        key = jnp.where(b < 0, -(b & 0x7FFFFFFF), b)
        valid = d0 >= (r0 - i * TQ)
        key = jnp.where(valid, key, INT_MIN)
        keys_ref[pl.ds(r0, TK), :] = key
        for n, cand in enumerate(EARLY_CANDS):
            cnt0_ref[n] = cnt0_ref[n] + _tree(jnp.add, _row_groups(jnp.where(key >= cand, 1, 0)))

    cnt0_ref[...] = jnp.zeros(cnt0_ref.shape, i32)

    idx_matmul(0, 0)
    idx_matmul(1, 1)

    def idx_body(b, carry):
        c0 = N_STAGE * b
        idx_keys(c0, 0, 0)
        idx_keys(c0 + 1, 0, 1)
        idx_matmul(2 * b + 2, 0)
        idx_keys(c0 + 2, 1, 0)
        idx_keys(c0 + 3, 1, 1)
        idx_matmul(2 * b + 3, 1)
        return carry

    paired_loop(idx_body, nbody - 1, 0)
    c_last = N_STAGE * (nbody - 1)
    for j in range(N_STAGE):
        idx_keys(c_last + j, j // 2, j % 2)

    def count_ge(cand):
        def add_chunk(c, accs):
            ind = jnp.where(keys_ref[pl.ds(chunk_start(c), TK), :] >= cand, 1, 0)
            for j, g in enumerate(_row_groups(ind)):
                accs[j % N_CNT_ACC] = accs[j % N_CNT_ACC] + g

        def body(b, accs):
            accs = list(accs)
            for part in range(N_STAGE):
                add_chunk(b * N_STAGE + part, accs)
            return tuple(accs)

        def tail(j, accs):
            accs = list(accs)
            add_chunk(n_full * N_STAGE + j, accs)
            return tuple(accs)

        accs = lax.fori_loop(0, n_full, body,
                             tuple(jnp.zeros((SUBLANE, TQ), i32) for _ in range(N_CNT_ACC)))
        accs = lax.fori_loop(0, n_tail, tail, accs)
        return jnp.sum(_tree(jnp.add, accs), axis=0, keepdims=True)

    def bit_step(bi, st):
        t, above = st
        cand = t + lax.shift_left(jnp.int32(1), 31 - bi)
        cnt = count_ge(cand)
        up = cnt >= topk
        return jnp.where(up, cand, t), jnp.where(up, above, cnt)

    c_zero, c_pos, c_neg = (jnp.sum(cnt0_ref[n], axis=0, keepdims=True) for n in range(len(EARLY_CANDS)))
    up0 = c_zero >= topk
    t = jnp.where(up0, 0, INT_MIN)
    above = jnp.where(up0, 0, c_zero)
    c_one = jnp.where(up0, c_pos, c_neg)
    up1 = c_one >= topk
    t = jnp.where(up1, t + (1 << 30), t)
    above = jnp.where(up1, above, c_one)
    t, above = lax.fori_loop(2, 32, bit_step, (t, above))
    t = jnp.maximum(t, INT_MIN + 1)
    r_tie = (topk - above).astype(f32)

    tri = jnp.where(lax.broadcasted_iota(i32, (TK, TK), 0) >= lax.broadcasted_iota(i32, (TK, TK), 1),
                    1.0, 0.0).astype(bf16)

    def tie_rank(c, slot):
        e = jnp.where(keys_ref[pl.ds(chunk_start(c), TK), :] == t, 1.0, 0.0).astype(bf16)
        pre_ref[slot][...] = jnp.dot(tri, e, preferred_element_type=f32)

    def mask_out(c, slot, rank):
        r0 = chunk_start(c)
        kk = keys_ref[pl.ds(r0, TK), :]
        pre = pre_ref[slot][...] + rank
        nm = jnp.where(kk > t, 0.0, jnp.where(kk == t, jnp.where(pre <= r_tie, 0.0, NEG_INF), NEG_INF))
        keys_ref[pl.ds(r0, TK), :] = lax.bitcast_convert_type(nm, i32)
        return pre[TK - 1:TK, :]

    pipeline(tie_rank, mask_out, jnp.zeros((1, TQ), f32))

    m_ref[...] = jnp.full(m_ref.shape, NEG_INF, f32)
    acc_ref[...] = jnp.zeros(acc_ref.shape, f32)
    lane2 = lax.broadcasted_iota(i32, (1, 2 * TQ), 1)
    ones_rows = jnp.ones((PV_ROWS - HEAD_DIM, TK), bf16)

    def logits(c, slot):
        rows = pl.ds(chunk_start(c), TK)
        nm = lax.bitcast_convert_type(keys_ref[rows, :], f32)
        nm2 = jnp.concatenate([nm, nm], axis=1)
        for p in range(N_PAIR):
            lhs = jnp.concatenate([k_ref[rows, p * LANE:(p + 1) * LANE], pos_ref[...]], axis=1)
            rhs = jnp.concatenate([qbd_ref[0, p], slf_ref[p]], axis=0)
            sm = jnp.dot(lhs, rhs, preferred_element_type=f32) + nm2
            sm_ref[slot][p] = sm
            cmax_ref[slot][p] = jnp.max(_tree(jnp.maximum, _row_groups(sm)), axis=0, keepdims=True)

    def probs(c, slot):
        r0f = jnp.asarray(c * TK, dtype=f32)
        for p in range(N_PAIR):
            coff = jnp.where(lane2 < TQ, _slope2(2 * p), _slope2(2 * p + 1)) * r0f
            m_old = m_ref[p]
            m_new = jnp.maximum(m_old, cmax_ref[slot][p] + coff)
            m_safe = jnp.where(m_new == NEG_INF, 0.0, m_new)
            alpha_ref[slot][p] = jnp.where(m_old == NEG_INF, 0.0, jnp.exp2(m_old - m_safe))
            p_ref[slot][p] = jnp.exp2((sm_ref[slot][p] - (m_safe - coff)).astype(bf16))
            m_ref[p] = m_new

    def weighted_sum(c, slot):
        for h in range(N_HEADS):
            lanes = slice((h % 2) * TQ, (h % 2 + 1) * TQ)
            lhs = jnp.concatenate([vt_ref[c, h], ones_rows], axis=0)
            pv = jnp.dot(lhs, p_ref[slot][h // 2, :, lanes], preferred_element_type=f32)
            acc_ref[h] = acc_ref[h] * alpha_ref[slot][h // 2, :, lanes] + pv

    logits(0, 0)
    probs(0, 0)
    logits(1, 1)

    def attn_step(c0):
        probs(c0 + 1, 1)
        logits(c0 + 2, 0)
        logits(c0 + 3, 1)
        weighted_sum(c0, 0)
        probs(c0 + 2, 0)
        weighted_sum(c0 + 1, 1)

    n_steps = nsc - 1

    def run_steps(first_step, width, trips):
        def body(b, carry):
            for j in range(width):
                attn_step(2 * (first_step + b * width + j))
            return carry
        lax.fori_loop(0, trips, body, 0)

    shift = ATTN_UNROLL.bit_length() - 1
    run_steps(0, ATTN_UNROLL, n_steps >> shift)
    done = (n_steps >> shift) << shift
    for sh in range(shift - 1, -1, -1):
        take = (n_steps >> sh) & 1
        run_steps(done, 1 << sh, take)
        done = done + (take << sh)
    weighted_sum(nch - 2, 0)
    probs(nch - 1, 1)
    weighted_sum(nch - 1, 1)

    outs = []
    for h in range(N_HEADS):
        a = acc_ref[h]
        outs.append(a[0:HEAD_DIM] / a[HEAD_DIM:HEAD_DIM + 1])
    o_ref[...] = jnp.concatenate(outs, axis=0).T.astype(bf16)


def _attn_call(qbd, qi, wit, k, vt, ki, topk):
    s = k.shape[0]
    nq = s // TQ
    whole = pl.BlockSpec(memory_space=pltpu.VMEM)
    il = [pltpu.VMEM((2 * TK, IDX_HEADS * TQ), f32)] * 2
    pre = [pltpu.VMEM((TK, TQ), f32)] * N_STAGE
    return pl.pallas_call(
        functools.partial(_attn_kernel, topk=topk),
        grid=(nq,),
        in_specs=[pl.BlockSpec((1, N_PAIR, LANE, 2 * TQ), lambda i: (i, 0, 0, 0)),
                  pl.BlockSpec((1, IDX_K, IDX_HEADS * TQ), lambda i: (i, 0, 0)),
                  pl.BlockSpec((SUBLANE, TQ), lambda i: (0, i)),
                  whole, whole, whole],
        out_specs=pl.BlockSpec((TQ, ATTN_WIDTH), lambda i: (i, 0)),
        out_shape=jax.ShapeDtypeStruct((s, ATTN_WIDTH), bf16),
        scratch_shapes=[pltpu.VMEM((s, TQ), i32),
                        pltpu.VMEM((len(EARLY_CANDS), SUBLANE, TQ), i32),
                        pltpu.VMEM((TK, LANE), bf16),
                        pltpu.VMEM((N_PAIR, LANE, 2 * TQ), bf16),
                        *il, *pre,
                        pltpu.VMEM((N_PAIR, TK, 2 * TQ), f32),
                        pltpu.VMEM((N_PAIR, TK, 2 * TQ), f32),
                        pltpu.VMEM((N_PAIR, 1, 2 * TQ), f32),
                        pltpu.VMEM((N_PAIR, 1, 2 * TQ), f32),
                        pltpu.VMEM((N_PAIR, TK, 2 * TQ), bf16),
                        pltpu.VMEM((N_PAIR, TK, 2 * TQ), bf16),
                        pltpu.VMEM((N_PAIR, 1, 2 * TQ), f32),
                        pltpu.VMEM((N_PAIR, 1, 2 * TQ), f32),
                        pltpu.VMEM((N_PAIR, 1, 2 * TQ), f32),
                        pltpu.VMEM((N_HEADS, PV_ROWS, TQ), f32)],
        compiler_params=pltpu.CompilerParams(dimension_semantics=("arbitrary",),
                                             vmem_limit_bytes=VMEM_LIMIT),
        name="attn",
    )(qbd, qi, wit, k, vt, ki)


HALO_POOL = 16


def _mix_kernel(x_ref, attn_ref, u_ref, uh_ref, ga_ref, gp_ref, mod_ref,
                wab_ref, wg_ref, ps_ref, wpb_ref, wo_ref, o_ref, *, tm):
    i = pl.program_id(0)
    y_attn = jnp.dot(attn_ref[...], wab_ref[...], preferred_element_type=f32)

    u = u_ref[...]
    halo = jnp.where(i > 0, uh_ref[...], 0.0)
    a = jnp.concatenate([halo, u], axis=0)
    tpos = (i * tm + lax.broadcasted_iota(i32, (tm, POOL_GROUP_DIM), 0) + 1).astype(f32)
    mixed = []
    for g, wdw in enumerate(POOL_WINDOWS):
        ag = a[:, g * POOL_GROUP_DIM:(g + 1) * POOL_GROUP_DIM]
        ug = ag[HALO_POOL:HALO_POOL + tm]
        ssum = ug
        for j in range(1, wdw):
            ssum = ssum + ag[HALO_POOL - j:HALO_POOL - j + tm]
        pooled = ssum / jnp.minimum(tpos, float(wdw)) - ug
        mixed.append(jnp.dot(pooled.astype(bf16), wg_ref[g], preferred_element_type=f32))
    mixed = jnp.concatenate(mixed, axis=1) * ps_ref[...]
    y_pool = jnp.dot(mixed.astype(bf16), wpb_ref[...], preferred_element_type=f32)

    merged = ga_ref[...].astype(f32) * y_attn + gp_ref[...].astype(f32) * y_pool
    o = jnp.dot(merged.astype(bf16), wo_ref[...], preferred_element_type=f32)
    gate = mod_ref[0:1, 2 * D_MODEL:3 * D_MODEL]
    o_ref[...] = x_ref[...] + gate * o


def _mix_call(x2, attn, u, ga, gp, mod, wab, wg, ps, wpb, wo, tm):
    s = x2.shape[0]
    const2 = lambda i: (0, 0)
    hb = tm // HALO_POOL
    return pl.pallas_call(
        functools.partial(_mix_kernel, tm=tm),
        grid=(s // tm,),
        in_specs=[pl.BlockSpec((tm, D_MODEL), lambda i: (i, 0)),
                  pl.BlockSpec((tm, ATTN_WIDTH), lambda i: (i, 0)),
                  pl.BlockSpec((tm, POOL_WIDTH), lambda i: (i, 0)),
                  pl.BlockSpec((HALO_POOL, POOL_WIDTH), lambda i: (jnp.maximum(i * hb - 1, 0), 0)),
                  pl.BlockSpec((tm, D_MODEL), lambda i: (i, 0)),
                  pl.BlockSpec((tm, D_MODEL), lambda i: (i, 0)),
                  pl.BlockSpec((SUBLANE, N_MOD * D_MODEL), const2),
                  pl.BlockSpec((ATTN_WIDTH, D_MODEL), const2),
                  pl.BlockSpec((len(POOL_WINDOWS), POOL_GROUP_DIM, POOL_GROUP_DIM), lambda i: (0, 0, 0)),
                  pl.BlockSpec((1, POOL_WIDTH), const2),
                  pl.BlockSpec((POOL_WIDTH, D_MODEL), const2),
                  pl.BlockSpec((D_MODEL, D_MODEL), const2)],
        out_specs=pl.BlockSpec((tm, D_MODEL), lambda i: (i, 0)),
        out_shape=jax.ShapeDtypeStruct((s, D_MODEL), f32),
        compiler_params=pltpu.CompilerParams(
            dimension_semantics=("parallel",), vmem_limit_bytes=VMEM_LIMIT,
            allow_input_fusion=[False, False, False, False, False, False, False,
                                True, True, False, True, True]),
        name="mix",
    )(x2, attn, u, u, ga, gp, mod, wab, wg, ps, wpb, wo)


HALO_CONV = 8


def _ffn_kernel(x_ref, xh_ref, mod_ref, g2_ref, wup_ref, cw_ref, cb_ref, wdn_ref, o_ref, *, tm):
    i = pl.program_id(0)
    shift = mod_ref[0:1, 3 * D_MODEL:4 * D_MODEL]
    scale = mod_ref[0:1, 4 * D_MODEL:5 * D_MODEL]
    gate = mod_ref[0:1, 5 * D_MODEL:6 * D_MODEL]
    g2 = g2_ref[...]
    x = x_ref[...]
    h = _rms_modulate(x, g2, shift, scale)
    hh = jnp.where(i > 0, _rms_modulate(xh_ref[...], g2, shift, scale), 0.0)
    ha = jnp.concatenate([hh, h], axis=0).astype(bf16)
    up = jnp.dot(ha, wup_ref[...], preferred_element_type=f32)
    cw = cw_ref[...]
    y = cb_ref[...] + cw[0:1, :] * up[HALO_CONV - 2:HALO_CONV - 2 + tm]
    y = y + cw[1:2, :] * up[HALO_CONV - 1:HALO_CONV - 1 + tm]
    y = y + cw[2:3, :] * up[HALO_CONV:HALO_CONV + tm]
    a = y[:, 0:D_FF]
    b = y[:, D_FF:2 * D_FF]
    gated = (a * _sigmoid(a)) * b
    o = jnp.dot(gated.astype(bf16), wdn_ref[...], preferred_element_type=f32)
    o_ref[...] = x + gate * o


def _ffn_call(x1, mod, g2, wup, cw, cb, wdn, tm):
    s = x1.shape[0]
    const2 = lambda i: (0, 0)
    hb = tm // HALO_CONV
    return pl.pallas_call(
        functools.partial(_ffn_kernel, tm=tm),
        grid=(s // tm,),
        in_specs=[pl.BlockSpec((tm, D_MODEL), lambda i: (i, 0)),
                  pl.BlockSpec((HALO_CONV, D_MODEL), lambda i: (jnp.maximum(i * hb - 1, 0), 0)),
                  pl.BlockSpec((SUBLANE, N_MOD * D_MODEL), const2),
                  pl.BlockSpec((1, D_MODEL), const2),
                  pl.BlockSpec((D_MODEL, 2 * D_FF), const2, pipeline_mode=pl.Buffered(1)),
                  pl.BlockSpec((3, 2 * D_FF), const2),
                  pl.BlockSpec((1, 2 * D_FF), const2),
                  pl.BlockSpec((D_FF, D_MODEL), const2, pipeline_mode=pl.Buffered(1))],
        out_specs=pl.BlockSpec((tm, D_MODEL), lambda i: (i, 0)),
        out_shape=jax.ShapeDtypeStruct((s, D_MODEL), f32),
        compiler_params=pltpu.CompilerParams(
            dimension_semantics=("parallel",), vmem_limit_bytes=VMEM_LIMIT,
            allow_input_fusion=[False, False, False, False, True, False, False, True]),
        name="ffn",
    )(x1, x1, mod, g2, wup, cw, cb, wdn)


def _pack_w_idx(w):
    return jnp.pad(w[:, IN_IDX0:IN_IDX1], ((0, 0), (0, 3 * LANE - (IN_IDX1 - IN_IDX0))))


def kernel(x, c, w_ada, b_ada, norm1_g, w_in, q_norm_g, k_norm_g, w_attn_br, w_pool_grp,
           pool_scale, w_pool_br, w_out, norm2_g, w_up, conv_w, conv_b, w_down):
    bsz, s, d = x.shape
    assert bsz == 1 and d == D_MODEL and s % (N_STAGE * TK) == 0
    depth = w_ada.shape[0]
    topk = min(TOPK_MAX, s // 4)
    x2 = x.reshape(s, d)
    c8 = jnp.pad(c, ((0, SUBLANE - bsz), (0, 0)))
    for l in range(depth):
        mod = _mod_call(c8, w_ada[l], b_ada[l].reshape(1, -1))
        qbd, k, vt, qi, ki, wit, u, ga, gp = _proj_call(
            x2, mod, norm1_g[l].reshape(1, -1),
            w_in[l][:, 0:IN_IDX0].astype(bf16), w_in[l][:, IN_IDX1:IN_END].astype(bf16),
            jnp.tile(q_norm_g[l], N_HEADS).reshape(1, -1),
            jnp.tile(k_norm_g[l], N_HEADS).reshape(1, -1), _pack_w_idx(w_in[l]), tm=512)
        attn = _attn_call(qbd, qi, wit, k, vt, ki, topk)
        x2 = _mix_call(x2, attn, u, ga, gp, mod, w_attn_br[l].astype(bf16),
                       w_pool_grp[l].astype(bf16), pool_scale[l].reshape(1, -1),
                       w_pool_br[l].astype(bf16), w_out[l].astype(bf16), tm=512)
        x2 = _ffn_call(x2, mod, norm2_g[l].reshape(1, -1), w_up[l].astype(bf16), conv_w[l],
                       conv_b[l].reshape(1, -1), w_down[l].astype(bf16), tm=512)
    return x2.reshape(bsz, s, d)
```
